```python
import math
import jax
import jax.numpy as jnp
from jax import lax
import numpy as np

D_MODEL = 1024
BATCH = 2
SEQ = 8192
DEPTH = 4
DEC_BATCH = 8
DEC_SEQ = 16
PAST_LEN = 2048

CHUNK = 64
HGRN_BLOCK = 16
N_AB = (DEPTH + 1) // 2
N_C = DEPTH // 2
H_A = 8
HD_A = 64
W_A = H_A * HD_A
LORA_W = 64
LORA_A = 64
LORA_G = 128
A_COLS = 3 * W_A + LORA_W + LORA_A + LORA_G
A_SPLITS = (W_A, 2 * W_A, 3 * W_A, 3 * W_A + LORA_W, 3 * W_A + LORA_W + LORA_A)
LNX_EPS = 64e-5
H_B = 16
HD_B = 64
W_B = H_B * HD_B
G_B = 2
N_B = 128
CONV_W = 4
XBC = W_B + 2 * G_B * N_B
B_COLS = W_B + XBC + H_B
IN_AB = A_COLS + B_COLS
MIX_AB = W_A + W_B
H_C = 8
DK_C = 128
DV_C = 128
W_C = H_C * DK_C
IN_C = 4 * W_C
D_FF = -(-8 * D_MODEL // (3 * 256)) * 256

kernel_name = 'rwkv7_mamba2_hgrn2_streaming_encoder_step'


def rms_norm(x, w, eps=1e-6):
    x32 = x.astype(jnp.float32)
    y = x32 * lax.rsqrt(jnp.mean(x32 * x32, axis=-1, keepdims=True) + eps)
    return (y * w.astype(jnp.float32)).astype(x.dtype)


def group_rms_norm(x, w, groups, eps=1e-5):
    shp = x.shape
    x32 = x.astype(jnp.float32).reshape(*shp[:-1], groups, shp[-1] // groups)
    y = x32 * lax.rsqrt(jnp.mean(x32 * x32, axis=-1, keepdims=True) + eps)
    return y.reshape(shp) * w.astype(jnp.float32)


def split_heads(u, n_heads):
    return u.astype(jnp.float32).reshape(*u.shape[:-1], n_heads, u.shape[-1] // n_heads)


def causal_dwconv(u, prev, w, b):
    up = jnp.concatenate([prev.astype(u.dtype), u], axis=1)
    y = lax.conv_general_dilated(up, w[:, None, :].astype(u.dtype), (1,), 'VALID',
                                 dimension_numbers=('NWC', 'WIO', 'NWC'),
                                 feature_group_count=u.shape[-1])
    return y + b, up[:, up.shape[1] - (CONV_W - 1):]


def rwkv7_scan(r, decay, k, v, kk, a, s0):
    def step(s, inp):
        r_t, d_t, k_t, v_t, kk_t, a_t = inp
        sa = jnp.einsum('bhvk,bhk->bhv', s, -kk_t)
        s = (s * d_t[:, :, None, :] + sa[..., None] * (kk_t * a_t)[:, :, None, :]
             + v_t[..., None] * k_t[:, :, None, :])
        return s, jnp.einsum('bhvk,bhk->bhv', s, r_t)
    xs = tuple(jnp.moveaxis(u, 1, 0) for u in (r, decay, k, v, kk, a))
    s_last, o = lax.scan(step, s0.astype(jnp.float32), xs)
    return jnp.moveaxis(o, 0, 1), s_last


def rwkv7_mixer(p, shift_prev, s0, mu, w0, w2, a0, a2, g2, k_k, k_a, r_k, lnx_w, lnx_b):
    bsz, t, _ = p.shape
    p_prev = jnp.concatenate([shift_prev[:, None, :].astype(p.dtype), p[:, :-1]], axis=1)
    pm = p + (p_prev - p) * mu
    r, k, v, xw, xa, xg = jnp.split(pm, A_SPLITS, axis=-1)
    w = -jax.nn.softplus(-(w0 + jnp.tanh(xw) @ w2)) - 0.5
    decay = jnp.exp(-jnp.exp(w.astype(jnp.float32)))
    a = jax.nn.sigmoid(a0 + xa @ a2)
    g = jax.nn.sigmoid(xg) @ g2
    kk = split_heads(k * k_k, H_A)
    kk = kk / jnp.maximum(jnp.sqrt(jnp.sum(kk * kk, axis=-1, keepdims=True)), 1e-12)
    k = k * (1.0 + (a - 1.0) * k_a)
    rh, kh, vh, ah, dh = (split_heads(u, H_A) for u in (r, k, v, a, decay))
    o, s_new = rwkv7_scan(rh, dh, kh, vh, kk, ah, s0)
    mean = jnp.mean(o, axis=-1, keepdims=True)
    var = jnp.mean(jnp.square(o - mean), axis=-1, keepdims=True)
    o = ((o - mean) * lax.rsqrt(var + LNX_EPS)).reshape(bsz, t, W_A) * lnx_w + lnx_b
    bonus = jnp.sum(rh * kh * r_k, axis=-1, keepdims=True) * vh
    out = (o + bonus.reshape(bsz, t, W_A)) * g
    return out, p[:, -1], s_new


def ssd_chunked(x, dt, a_neg, bm, cm, s0, chunk):
    bsz, t = x.shape[:2]
    nc = t // chunk
    hg = H_B // G_B
    xd = (x * dt[..., None]).reshape(bsz, nc, chunk, G_B, hg, HD_B)
    la = (dt * a_neg).reshape(bsz, nc, chunk, G_B, hg)
    bc = bm.reshape(bsz, nc, chunk, G_B, N_B)
    cc = cm.reshape(bsz, nc, chunk, G_B, N_B)
    acum = jnp.cumsum(la, axis=2)
    causal = jnp.tril(jnp.ones((chunk, chunk), bool))[None, None, :, :, None, None]
    seg = acum[:, :, :, None] - acum[:, :, None, :]
    decay_ls = jnp.where(causal, jnp.exp(jnp.where(causal, seg, 0.0)), 0.0)
    cb = jnp.einsum('bclgn,bcsgn->bclsg', cc, bc)
    y_intra = jnp.einsum('bclsg,bclsgh,bcsghp->bclghp', cb, decay_ls, xd)
    to_end = jnp.exp(acum[:, :, -1:] - acum)
    chunk_states = jnp.einsum('bclgn,bclgh,bclghp->bcghpn', bc, to_end, xd)
    chunk_decay = jnp.exp(acum[:, :, -1])

    def step(s, inp):
        st, dec = inp
        return s * dec[..., None, None] + st, s
    s_last, s_in = lax.scan(step, s0.astype(jnp.float32).reshape(bsz, G_B, hg, HD_B, N_B),
                            (jnp.moveaxis(chunk_states, 1, 0), jnp.moveaxis(chunk_decay, 1, 0)))
    s_in = jnp.moveaxis(s_in, 0, 1)
    y_inter = jnp.einsum('bclgn,bcghpn,bclgh->bclghp', cc, s_in, jnp.exp(acum))
    y = (y_intra + y_inter).reshape(bsz, t, H_B, HD_B)
    return y, s_last.reshape(bsz, H_B, HD_B, N_B)


def mamba2_mixer(p, conv_prev, s0, conv_w, conv_b, dt_bias, a_log, d_skip, norm_w, chunk):
    z, xbc, dt = jnp.split(p, (W_B, W_B + XBC), axis=-1)
    xbc_c, conv_new = causal_dwconv(xbc, conv_prev, conv_w, conv_b)
    xbc_c = jax.nn.silu(xbc_c)
    xs, bm, cm = jnp.split(xbc_c, (W_B, W_B + G_B * N_B), axis=-1)
    xs = split_heads(xs, H_B)
    dt = jax.nn.softplus(dt.astype(jnp.float32) + dt_bias.astype(jnp.float32))
    y, s_new = ssd_chunked(xs, dt, -jnp.exp(a_log.astype(jnp.float32)),
                           split_heads(bm, G_B), split_heads(cm, G_B), s0, chunk)
    y = y + d_skip[:, None] * xs
    y = y.reshape(*y.shape[:2], W_B) * jax.nn.silu(z.astype(jnp.float32))
    return group_rms_norm(y, norm_w, G_B), conv_new, s_new


def gla_chunked(q, k, v, log_f, s0, chunk):
    bsz, t = q.shape[:2]
    nc = t // chunk
    q, k, v, log_f = (u.reshape(bsz, nc, chunk, H_C, u.shape[-1]) for u in (q, k, v, log_f))
    bcum = jnp.cumsum(log_f, axis=2)
    causal = jnp.tril(jnp.ones((chunk, chunk), bool))[:, :, None, None]
    seg = bcum[:, :, :, None] - bcum[:, :, None, :]
    decay_ls = jnp.where(causal, jnp.exp(jnp.where(causal, seg, 0.0)), 0.0)
    att = jnp.einsum('bclhk,bcshk,bclshk->bchls', q, k, decay_ls)
    o_intra = jnp.einsum('bchls,bcshv->bclhv', att, v)
    chunk_states = jnp.einsum('bclhk,bclhv->bchkv', k * jnp.exp(bcum[:, :, -1:] - bcum), v)
    chunk_decay = jnp.exp(bcum[:, :, -1])

    def step(s, inp):
        st, dec = inp
        return s * dec[..., None] + st, s
    s_last, s_in = lax.scan(step, s0.astype(jnp.float32),
                            (jnp.moveaxis(chunk_states, 1, 0), jnp.moveaxis(chunk_decay, 1, 0)))
    s_in = jnp.moveaxis(s_in, 0, 1)
    o_inter = jnp.einsum('bclhk,bchkv->bclhv', q * jnp.exp(bcum), s_in)
    return (o_intra + o_inter).reshape(bsz, t, H_C, DV_C), s_last


def hgrn2_mixer(p, s0, lb, norm_w, chunk):
    q, f, i, g = jnp.split(p, 4, axis=-1)
    f = f.astype(jnp.float32)
    lb = lb.astype(jnp.float32)
    log_f = jnp.log(lb + (1.0 - lb) * jax.nn.sigmoid(f))
    k = (1.0 - lb) * jax.nn.sigmoid(-f)
    o, s_new = gla_chunked(split_heads(jax.nn.silu(q), H_C), split_heads(k, H_C),
                           split_heads(i, H_C), split_heads(log_f, H_C), s0, chunk)
    o = group_rms_norm(o.reshape(*o.shape[:2], W_C), norm_w, H_C) * jax.nn.silu(g.astype(jnp.float32))
    return o, s_new


def trunk(x, c, st_rwkv, st_shift, st_ssm, st_conv, st_hgrn,
          norm_mix_w, norm_ffn_w, norm_out_w, ada_w, ada_b,
          w_in_ab, w_out_ab, mu_a, w0, w2, a0, a2, g2, k_k, k_a, r_k, lnx_w, lnx_b,
          conv_w, conv_b, dt_bias, a_log, d_skip, norm_b_w,
          w_in_c, w_out_c, lb_param, norm_c_w,
          w_gate, w_up, w_down):
    chunk = math.gcd(CHUNK, x.shape[1])
    chunk_c = math.gcd(HGRN_BLOCK, x.shape[1])
    lb_soft = jax.nn.softmax(lb_param.astype(jnp.float32), axis=0)
    lb_all = jnp.cumsum(lb_soft, axis=0) - lb_soft[0]
    new_rwkv, new_shift, new_ssm, new_conv, new_hgrn = [], [], [], [], []
    for layer in range(DEPTH):
        j = layer // 2
        mod = jax.nn.silu(c) @ ada_w[layer] + ada_b[layer]
        sh_m, sc_m, g_m, sh_f, sc_f, g_f = jnp.split(mod[:, None, :], 6, axis=-1)
        h = rms_norm(x, norm_mix_w[layer]) * (1.0 + sc_m) + sh_m
        if layer % 2 == 0:
            proj = h @ w_in_ab[j]
            oa, shift_new, rwkv_new = rwkv7_mixer(proj[..., :A_COLS], st_shift[:, j], st_rwkv[:, j],
                                                  mu_a[j], w0[j], w2[j], a0[j], a2[j], g2[j],
                                                  k_k[j], k_a[j], r_k[j], lnx_w[j], lnx_b[j])
            ob, conv_new, ssm_new = mamba2_mixer(proj[..., A_COLS:], st_conv[:, j], st_ssm[:, j],
                                                 conv_w[j], conv_b[j], dt_bias[j], a_log[j],
                                                 d_skip[j], norm_b_w[j], chunk)
            mix = jnp.concatenate([oa.astype(x.dtype), ob.astype(x.dtype)], axis=-1) @ w_out_ab[j]
            new_rwkv.append(rwkv_new)
            new_shift.append(shift_new)
            new_ssm.append(ssm_new)
            new_conv.append(conv_new)
        else:
            oc, hgrn_new = hgrn2_mixer(h @ w_in_c[j], st_hgrn[:, j], lb_all[j], norm_c_w[j], chunk_c)
            mix = oc.astype(x.dtype) @ w_out_c[j]
            new_hgrn.append(hgrn_new)
        x = x + g_m * mix
        h = rms_norm(x, norm_ffn_w[layer]) * (1.0 + sc_f) + sh_f
        x = x + g_f * ((jax.nn.silu(h @ w_gate[layer]) * (h @ w_up[layer])) @ w_down[layer])
    y = rms_norm(x, norm_out_w)
    return (y, jnp.stack(new_rwkv, axis=1), jnp.stack(new_shift, axis=1), jnp.stack(new_ssm, axis=1),
            jnp.stack(new_conv, axis=1), jnp.stack(new_hgrn, axis=1))


def setup_inputs(seed: int = 0) -> dict:
    key = jax.random.key(seed)
    ks = iter(jax.random.split(key, 64))

    def nrm(shape, scale):
        return jax.random.normal(next(ks), shape, jnp.float32) * scale

    def unif(shape, lo, hi):
        return jax.random.uniform(next(ks), shape, jnp.float32, lo, hi)

    d = D_MODEL
    inputs = {}
    inputs['x_prompt'] = nrm((BATCH, SEQ, d), 1.0)
    inputs['x_sample'] = nrm((DEC_BATCH, DEC_SEQ, d), 1.0)
    inputs['state_rwkv'] = nrm((DEC_BATCH, N_AB, H_A, HD_A, HD_A), 0.3)
    inputs['state_rwkv_shift'] = nrm((DEC_BATCH, N_AB, A_COLS), 1.0)
    inputs['state_ssm'] = nrm((DEC_BATCH, N_AB, H_B, HD_B, N_B), 0.1)
    inputs['state_conv'] = nrm((DEC_BATCH, N_AB, CONV_W - 1, XBC), 1.0)
    inputs['state_hgrn'] = nrm((DEC_BATCH, N_C, H_C, DK_C, DV_C), 0.3)
    inputs['c_prompt'] = nrm((BATCH, d), 1.0)
    inputs['c_sample'] = nrm((DEC_BATCH, d), 1.0)
    inputs['norm_mix_w'] = 1.0 + nrm((DEPTH, d), 0.02)
    inputs['norm_ffn_w'] = 1.0 + nrm((DEPTH, d), 0.02)
    inputs['norm_out_w'] = 1.0 + nrm((d,), 0.02)
    inputs['ada_w'] = nrm((DEPTH, d, 6 * d), 0.5 * d ** -0.5)
    inputs['ada_b'] = nrm((DEPTH, 6 * d), 0.02)
    inputs['w_in_ab'] = nrm((N_AB, d, IN_AB), d ** -0.5)
    inputs['w_out_ab'] = nrm((N_AB, MIX_AB, d), MIX_AB ** -0.5)
    inputs['mu_a'] = unif((N_AB, A_COLS), 0.0, 1.0)
    inputs['w0'] = unif((N_AB, W_A), -5.0, 0.0)
    inputs['w2'] = nrm((N_AB, LORA_W, W_A), 0.1)
    inputs['a0'] = nrm((N_AB, W_A), 0.5)
    inputs['a2'] = nrm((N_AB, LORA_A, W_A), 0.1)
    inputs['g2'] = nrm((N_AB, LORA_G, W_A), LORA_G ** -0.5)
    inputs['k_k'] = 0.85 + nrm((N_AB, W_A), 0.02)
    inputs['k_a'] = 1.0 + nrm((N_AB, W_A), 0.02)
    inputs['r_k'] = nrm((N_AB, H_A, HD_A), 0.1)
    inputs['lnx_w'] = 1.0 + nrm((N_AB, W_A), 0.02)
    inputs['lnx_b'] = nrm((N_AB, W_A), 0.02)
    inputs['conv_w'] = nrm((N_AB, CONV_W, XBC), CONV_W ** -0.5)
    inputs['conv_b'] = nrm((N_AB, XBC), 0.02)
    dt0 = jnp.exp(unif((N_AB, H_B), math.log(1e-3), math.log(1e-1)))
    inputs['dt_bias'] = dt0 + jnp.log(-jnp.expm1(-dt0))
    inputs['a_log'] = jnp.log(unif((N_AB, H_B), 1.0, 16.0))
    inputs['d_skip'] = 1.0 + nrm((N_AB, H_B), 0.1)
    inputs['norm_b_w'] = 1.0 + nrm((N_AB, W_B), 0.02)
    inputs['w_in_c'] = nrm((N_C, d, IN_C), d ** -0.5)
    inputs['w_out_c'] = nrm((N_C, W_C, d), W_C ** -0.5)
    inputs['lb_param'] = nrm((N_C, W_C), 0.1)
    inputs['norm_c_w'] = 1.0 + nrm((N_C, W_C), 0.02)
    inputs['w_gate'] = nrm((DEPTH, d, D_FF), d ** -0.5)
    inputs['w_up'] = nrm((DEPTH, d, D_FF), d ** -0.5)
    inputs['w_down'] = nrm((DEPTH, D_FF, d), D_FF ** -0.5)
    return inputs


def reference(x_prompt, x_sample, state_rwkv, state_rwkv_shift, state_ssm, state_conv, state_hgrn,
              c_prompt, c_sample, norm_mix_w, norm_ffn_w, norm_out_w, ada_w, ada_b,
              w_in_ab, w_out_ab, mu_a, w0, w2, a0, a2, g2, k_k, k_a, r_k, lnx_w, lnx_b,
              conv_w, conv_b, dt_bias, a_log, d_skip, norm_b_w,
              w_in_c, w_out_c, lb_param, norm_c_w, w_gate, w_up, w_down):
    weights = (norm_mix_w, norm_ffn_w, norm_out_w, ada_w, ada_b,
               w_in_ab, w_out_ab, mu_a, w0, w2, a0, a2, g2, k_k, k_a, r_k, lnx_w, lnx_b,
               conv_w, conv_b, dt_bias, a_log, d_skip, norm_b_w,
               w_in_c, w_out_c, lb_param, norm_c_w, w_gate, w_up, w_down)
    bp = x_prompt.shape[0]
    z_rwkv = jnp.zeros((bp, N_AB, H_A, HD_A, HD_A), jnp.float32)
    z_shift = jnp.zeros((bp, N_AB, A_COLS), x_prompt.dtype)
    z_ssm = jnp.zeros((bp, N_AB, H_B, HD_B, N_B), jnp.float32)
    z_conv = jnp.zeros((bp, N_AB, CONV_W - 1, XBC), x_prompt.dtype)
    z_hgrn = jnp.zeros((bp, N_C, H_C, DK_C, DV_C), jnp.float32)
    y_prompt, p_rwkv, p_shift, p_ssm, p_conv, p_hgrn = trunk(
        x_prompt, c_prompt, z_rwkv, z_shift, z_ssm, z_conv, z_hgrn, *weights)
    y_sample, s_rwkv, s_shift, s_ssm, s_conv, s_hgrn = trunk(
        x_sample, c_sample, state_rwkv, state_rwkv_shift, state_ssm, state_conv, state_hgrn, *weights)
    return (y_prompt, y_sample, p_rwkv, p_shift, p_ssm, p_conv, p_hgrn,
            s_rwkv, s_shift, s_ssm, s_conv, s_hgrn)
```

```python
import functools
import math

import numpy as np
import jax
import jax.numpy as jnp
from jax import lax
from jax.experimental import pallas as pl
from jax.experimental.pallas import tpu as pltpu

F32 = jnp.float32
BF16 = jnp.bfloat16
HIGHEST = lax.Precision.HIGHEST

LANE = 128
VMEM_LIMIT = 56 * 2**20
ROW_TILE = 512
COL_CHUNK = 512
SCAN_CHUNK = 64
LNX_EPS = 64e-5
RMS_EPS = 1e-6
GROUP_RMS_EPS = 1e-5


def _bdot(a, b):
    return jnp.dot(a.astype(BF16), b.astype(BF16), preferred_element_type=F32)


def _bdot_nt(a, b):
    return lax.dot_general(a.astype(BF16), b.astype(BF16), (((1,), (1,)), ((), ())),
                           preferred_element_type=F32)


def _bdot_tn(a, b):
    return lax.dot_general(a.astype(BF16), b.astype(BF16), (((0,), (0,)), ((), ())),
                           preferred_element_type=F32)


def _fdot(a, b):
    return jnp.dot(a, b, precision=HIGHEST, preferred_element_type=F32)


def _silu(x):
    return x * jax.nn.sigmoid(x)


def _const_spec(shape, single_buffer=False):
    nd = len(shape)
    if single_buffer:
        return pl.BlockSpec(shape, lambda *_: (0,) * nd, pipeline_mode=pl.Buffered(1))
    return pl.BlockSpec(shape, lambda *_: (0,) * nd)


def _params(n_axes):
    return pltpu.CompilerParams(dimension_semantics=("arbitrary",) * n_axes,
                                vmem_limit_bytes=VMEM_LIMIT)


def _row_tiling(b, t):
    if t >= ROW_TILE:
        assert t % ROW_TILE == 0
        return 1, ROW_TILE
    bb = max(1, min(b, ROW_TILE // t))
    while b % bb:
        bb -= 1
    return bb, t


def _col_chunks(n):
    return [(n0, min(n0 + COL_CHUNK, n)) for n0 in range(0, n, COL_CHUNK)]


def _ada_body(c_ref, w_ref, b_ref, o_ref):
    o_ref[0] = _bdot(_silu(c_ref[...]), w_ref[0]) + b_ref[0]


def _ada_call(c_all, ada_w, ada_b):
    depth, d, n = ada_w.shape
    r = c_all.shape[0]
    tn = 1024
    assert n % tn == 0
    return pl.pallas_call(
        _ada_body,
        out_shape=jax.ShapeDtypeStruct((depth, r, n), F32),
        grid=(depth, n // tn),
        in_specs=[pl.BlockSpec((r, d), lambda l, j: (0, 0)),
                  pl.BlockSpec((1, d, tn), lambda l, j: (l, 0, j)),
                  pl.BlockSpec((1, 1, tn), lambda l, j: (l, 0, j))],
        out_specs=pl.BlockSpec((1, r, tn), lambda l, j: (l, 0, j)),
        compiler_params=_params(2),
        name="ada_mod",
    )(c_all, ada_w, ada_b.reshape(depth, 1, n))


def _norm_mod(x, nw, sc, sh):
    ms = jnp.mean(x * x, axis=-1, keepdims=True)
    y = x * lax.rsqrt(ms + RMS_EPS) * nw
    return y * (1.0 + sc) + sh


def _inproj_body(*refs, n_out):
    x_ref, nw_ref, sc_ref, sh_ref = refs[:4]
    w_refs = refs[4:4 + n_out]
    o_refs = refs[4 + n_out:]
    bb, tt, d = x_ref.shape
    h = _norm_mod(x_ref[...], nw_ref[...], sc_ref[...], sh_ref[...])
    hb = h.reshape(bb * tt, d).astype(BF16)
    for w_ref, o_ref in zip(w_refs, o_refs):
        for n0, n1 in _col_chunks(w_ref.shape[1]):
            o_ref[:, :, n0:n1] = jnp.dot(hb, w_ref[:, n0:n1],
                                         preferred_element_type=F32).reshape(bb, tt, n1 - n0)


def _inproj_call(x, nw, sc, sh, weights):
    b, t, d = x.shape
    bb, tt = _row_tiling(b, t)
    n_out = len(weights)
    return pl.pallas_call(
        functools.partial(_inproj_body, n_out=n_out),
        out_shape=[jax.ShapeDtypeStruct((b, t, w.shape[1]), F32) for w in weights],
        grid=(b // bb, t // tt),
        in_specs=[pl.BlockSpec((bb, tt, d), lambda i, j: (i, j, 0)),
                  _const_spec((1, d)),
                  pl.BlockSpec((bb, 1, d), lambda i, j: (i, 0, 0)),
                  pl.BlockSpec((bb, 1, d), lambda i, j: (i, 0, 0))]
                 + [_const_spec(w.shape, single_buffer=True) for w in weights],
        out_specs=[pl.BlockSpec((bb, tt, w.shape[1]), lambda i, j: (i, j, 0)) for w in weights],
        compiler_params=_params(2),
        name="in_proj",
    )(x, nw, sc, sh, *weights)


def _post_body(*refs, n_mix, final):
    x_ref = refs[0]
    mix_refs = refs[1:1 + n_mix]
    (wout_ref, gm_ref, nw_ref, sc_ref, sh_ref, gf_ref, wg_ref, wu_ref, wd_ref) = refs[1 + n_mix:10 + n_mix]
    rest = refs[10 + n_mix:]
    if final:
        fw_ref, o_ref, act_ref = rest
    else:
        o_ref, act_ref = rest
    bb, tt, d = x_ref.shape
    rows = bb * tt

    mix = None
    off = 0
    for m_ref in mix_refs:
        wdt = m_ref.shape[-1]
        part = jnp.dot(m_ref[...].reshape(rows, wdt).astype(BF16), wout_ref[off:off + wdt, :],
                       preferred_element_type=F32)
        mix = part if mix is None else mix + part
        off += wdt
    x1 = x_ref[...] + gm_ref[...] * mix.reshape(bb, tt, d)

    h = _norm_mod(x1, nw_ref[...], sc_ref[...], sh_ref[...])
    hb = h.reshape(rows, d).astype(BF16)
    for f0, f1 in _col_chunks(wg_ref.shape[1]):
        gate = jnp.dot(hb, wg_ref[:, f0:f1], preferred_element_type=F32)
        up = jnp.dot(hb, wu_ref[:, f0:f1], preferred_element_type=F32)
        act_ref[:, f0:f1] = (_silu(gate) * up).astype(BF16)
    ffn = jnp.dot(act_ref[...], wd_ref[...], preferred_element_type=F32)
    x2 = x1 + gf_ref[...] * ffn.reshape(bb, tt, d)
    if final:
        ms = jnp.mean(x2 * x2, axis=-1, keepdims=True)
        x2 = x2 * lax.rsqrt(ms + RMS_EPS) * fw_ref[...]
    o_ref[...] = x2


def _post_call(x, mixes, wout, gm, nw, sc, sh, gf, wg, wu, wd, final_w):
    b, t, d = x.shape
    bb, tt = _row_tiling(b, t)
    final = final_w is not None
    row_spec = lambda w: pl.BlockSpec((bb, tt, w), lambda i, j: (i, j, 0))
    seq_spec = pl.BlockSpec((bb, 1, d), lambda i, j: (i, 0, 0))
    in_specs = ([row_spec(d)] + [row_spec(m.shape[-1]) for m in mixes]
                + [_const_spec(wout.shape, True), seq_spec, _const_spec((1, d)), seq_spec, seq_spec, seq_spec,
                   _const_spec(wg.shape, True), _const_spec(wu.shape, True), _const_spec(wd.shape, True)])
    args = [x, *mixes, wout, gm, nw, sc, sh, gf, wg, wu, wd]
    if final:
        in_specs.append(_const_spec((1, d)))
        args.append(final_w)
    return pl.pallas_call(
        functools.partial(_post_body, n_mix=len(mixes), final=final),
        out_shape=jax.ShapeDtypeStruct((b, t, d), F32),
        grid=(b // bb, t // tt),
        in_specs=in_specs,
        out_specs=row_spec(d),
        scratch_shapes=[pltpu.VMEM((bb * tt, wg.shape[1]), BF16)],
        compiler_params=_params(2),
        name="post_ffn",
    )(*args)


def _tri(c, strict=False, upper=False, reps=1):
    row = lax.broadcasted_iota(jnp.int32, (c, reps * c), 0)
    col = lax.broadcasted_iota(jnp.int32, (c, reps * c), 1) & (c - 1)
    if upper:
        row, col = col, row
    return (row > col) if strict else (row >= col)


def _chunk_len(t):
    return math.gcd(SCAN_CHUNK, t)


def _unit_lower_inverse(a_strict, c):
    eye = (lax.broadcasted_iota(jnp.int32, (c, c), 0) == lax.broadcasted_iota(jnp.int32, (c, c), 1)).astype(F32)
    n = a_strict
    t = eye + n
    p = 2
    while p < c:
        n = _fdot(n, n)
        t = t + _fdot(t, n)
        p *= 2
    return t


def _rwkv_body(p_ref, shift_ref, s0_ref, mu_ref, w0_ref, w2_ref, a0_ref, a2_ref, g2_ref,
               kk_ref, ka_ref, rk_ref, lw_ref, lb_ref,
               o_ref, shift_out_ref, s_ref, prev_ref, *, heads, lora):
    c, cols = p_ref.shape[1], p_ref.shape[2]
    hd = s_ref.shape[-1]
    wa = heads * hd
    lw_, la_, lg_ = lora

    @pl.when(pl.program_id(1) == 0)
    def _():
        s_ref[...] = s0_ref[...]
        prev_ref[0:1, :] = shift_ref[0]

    p = p_ref[0]
    row = lax.broadcasted_iota(jnp.int32, (c, cols), 0)
    p_prev = jnp.where(row == 0, prev_ref[0:1, :], pltpu.roll(p, 1, axis=0))
    pm = p + (p_prev - p) * mu_ref[...]
    last = p[c - 1:c, :]
    prev_ref[0:1, :] = last
    shift_out_ref[0] = last

    r = pm[:, 0:wa]
    k = pm[:, wa:2 * wa]
    v = pm[:, 2 * wa:3 * wa]
    o1 = 3 * wa
    xw = pm[:, o1:o1 + lw_]
    xa = pm[:, o1 + lw_:o1 + lw_ + la_]
    xg = pm[:, o1 + lw_ + la_:o1 + lw_ + la_ + lg_]

    w = -jax.nn.softplus(-(w0_ref[...] + _bdot(jnp.tanh(xw), w2_ref[...]))) - 0.5
    logd = -jnp.exp(w)
    a = jax.nn.sigmoid(a0_ref[...] + _bdot(xa, a2_ref[...]))
    g = _bdot(jax.nn.sigmoid(xg), g2_ref[...])
    kk_raw = k * kk_ref[...]
    k2 = k * (1.0 + (a - 1.0) * ka_ref[...])
    rkk = r * k2 * rk_ref[...]

    incl = _tri(c)
    incl2 = _tri(c, reps=2)
    strict = _tri(c, strict=True)
    cum = _fdot(incl.astype(F32), logd)
    tot = cum[c - 1:c, :]
    p_incl = jnp.exp(cum)
    p_excl = jnp.exp(cum - logd)
    p_inv = jnp.exp(-cum)
    p_end = jnp.exp(tot - cum)
    p_tot = jnp.exp(tot)

    for h in range(heads):
        sl = slice(h * hd, (h + 1) * hd)
        kk = kk_raw[:, sl]
        kk = kk / jnp.maximum(jnp.sqrt(jnp.sum(kk * kk, axis=-1, keepdims=True)), 1e-12)
        kka = kk * a[:, sl]
        vh = v[:, sl]
        x = jnp.concatenate([-kk * p_excl[:, sl], r[:, sl] * p_incl[:, sl]], axis=0)
        y = jnp.concatenate([kka * p_inv[:, sl], k2[:, sl] * p_inv[:, sl]], axis=0)
        gram = _bdot_nt(x, y)
        a_ab = jnp.where(strict, gram[:c, :c], 0.0)
        a_ak = jnp.where(strict, gram[:c, c:], 0.0)
        a_r = jnp.where(incl2, gram[c:, :], 0.0)
        s0 = s_ref[0, h]
        xs = _bdot_nt(x, s0)
        t_inv = _unit_lower_inverse(a_ab, c)
        u = _fdot(t_inv, xs[:c] + _bdot(a_ak, vh))
        uv = jnp.concatenate([u, vh], axis=0)
        o = xs[c:] + _bdot(a_r, uv)
        ends = jnp.concatenate([kka * p_end[:, sl], k2[:, sl] * p_end[:, sl]], axis=0)
        s_ref[0, h] = s0 * p_tot[:, sl] + _bdot_tn(uv, ends)

        mean = jnp.mean(o, axis=-1, keepdims=True)
        var = jnp.mean(jnp.square(o - mean), axis=-1, keepdims=True)
        o = (o - mean) * lax.rsqrt(var + LNX_EPS) * lw_ref[:, sl] + lb_ref[:, sl]
        bonus = jnp.sum(rkk[:, sl], axis=-1, keepdims=True) * vh
        o_ref[0, :, sl] = (o + bonus) * g[:, sl]


def _rwkv_call(p, shift_prev, s0, prm):
    b, t, cols = p.shape
    heads, hd = s0.shape[1], s0.shape[2]
    wa = heads * hd
    c = _chunk_len(t)
    lora = (prm["w2"].shape[0], prm["a2"].shape[0], prm["g2"].shape[0])
    vec = lambda a: a.reshape(1, -1)
    consts = [vec(prm["mu"]), vec(prm["w0"]), prm["w2"], vec(prm["a0"]), prm["a2"], prm["g2"],
              vec(prm["k_k"]), vec(prm["k_a"]), vec(prm["r_k"]), vec(prm["lnx_w"]), vec(prm["lnx_b"])]
    return pl.pallas_call(
        functools.partial(_rwkv_body, heads=heads, lora=lora),
        out_shape=[jax.ShapeDtypeStruct((b, t, wa), F32),
                   jax.ShapeDtypeStruct((b, 1, cols), F32),
                   jax.ShapeDtypeStruct(s0.shape, F32)],
        grid=(b, t // c),
        in_specs=[pl.BlockSpec((1, c, cols), lambda i, j: (i, j, 0)),
                  pl.BlockSpec((1, 1, cols), lambda i, j: (i, 0, 0)),
                  pl.BlockSpec((1,) + s0.shape[1:], lambda i, j: (i, 0, 0, 0))]
                 + [_const_spec(a.shape) for a in consts],
        out_specs=[pl.BlockSpec((1, c, wa), lambda i, j: (i, j, 0)),
                   pl.BlockSpec((1, 1, cols), lambda i, j: (i, 0, 0)),
                   pl.BlockSpec((1,) + s0.shape[1:], lambda i, j: (i, 0, 0, 0))],
        scratch_shapes=[pltpu.VMEM((8, cols), F32)],
        compiler_params=_params(2),
        name="rwkv7_mixer",
    )(p, shift_prev.reshape(b, 1, cols), s0, *consts)


def _mamba_body(p_ref, conv_ref, s0_ref, cw_ref, cb_ref, dtb_ref, alog_ref, dskip_ref, nw_ref,
                o_ref, conv_out_ref, s_ref, ubuf_ref, *, groups, conv_w):
    c = p_ref.shape[1]
    heads, hd, ns = s_ref.shape[1], s_ref.shape[2], s_ref.shape[3]
    wb = heads * hd
    xbc = conv_ref.shape[-1]
    hpg = heads // groups
    pad = 8
    hist = conv_w - 1

    @pl.when(pl.program_id(1) == 0)
    def _():
        s_ref[...] = s0_ref[...]
        ubuf_ref[0:pad, :] = jnp.zeros((pad, xbc), F32)
        ubuf_ref[pad - hist:pad, :] = conv_ref[0]

    p = p_ref[0]
    z = p[:, 0:wb]
    u = p[:, wb:wb + xbc]
    dt_raw = p[:, wb + xbc:wb + xbc + heads]

    ubuf_ref[pad:pad + c, :] = u
    y = cb_ref[...] + cw_ref[hist:hist + 1, :] * u
    for i in range(hist):
        y = y + cw_ref[i:i + 1, :] * ubuf_ref[pad - hist + i:pad - hist + i + c, :]
    conv_out_ref[0] = ubuf_ref[pad + c - hist:pad + c, :]
    ubuf_ref[0:pad, :] = ubuf_ref[c:c + pad, :]

    xc = _silu(y)
    xs = xc[:, 0:wb]
    bm = xc[:, wb:wb + groups * ns]
    cm = xc[:, wb + groups * ns:wb + 2 * groups * ns]
    dt = jax.nn.softplus(dt_raw + dtb_ref[...])
    la = dt * (-jnp.exp(alog_ref[...]))

    incl = _tri(c)
    acum = _fdot(incl.astype(F32), la)
    acum_t = lax.dot_general(la, _tri(c, upper=True).astype(F32), (((0,), (0,)), ((), ())),
                             precision=HIGHEST, preferred_element_type=F32)
    a_last = acum[c - 1:c, :]
    e_cum = jnp.exp(acum)
    e_end = jnp.exp(a_last - acum)
    e_tot = jnp.exp(a_last)

    for gi in range(groups):
        bg = bm[:, gi * ns:(gi + 1) * ns]
        cg = cm[:, gi * ns:(gi + 1) * ns]
        cb = _bdot_nt(cg, bg)
        for hh in range(hpg):
            h = gi * hpg + hh
            sl = slice(h * hd, (h + 1) * hd)
            seg = acum[:, h:h + 1] - acum_t[h:h + 1, :]
            decay = jnp.where(incl, jnp.exp(jnp.where(incl, seg, 0.0)), 0.0)
            xh = xs[:, sl]
            xd = xh * dt[:, h:h + 1]
            s0 = s_ref[0, h]
            yh = _bdot(cb * decay, xd) + _bdot_nt(cg, s0) * e_cum[:, h:h + 1]
            s_ref[0, h] = s0 * e_tot[:, h:h + 1] + _bdot_tn(xd * e_end[:, h:h + 1], bg)
            o_ref[0, :, sl] = yh + dskip_ref[:, h:h + 1] * xh

    yv = o_ref[0] * _silu(z)
    gw = wb // groups
    for gi in range(groups):
        sl = slice(gi * gw, (gi + 1) * gw)
        yg = yv[:, sl]
        ms = jnp.mean(yg * yg, axis=-1, keepdims=True)
        o_ref[0, :, sl] = yg * lax.rsqrt(ms + GROUP_RMS_EPS) * nw_ref[:, sl]


def _mamba_call(p, conv_prev, s0, prm, groups):
    b, t, cols = p.shape
    heads, hd, ns = s0.shape[1:]
    wb = heads * hd
    hist, xbc = conv_prev.shape[1:]
    c = _chunk_len(t)
    vec = lambda a: a.reshape(1, -1)
    consts = [prm["conv_w"], vec(prm["conv_b"]), vec(prm["dt_bias"]), vec(prm["a_log"]),
              vec(prm["d_skip"]), vec(prm["norm_b_w"])]
    return pl.pallas_call(
        functools.partial(_mamba_body, groups=groups, conv_w=hist + 1),
        out_shape=[jax.ShapeDtypeStruct((b, t, wb), F32),
                   jax.ShapeDtypeStruct(conv_prev.shape, F32),
                   jax.ShapeDtypeStruct(s0.shape, F32)],
        grid=(b, t // c),
        in_specs=[pl.BlockSpec((1, c, cols), lambda i, j: (i, j, 0)),
                  pl.BlockSpec((1, hist, xbc), lambda i, j: (i, 0, 0)),
                  pl.BlockSpec((1,) + s0.shape[1:], lambda i, j: (i, 0, 0, 0))]
                 + [_const_spec(a.shape) for a in consts],
        out_specs=[pl.BlockSpec((1, c, wb), lambda i, j: (i, j, 0)),
                   pl.BlockSpec((1, hist, xbc), lambda i, j: (i, 0, 0)),
                   pl.BlockSpec((1,) + s0.shape[1:], lambda i, j: (i, 0, 0, 0))],
        scratch_shapes=[pltpu.VMEM((c + 8, xbc), F32)],
        compiler_params=_params(2),
        name="mamba2_mixer",
    )(p, conv_prev, s0, *consts)


def _hgrn_sum_matrix(c):
    t = np.arange(c)[:, None]
    j = np.arange(c)[None, :]
    blocks = [(j <= t), (j > t)]
    m = c // 2
    while m >= 1:
        mid = (t // (2 * m)) * (2 * m) + m - 1
        right = (t % (2 * m)) >= m
        blocks.append(np.where(right, (j > mid) & (j <= t), (j > t) & (j <= mid)))
        m //= 2
    return np.concatenate(blocks, axis=0).astype(np.float32)


def _split_dot(w01, x):
    wb = w01
    x1 = x.astype(BF16)
    r1 = x - x1.astype(F32)
    x2 = r1.astype(BF16)
    x3 = (r1 - x2.astype(F32)).astype(BF16)
    dot = lambda piece: jnp.dot(wb, piece, preferred_element_type=F32)
    return dot(x1) + dot(x2) + dot(x3)


def _hgrn_body(p_ref, s0_ref, sums_ref, lbp_ref, nw_ref, o_ref, s_ref, st_ref, *, layer):
    c = p_ref.shape[1]
    heads, dk, dv = s0_ref.shape[1:]
    wc = heads * dk
    nchunks = pl.num_programs(1)

    @pl.when(pl.program_id(1) == 0)
    def _():
        for h in range(heads):
            st_ref[h] = s0_ref[0, h].T

    lbp = lbp_ref[...]
    e = jnp.exp(lbp - jnp.max(lbp, axis=0, keepdims=True))
    soft = e / jnp.sum(e, axis=0, keepdims=True)
    lb = soft[0:1, :]
    for i in range(1, layer + 1):
        lb = lb + soft[i:i + 1, :]
    lb = lb - soft[0:1, :]

    p = p_ref[0]
    q = _silu(p[:, 0:wc])
    f = p[:, wc:2 * wc]
    v = p[:, 2 * wc:3 * wc]
    gate = p[:, 3 * wc:4 * wc]
    log_f = jnp.log(lb + (1.0 - lb) * jax.nn.sigmoid(f))
    k = (1.0 - lb) * jax.nn.sigmoid(-f)

    sums = _split_dot(sums_ref[...], log_f)
    bcum = sums[0:c]
    qe = q * jnp.exp(bcum)
    ke = k * jnp.exp(sums[c:2 * c])
    f_tot = jnp.exp(bcum[c - 1:c, :])

    row = lax.broadcasted_iota(jnp.int32, (c, 1), 0)
    rr = lax.broadcasted_iota(jnp.int32, (c, c), 0)
    cc = lax.broadcasted_iota(jnp.int32, (c, c), 1)
    levels = []
    m = c // 2
    i = 2
    while m >= 1:
        wgt = jnp.exp(sums[i * c:(i + 1) * c])
        right = (row & (2 * m - 1)) >= m
        levels.append((jnp.where(right, q * wgt, 0.0).astype(BF16),
                       jnp.where(right, 0.0, k * wgt).astype(BF16),
                       (rr ^ cc) < 2 * m))
        m //= 2
        i += 1
    qk = q * k
    eye = rr == cc

    for h in range(heads):
        sl = slice(h * dk, (h + 1) * dk)
        att = jnp.where(eye, jnp.sum(qk[:, sl], axis=-1, keepdims=True), 0.0)
        for qm, km, same in levels:
            att = att + jnp.where(same, _bdot_nt(qm[:, sl], km[:, sl]), 0.0)
        vh = v[:, sl]
        st = st_ref[h]
        o = _bdot(att, vh) + _bdot_nt(qe[:, sl], st)
        st_ref[h] = st * f_tot[:, sl] + _bdot_tn(vh, ke[:, sl])
        ms = jnp.mean(o * o, axis=-1, keepdims=True)
        o_ref[0, :, sl] = o * lax.rsqrt(ms + GROUP_RMS_EPS) * nw_ref[:, sl] * _silu(gate[:, sl])

    @pl.when(pl.program_id(1) == nchunks - 1)
    def _():
        for h in range(heads):
            s_ref[0, h] = st_ref[h].T


def _hgrn_call(p, s0, lb_param, norm_w, layer):
    b, t, cols = p.shape
    heads, dk, dv = s0.shape[1:]
    assert dk == dv
    wc = heads * dk
    c = _chunk_len(t)
    sums = jnp.asarray(_hgrn_sum_matrix(c), dtype=BF16)
    return pl.pallas_call(
        functools.partial(_hgrn_body, layer=layer),
        out_shape=[jax.ShapeDtypeStruct((b, t, wc), F32),
                   jax.ShapeDtypeStruct(s0.shape, F32)],
        grid=(b, t // c),
        in_specs=[pl.BlockSpec((1, c, cols), lambda i, j: (i, j, 0)),
                  pl.BlockSpec((1,) + s0.shape[1:], lambda i, j: (i, 0, 0, 0)),
                  _const_spec(sums.shape), _const_spec(lb_param.shape), _const_spec((1, wc))],
        out_specs=[pl.BlockSpec((1, c, wc), lambda i, j: (i, j, 0)),
                   pl.BlockSpec((1,) + s0.shape[1:], lambda i, j: (i, 0, 0, 0))],
        scratch_shapes=[pltpu.VMEM((heads, dv, dk), F32)],
        compiler_params=_params(2),
        name="hgrn2_mixer",
    )(p, s0, sums, lb_param, norm_w.reshape(1, wc))


def _round_up(n, m):
    return -(-n // m) * m


def _prepare_weights(w):
    a_cols = w["mu_a"].shape[1]
    b_cols = w["w_in_ab"].shape[2] - a_cols
    b_pad = _round_up(b_cols, LANE) - b_cols
    return dict(
        w_in_a=w["w_in_ab"][:, :, :a_cols].astype(BF16),
        w_in_b=jnp.pad(w["w_in_ab"][:, :, a_cols:], ((0, 0), (0, 0), (0, b_pad))).astype(BF16),
        w_out_ab=w["w_out_ab"].astype(BF16),
        w_in_c=w["w_in_c"].astype(BF16),
        w_out_c=w["w_out_c"].astype(BF16),
        w_gate=w["w_gate"].astype(BF16),
        w_up=w["w_up"].astype(BF16),
        w_down=w["w_down"].astype(BF16),
    )


def _trunk(x, mod, st_rwkv, st_shift, st_ssm, st_conv, st_hgrn, w, wb):
    depth = mod.shape[0]
    b, t, d = x.shape
    groups = (st_conv.shape[-1] - st_ssm.shape[2] * st_ssm.shape[3]) // (2 * st_ssm.shape[4])
    new_rwkv, new_shift, new_ssm, new_conv, new_hgrn = [], [], [], [], []
    for layer in range(depth):
        j = layer // 2
        sh_m, sc_m, g_m, sh_f, sc_f, g_f = (mod[layer, :, None, i * d:(i + 1) * d] for i in range(6))
        nw_mix = w["norm_mix_w"][layer].reshape(1, d)
        nw_ffn = w["norm_ffn_w"][layer].reshape(1, d)
        if layer % 2 == 0:
            pa, pb = _inproj_call(x, nw_mix, sc_m, sh_m, [wb["w_in_a"][j], wb["w_in_b"][j]])
            prm = dict(mu=w["mu_a"][j], w0=w["w0"][j], w2=w["w2"][j], a0=w["a0"][j], a2=w["a2"][j],
                       g2=w["g2"][j], k_k=w["k_k"][j], k_a=w["k_a"][j], r_k=w["r_k"][j],
                       lnx_w=w["lnx_w"][j], lnx_b=w["lnx_b"][j])
            oa, shift_new, rwkv_new = _rwkv_call(pa, st_shift[:, j], st_rwkv[:, j], prm)
            prm_b = dict(conv_w=w["conv_w"][j], conv_b=w["conv_b"][j], dt_bias=w["dt_bias"][j],
                         a_log=w["a_log"][j], d_skip=w["d_skip"][j], norm_b_w=w["norm_b_w"][j])
            ob, conv_new, ssm_new = _mamba_call(pb, st_conv[:, j], st_ssm[:, j], prm_b, groups)
            mixes, wout = [oa, ob], wb["w_out_ab"][j]
            new_rwkv.append(rwkv_new)
            new_shift.append(shift_new[:, 0])
            new_ssm.append(ssm_new)
            new_conv.append(conv_new)
        else:
            (pc,) = _inproj_call(x, nw_mix, sc_m, sh_m, [wb["w_in_c"][j]])
            oc, hgrn_new = _hgrn_call(pc, st_hgrn[:, j], w["lb_param"], w["norm_c_w"][j], j)
            mixes, wout = [oc], wb["w_out_c"][j]
            new_hgrn.append(hgrn_new)
        final_w = w["norm_out_w"].reshape(1, d) if layer == depth - 1 else None
        x = _post_call(x, mixes, wout, g_m, nw_ffn, sc_f, sh_f, g_f,
                       wb["w_gate"][layer], wb["w_up"][layer], wb["w_down"][layer], final_w)
    return (x, jnp.stack(new_rwkv, axis=1), jnp.stack(new_shift, axis=1), jnp.stack(new_ssm, axis=1),
            jnp.stack(new_conv, axis=1), jnp.stack(new_hgrn, axis=1))


def _run(x_prompt, x_sample, state_rwkv, state_rwkv_shift, state_ssm, state_conv, state_hgrn,
         c_prompt, c_sample, w):
    bp, bs = x_prompt.shape[0], x_sample.shape[0]
    rows = _round_up(bp + bs, 8)
    c_all = jnp.pad(jnp.concatenate([c_prompt, c_sample], axis=0), ((0, rows - bp - bs), (0, 0)))
    mod = _ada_call(c_all, w["ada_w"], w["ada_b"])
    wb = _prepare_weights(w)
    zeros = lambda s: jnp.zeros((bp,) + s.shape[1:], F32)
    outs_p = _trunk(x_prompt, mod[:, :bp], zeros(state_rwkv), zeros(state_rwkv_shift), zeros(state_ssm),
                    zeros(state_conv), zeros(state_hgrn), w, wb)
    outs_s = _trunk(x_sample, mod[:, bp:bp + bs], state_rwkv, state_rwkv_shift, state_ssm,
                    state_conv, state_hgrn, w, wb)
    return (outs_p[0], outs_s[0]) + outs_p[1:] + outs_s[1:]


def kernel(x_prompt, x_sample, state_rwkv, state_rwkv_shift, state_ssm, state_conv, state_hgrn, c_prompt, c_sample, norm_mix_w, norm_ffn_w, norm_out_w, ada_w, ada_b, w_in_ab, w_out_ab, mu_a, w0, w2, a0, a2, g2, k_k, k_a, r_k, lnx_w, lnx_b, conv_w, conv_b, dt_bias, a_log, d_skip, norm_b_w, w_in_c, w_out_c, lb_param, norm_c_w, w_gate, w_up, w_down):
    w = dict(norm_mix_w=norm_mix_w, norm_ffn_w=norm_ffn_w, norm_out_w=norm_out_w, ada_w=ada_w, ada_b=ada_b,
             w_in_ab=w_in_ab, w_out_ab=w_out_ab, mu_a=mu_a, w0=w0, w2=w2, a0=a0, a2=a2, g2=g2, k_k=k_k,
             k_a=k_a, r_k=r_k, lnx_w=lnx_w, lnx_b=lnx_b, conv_w=conv_w, conv_b=conv_b, dt_bias=dt_bias,
             a_log=a_log, d_skip=d_skip, norm_b_w=norm_b_w, w_in_c=w_in_c, w_out_c=w_out_c,
             lb_param=lb_param, norm_c_w=norm_c_w, w_gate=w_gate, w_up=w_up, w_down=w_down)
    return _run(x_prompt, x_sample, state_rwkv, state_rwkv_shift, state_ssm, state_conv, state_hgrn,
                c_prompt, c_sample, w)
```

```python
import functools
import math

import numpy as np
import jax
import jax.numpy as jnp
from jax import lax
from jax.experimental import pallas as pl
from jax.experimental.pallas import tpu as pltpu

F32 = jnp.float32
BF16 = jnp.bfloat16
HIGHEST = lax.Precision.HIGHEST

LANE = 128
VMEM_LIMIT = 56 * 2**20
ROW_TILE = 512
COL_CHUNK = 512
SCAN_CHUNK = 64
LNX_EPS = 64e-5
RMS_EPS = 1e-6
GROUP_RMS_EPS = 1e-5


def _bdot(a, b):
    return jnp.dot(a.astype(BF16), b.astype(BF16), preferred_element_type=F32)


def _bdot_nt(a, b):
    return lax.dot_general(a.astype(BF16), b.astype(BF16), (((1,), (1,)), ((), ())),
                           preferred_element_type=F32)


def _bdot_tn(a, b):
    return lax.dot_general(a.astype(BF16), b.astype(BF16), (((0,), (0,)), ((), ())),
                           preferred_element_type=F32)


def _fdot(a, b):
    return jnp.dot(a, b, precision=HIGHEST, preferred_element_type=F32)


def _cdot(a, b):
    return _bdot(a, b)


def _silu(x):
    return x * jax.nn.sigmoid(x)


def _const_spec(shape, single_buffer=False):
    nd = len(shape)
    if single_buffer:
        return pl.BlockSpec(shape, lambda *_: (0,) * nd, pipeline_mode=pl.Buffered(1))
    return pl.BlockSpec(shape, lambda *_: (0,) * nd)


def _params(n_axes):
    return pltpu.CompilerParams(dimension_semantics=("arbitrary",) * n_axes,
                                vmem_limit_bytes=VMEM_LIMIT)


def _row_tiling(b, t):
    if t >= ROW_TILE:
        assert t % ROW_TILE == 0
        return 1, ROW_TILE
    bb = max(1, min(b, ROW_TILE // t))
    while b % bb:
        bb -= 1
    return bb, t


def _col_chunks(n):
    return [(n0, min(n0 + COL_CHUNK, n)) for n0 in range(0, n, COL_CHUNK)]


def _ada_body(c_ref, w_ref, b_ref, o_ref):
    o_ref[0] = _bdot(_silu(c_ref[...]), w_ref[0]) + b_ref[0]


def _ada_call(c_all, ada_w, ada_b):
    depth, d, n = ada_w.shape
    r = c_all.shape[0]
    tn = 1024
    assert n % tn == 0
    return pl.pallas_call(
        _ada_body,
        out_shape=jax.ShapeDtypeStruct((depth, r, n), F32),
        grid=(depth, n // tn),
        in_specs=[pl.BlockSpec((r, d), lambda l, j: (0, 0)),
                  pl.BlockSpec((1, d, tn), lambda l, j: (l, 0, j)),
                  pl.BlockSpec((1, 1, tn), lambda l, j: (l, 0, j))],
        out_specs=pl.BlockSpec((1, r, tn), lambda l, j: (l, 0, j)),
        compiler_params=_params(2),
        name="ada_mod",
    )(c_all, ada_w, ada_b.reshape(depth, 1, n))


def _norm_mod(x, nw, sc, sh):
    ms = jnp.mean(x * x, axis=-1, keepdims=True)
    y = x * lax.rsqrt(ms + RMS_EPS) * nw
    return y * (1.0 + sc) + sh


def _inproj_body(*refs, n_out):
    x_ref, nw_ref, sc_ref, sh_ref = refs[:4]
    w_refs = refs[4:4 + n_out]
    o_refs = refs[4 + n_out:]
    bb, tt, d = x_ref.shape
    h = _norm_mod(x_ref[...], nw_ref[...], sc_ref[...], sh_ref[...])
    hb = h.reshape(bb * tt, d).astype(BF16)
    for w_ref, o_ref in zip(w_refs, o_refs):
        for n0, n1 in _col_chunks(w_ref.shape[1]):
            o_ref[:, :, n0:n1] = jnp.dot(hb, w_ref[:, n0:n1],
                                         preferred_element_type=F32).reshape(bb, tt, n1 - n0)


def _inproj_call(x, nw, sc, sh, weights):
    b, t, d = x.shape
    bb, tt = _row_tiling(b, t)
    n_out = len(weights)
    return pl.pallas_call(
        functools.partial(_inproj_body, n_out=n_out),
        out_shape=[jax.ShapeDtypeStruct((b, t, w.shape[1]), F32) for w in weights],
        grid=(b // bb, t // tt),
        in_specs=[pl.BlockSpec((bb, tt, d), lambda i, j: (i, j, 0)),
                  _const_spec((1, d)),
                  pl.BlockSpec((bb, 1, d), lambda i, j: (i, 0, 0)),
                  pl.BlockSpec((bb, 1, d), lambda i, j: (i, 0, 0))]
                 + [_const_spec(w.shape, single_buffer=True) for w in weights],
        out_specs=[pl.BlockSpec((bb, tt, w.shape[1]), lambda i, j: (i, j, 0)) for w in weights],
        compiler_params=_params(2),
        name="in_proj",
    )(x, nw, sc, sh, *weights)


def _post_body(*refs, n_mix, final):
    x_ref = refs[0]
    mix_refs = refs[1:1 + n_mix]
    (wout_ref, gm_ref, nw_ref, sc_ref, sh_ref, gf_ref, wg_ref, wu_ref, wd_ref) = refs[1 + n_mix:10 + n_mix]
    rest = refs[10 + n_mix:]
    if final:
        fw_ref, o_ref, act_ref = rest
    else:
        o_ref, act_ref = rest
    bb, tt, d = x_ref.shape
    rows = bb * tt

    mix = None
    off = 0
    for m_ref in mix_refs:
        wdt = m_ref.shape[-1]
        part = jnp.dot(m_ref[...].reshape(rows, wdt).astype(BF16), wout_ref[off:off + wdt, :],
                       preferred_element_type=F32)
        mix = part if mix is None else mix + part
        off += wdt
    x1 = x_ref[...] + gm_ref[...] * mix.reshape(bb, tt, d)

    h = _norm_mod(x1, nw_ref[...], sc_ref[...], sh_ref[...])
    hb = h.reshape(rows, d).astype(BF16)
    for f0, f1 in _col_chunks(wg_ref.shape[1]):
        gate = jnp.dot(hb, wg_ref[:, f0:f1], preferred_element_type=F32)
        up = jnp.dot(hb, wu_ref[:, f0:f1], preferred_element_type=F32)
        act_ref[:, f0:f1] = (_silu(gate) * up).astype(BF16)
    ffn = jnp.dot(act_ref[...], wd_ref[...], preferred_element_type=F32)
    x2 = x1 + gf_ref[...] * ffn.reshape(bb, tt, d)
    if final:
        ms = jnp.mean(x2 * x2, axis=-1, keepdims=True)
        x2 = x2 * lax.rsqrt(ms + RMS_EPS) * fw_ref[...]
    o_ref[...] = x2


def _post_call(x, mixes, wout, gm, nw, sc, sh, gf, wg, wu, wd, final_w):
    b, t, d = x.shape
    bb, tt = _row_tiling(b, t)
    final = final_w is not None
    row_spec = lambda w: pl.BlockSpec((bb, tt, w), lambda i, j: (i, j, 0))
    seq_spec = pl.BlockSpec((bb, 1, d), lambda i, j: (i, 0, 0))
    in_specs = ([row_spec(d)] + [row_spec(m.shape[-1]) for m in mixes]
                + [_const_spec(wout.shape, True), seq_spec, _const_spec((1, d)), seq_spec, seq_spec, seq_spec,
                   _const_spec(wg.shape, True), _const_spec(wu.shape, True), _const_spec(wd.shape, True)])
    args = [x, *mixes, wout, gm, nw, sc, sh, gf, wg, wu, wd]
    if final:
        in_specs.append(_const_spec((1, d)))
        args.append(final_w)
    return pl.pallas_call(
        functools.partial(_post_body, n_mix=len(mixes), final=final),
        out_shape=jax.ShapeDtypeStruct((b, t, d), F32),
        grid=(b // bb, t // tt),
        in_specs=in_specs,
        out_specs=row_spec(d),
        scratch_shapes=[pltpu.VMEM((bb * tt, wg.shape[1]), BF16)],
        compiler_params=_params(2),
        name="post_ffn",
    )(*args)


def _tri(c, strict=False, upper=False, reps=1):
    row = lax.broadcasted_iota(jnp.int32, (c, reps * c), 0)
    col = lax.broadcasted_iota(jnp.int32, (c, reps * c), 1) & (c - 1)
    if upper:
        row, col = col, row
    return (row > col) if strict else (row >= col)


def _chunk_len(t):
    return math.gcd(SCAN_CHUNK, t)


def _unit_lower_inverse(a_list, c):
    eye = (lax.broadcasted_iota(jnp.int32, (c, c), 0) == lax.broadcasted_iota(jnp.int32, (c, c), 1)).astype(F32)
    ns = list(a_list)
    ts = [eye + n for n in ns]
    p = 2
    while p < c:
        ns = [_cdot(n, n) for n in ns]
        ts = [t + _cdot(t, n) for t, n in zip(ts, ns)]
        p *= 2
    return ts


def _rwkv_body(p_ref, shift_ref, s0_ref, mu_ref, w0_ref, w2_ref, a0_ref, a2_ref, g2_ref,
               kk_ref, ka_ref, rk_ref, lw_ref, lb_ref,
               o_ref, shift_out_ref, s_ref, prev_ref, *, heads, lora):
    c, cols = p_ref.shape[1], p_ref.shape[2]
    hd = s_ref.shape[-1]
    wa = heads * hd
    lw_, la_, lg_ = lora

    @pl.when(pl.program_id(1) == 0)
    def _():
        s_ref[...] = s0_ref[...]
        prev_ref[0:1, :] = shift_ref[0]

    p = p_ref[0]
    row = lax.broadcasted_iota(jnp.int32, (c, cols), 0)
    p_prev = jnp.where(row == 0, prev_ref[0:1, :], pltpu.roll(p, 1, axis=0))
    pm = p + (p_prev - p) * mu_ref[...]
    last = p[c - 1:c, :]
    prev_ref[0:1, :] = last
    shift_out_ref[0] = last

    r = pm[:, 0:wa]
    k = pm[:, wa:2 * wa]
    v = pm[:, 2 * wa:3 * wa]
    o1 = 3 * wa
    xw = pm[:, o1:o1 + lw_]
    xa = pm[:, o1 + lw_:o1 + lw_ + la_]
    xg = pm[:, o1 + lw_ + la_:o1 + lw_ + la_ + lg_]

    w = -jax.nn.softplus(-(w0_ref[...] + _bdot(jnp.tanh(xw), w2_ref[...]))) - 0.5
    logd = -jnp.exp(w)
    a = jax.nn.sigmoid(a0_ref[...] + _bdot(xa, a2_ref[...]))
    g = _bdot(jax.nn.sigmoid(xg), g2_ref[...])
    kk_raw = k * kk_ref[...]
    k2 = k * (1.0 + (a - 1.0) * ka_ref[...])
    rkk = r * k2 * rk_ref[...]

    incl = _tri(c)
    incl2 = _tri(c, reps=2)
    strict = _tri(c, strict=True)
    cum = _fdot(incl.astype(F32), logd)
    tot = cum[c - 1:c, :]
    p_incl = jnp.exp(cum)
    p_excl = jnp.exp(cum - logd)
    p_inv = jnp.exp(-cum)
    p_end = jnp.exp(tot - cum)
    p_tot = jnp.exp(tot)

    sls = [slice(h * hd, (h + 1) * hd) for h in range(heads)]
    xs_in, ys_in, ends, vs = [], [], [], []
    for sl in sls:
        kk = kk_raw[:, sl]
        kk = kk / jnp.maximum(jnp.sqrt(jnp.sum(kk * kk, axis=-1, keepdims=True)), 1e-12)
        kka = kk * a[:, sl]
        xs_in.append(jnp.concatenate([-kk * p_excl[:, sl], r[:, sl] * p_incl[:, sl]], axis=0).astype(BF16))
        ys_in.append(jnp.concatenate([kka * p_inv[:, sl], k2[:, sl] * p_inv[:, sl]], axis=0).astype(BF16))
        ends.append(jnp.concatenate([kka * p_end[:, sl], k2[:, sl] * p_end[:, sl]], axis=0).astype(BF16))
        vs.append(v[:, sl])
    s0s = [s_ref[0, h] for h in range(heads)]
    grams = [_bdot_nt(x, y) for x, y in zip(xs_in, ys_in)]
    xss = [_bdot_nt(x, s0) for x, s0 in zip(xs_in, s0s)]
    a_abs = [jnp.where(strict, gm[:c, :c], 0.0) for gm in grams]
    ws = [xs[:c] + _bdot(jnp.where(strict, gm[:c, c:], 0.0), vh) for xs, gm, vh in zip(xss, grams, vs)]
    t_invs = _unit_lower_inverse(a_abs, c)
    us = [_cdot(t_inv, w_) for t_inv, w_ in zip(t_invs, ws)]
    uvs = [jnp.concatenate([u, vh], axis=0).astype(BF16) for u, vh in zip(us, vs)]
    outs = [xs[c:] + _bdot(jnp.where(incl2, gm[c:, :], 0.0), uv) for xs, gm, uv in zip(xss, grams, uvs)]
    news = [s0 * p_tot[:, sl] + _bdot_tn(uv, en) for s0, sl, uv, en in zip(s0s, sls, uvs, ends)]
    for h, sl in enumerate(sls):
        s_ref[0, h] = news[h]
        o = outs[h]
        mean = jnp.mean(o, axis=-1, keepdims=True)
        var = jnp.mean(jnp.square(o - mean), axis=-1, keepdims=True)
        o = (o - mean) * lax.rsqrt(var + LNX_EPS) * lw_ref[:, sl] + lb_ref[:, sl]
        bonus = jnp.sum(rkk[:, sl], axis=-1, keepdims=True) * vs[h]
        o_ref[0, :, sl] = (o + bonus) * g[:, sl]


def _rwkv_call(p, shift_prev, s0, prm):
    b, t, cols = p.shape
    heads, hd = s0.shape[1], s0.shape[2]
    wa = heads * hd
    c = _chunk_len(t)
    lora = (prm["w2"].shape[0], prm["a2"].shape[0], prm["g2"].shape[0])
    vec = lambda a: a.reshape(1, -1)
    consts = [vec(prm["mu"]), vec(prm["w0"]), prm["w2"], vec(prm["a0"]), prm["a2"], prm["g2"],
              vec(prm["k_k"]), vec(prm["k_a"]), vec(prm["r_k"]), vec(prm["lnx_w"]), vec(prm["lnx_b"])]
    return pl.pallas_call(
        functools.partial(_rwkv_body, heads=heads, lora=lora),
        out_shape=[jax.ShapeDtypeStruct((b, t, wa), F32),
                   jax.ShapeDtypeStruct((b, 1, cols), F32),
                   jax.ShapeDtypeStruct(s0.shape, F32)],
        grid=(b, t // c),
        in_specs=[pl.BlockSpec((1, c, cols), lambda i, j: (i, j, 0)),
                  pl.BlockSpec((1, 1, cols), lambda i, j: (i, 0, 0)),
                  pl.BlockSpec((1,) + s0.shape[1:], lambda i, j: (i, 0, 0, 0))]
                 + [_const_spec(a.shape) for a in consts],
        out_specs=[pl.BlockSpec((1, c, wa), lambda i, j: (i, j, 0)),
                   pl.BlockSpec((1, 1, cols), lambda i, j: (i, 0, 0)),
                   pl.BlockSpec((1,) + s0.shape[1:], lambda i, j: (i, 0, 0, 0))],
        scratch_shapes=[pltpu.VMEM((8, cols), F32)],
        compiler_params=_params(2),
        name="rwkv7_mixer",
    )(p, shift_prev.reshape(b, 1, cols), s0, *consts)


def _mamba_body(p_ref, conv_ref, s0_ref, cw_ref, cb_ref, dtb_ref, alog_ref, dskip_ref, nw_ref,
                o_ref, conv_out_ref, s_ref, ubuf_ref, *, groups, conv_w):
    c = p_ref.shape[1]
    heads, hd, ns = s_ref.shape[1], s_ref.shape[2], s_ref.shape[3]
    wb = heads * hd
    xbc = conv_ref.shape[-1]
    hpg = heads // groups
    pad = 8
    hist = conv_w - 1

    @pl.when(pl.program_id(1) == 0)
    def _():
        s_ref[...] = s0_ref[...]
        ubuf_ref[0:pad, :] = jnp.zeros((pad, xbc), F32)
        ubuf_ref[pad - hist:pad, :] = conv_ref[0]

    p = p_ref[0]
    z = p[:, 0:wb]
    u = p[:, wb:wb + xbc]
    dt_raw = p[:, wb + xbc:wb + xbc + heads]

    ubuf_ref[pad:pad + c, :] = u
    y = cb_ref[...] + cw_ref[hist:hist + 1, :] * u
    for i in range(hist):
        y = y + cw_ref[i:i + 1, :] * ubuf_ref[pad - hist + i:pad - hist + i + c, :]
    conv_out_ref[0] = ubuf_ref[pad + c - hist:pad + c, :]
    ubuf_ref[0:pad, :] = ubuf_ref[c:c + pad, :]

    xc = _silu(y)
    xs = xc[:, 0:wb]
    bm = xc[:, wb:wb + groups * ns]
    cm = xc[:, wb + groups * ns:wb + 2 * groups * ns]
    dt = jax.nn.softplus(dt_raw + dtb_ref[...])
    la = dt * (-jnp.exp(alog_ref[...]))

    incl = _tri(c)
    acum = _fdot(incl.astype(F32), la)
    acum_t = lax.dot_general(la, _tri(c, upper=True).astype(F32), (((0,), (0,)), ((), ())),
                             precision=HIGHEST, preferred_element_type=F32)
    a_last = acum[c - 1:c, :]
    e_cum = jnp.exp(acum)
    e_end = jnp.exp(a_last - acum)
    e_tot = jnp.exp(a_last)

    bgs = [bm[:, gi * ns:(gi + 1) * ns].astype(BF16) for gi in range(groups)]
    cgs = [cm[:, gi * ns:(gi + 1) * ns].astype(BF16) for gi in range(groups)]
    cbs = [_bdot_nt(cg, bg) for cg, bg in zip(cgs, bgs)]
    hs = range(heads)
    sls = [slice(h * hd, (h + 1) * hd) for h in hs]
    xhs = [xs[:, sl] for sl in sls]
    xds = [xh * dt[:, h:h + 1] for h, xh in zip(hs, xhs)]
    s0s = [s_ref[0, h] for h in hs]
    mats = []
    for h in hs:
        seg = acum[:, h:h + 1] - acum_t[h:h + 1, :]
        decay = jnp.where(incl, jnp.exp(jnp.where(incl, seg, 0.0)), 0.0)
        mats.append((cbs[h // hpg] * decay).astype(BF16))
    intra = [_bdot(m, xd) for m, xd in zip(mats, xds)]
    inter = [_bdot_nt(cgs[h // hpg], s0s[h]) for h in hs]
    upd = [_bdot_tn(xds[h] * e_end[:, h:h + 1], bgs[h // hpg]) for h in hs]
    for h, sl in zip(hs, sls):
        s_ref[0, h] = s0s[h] * e_tot[:, h:h + 1] + upd[h]
        o_ref[0, :, sl] = intra[h] + inter[h] * e_cum[:, h:h + 1] + dskip_ref[:, h:h + 1] * xhs[h]

    yv = o_ref[0] * _silu(z)
    gw = wb // groups
    for gi in range(groups):
        sl = slice(gi * gw, (gi + 1) * gw)
        yg = yv[:, sl]
        ms = jnp.mean(yg * yg, axis=-1, keepdims=True)
        o_ref[0, :, sl] = yg * lax.rsqrt(ms + GROUP_RMS_EPS) * nw_ref[:, sl]


def _mamba_call(p, conv_prev, s0, prm, groups):
    b, t, cols = p.shape
    heads, hd, ns = s0.shape[1:]
    wb = heads * hd
    hist, xbc = conv_prev.shape[1:]
    c = _chunk_len(t)
    vec = lambda a: a.reshape(1, -1)
    consts = [prm["conv_w"], vec(prm["conv_b"]), vec(prm["dt_bias"]), vec(prm["a_log"]),
              vec(prm["d_skip"]), vec(prm["norm_b_w"])]
    return pl.pallas_call(
        functools.partial(_mamba_body, groups=groups, conv_w=hist + 1),
        out_shape=[jax.ShapeDtypeStruct((b, t, wb), F32),
                   jax.ShapeDtypeStruct(conv_prev.shape, F32),
                   jax.ShapeDtypeStruct(s0.shape, F32)],
        grid=(b, t // c),
        in_specs=[pl.BlockSpec((1, c, cols), lambda i, j: (i, j, 0)),
                  pl.BlockSpec((1, hist, xbc), lambda i, j: (i, 0, 0)),
                  pl.BlockSpec((1,) + s0.shape[1:], lambda i, j: (i, 0, 0, 0))]
                 + [_const_spec(a.shape) for a in consts],
        out_specs=[pl.BlockSpec((1, c, wb), lambda i, j: (i, j, 0)),
                   pl.BlockSpec((1, hist, xbc), lambda i, j: (i, 0, 0)),
                   pl.BlockSpec((1,) + s0.shape[1:], lambda i, j: (i, 0, 0, 0))],
        scratch_shapes=[pltpu.VMEM((c + 8, xbc), F32)],
        compiler_params=_params(2),
        name="mamba2_mixer",
    )(p, conv_prev, s0, *consts)


def _hgrn_sum_matrix(c):
    t = np.arange(c)[:, None]
    j = np.arange(c)[None, :]
    blocks = [(j <= t), (j > t)]
    m = c // 2
    while m >= 1:
        mid = (t // (2 * m)) * (2 * m) + m - 1
        right = (t % (2 * m)) >= m
        blocks.append(np.where(right, (j > mid) & (j <= t), (j > t) & (j <= mid)))
        m //= 2
    return np.concatenate(blocks, axis=0).astype(np.float32)


def _split_dot(w01, x):
    wb = w01
    x1 = x.astype(BF16)
    r1 = x - x1.astype(F32)
    x2 = r1.astype(BF16)
    x3 = (r1 - x2.astype(F32)).astype(BF16)
    dot = lambda piece: jnp.dot(wb, piece, preferred_element_type=F32)
    return dot(x1) + dot(x2) + dot(x3)


def _hgrn_body(p_ref, s0_ref, sums_ref, lbp_ref, nw_ref, o_ref, s_ref, st_ref, *, layer):
    c = p_ref.shape[1]
    heads, dk, dv = s0_ref.shape[1:]
    wc = heads * dk
    nchunks = pl.num_programs(1)

    @pl.when(pl.program_id(1) == 0)
    def _():
        for h in range(heads):
            st_ref[h] = s0_ref[0, h].T

    lbp = lbp_ref[...]
    e = jnp.exp(lbp - jnp.max(lbp, axis=0, keepdims=True))
    soft = e / jnp.sum(e, axis=0, keepdims=True)
    lb = soft[0:1, :]
    for i in range(1, layer + 1):
        lb = lb + soft[i:i + 1, :]
    lb = lb - soft[0:1, :]

    p = p_ref[0]
    q = _silu(p[:, 0:wc])
    f = p[:, wc:2 * wc]
    v = p[:, 2 * wc:3 * wc]
    gate = p[:, 3 * wc:4 * wc]
    log_f = jnp.log(lb + (1.0 - lb) * jax.nn.sigmoid(f))
    k = (1.0 - lb) * jax.nn.sigmoid(-f)

    sums = _split_dot(sums_ref[...], log_f)
    bcum = sums[0:c]
    qe = q * jnp.exp(bcum)
    ke = k * jnp.exp(sums[c:2 * c])
    f_tot = jnp.exp(bcum[c - 1:c, :])

    row = lax.broadcasted_iota(jnp.int32, (c, 1), 0)
    rr = lax.broadcasted_iota(jnp.int32, (c, c), 0)
    cc = lax.broadcasted_iota(jnp.int32, (c, c), 1)
    levels = []
    m = c // 2
    i = 2
    while m >= 1:
        wgt = jnp.exp(sums[i * c:(i + 1) * c])
        right = (row & (2 * m - 1)) >= m
        levels.append((jnp.where(right, q * wgt, 0.0).astype(BF16),
                       jnp.where(right, 0.0, k * wgt).astype(BF16),
                       (rr ^ cc) < 2 * m))
        m //= 2
        i += 1
    qk = q * k
    eye = rr == cc

    hs = range(heads)
    sls = [slice(h * dk, (h + 1) * dk) for h in hs]
    atts = [jnp.where(eye, jnp.sum(qk[:, sl], axis=-1, keepdims=True), 0.0) for sl in sls]
    for qm, km, same in levels:
        atts = [att + jnp.where(same, _bdot_nt(qm[:, sl], km[:, sl]), 0.0) for att, sl in zip(atts, sls)]
    vhs = [v[:, sl].astype(BF16) for sl in sls]
    sts = [st_ref[h] for h in hs]
    intra = [_bdot(att, vh) for att, vh in zip(atts, vhs)]
    inter = [_bdot_nt(qe[:, sl], st) for sl, st in zip(sls, sts)]
    upd = [_bdot_tn(vh, ke[:, sl]) for vh, sl in zip(vhs, sls)]
    for h, sl in zip(hs, sls):
        st_ref[h] = sts[h] * f_tot[:, sl] + upd[h]
        o = intra[h] + inter[h]
        ms = jnp.mean(o * o, axis=-1, keepdims=True)
        o_ref[0, :, sl] = o * lax.rsqrt(ms + GROUP_RMS_EPS) * nw_ref[:, sl] * _silu(gate[:, sl])

    @pl.when(pl.program_id(1) == nchunks - 1)
    def _():
        for h in range(heads):
            s_ref[0, h] = st_ref[h].T


def _hgrn_call(p, s0, lb_param, norm_w, layer):
    b, t, cols = p.shape
    heads, dk, dv = s0.shape[1:]
    assert dk == dv
    wc = heads * dk
    c = _chunk_len(t)
    sums = jnp.asarray(_hgrn_sum_matrix(c), dtype=BF16)
    return pl.pallas_call(
        functools.partial(_hgrn_body, layer=layer),
        out_shape=[jax.ShapeDtypeStruct((b, t, wc), F32),
                   jax.ShapeDtypeStruct(s0.shape, F32)],
        grid=(b, t // c),
        in_specs=[pl.BlockSpec((1, c, cols), lambda i, j: (i, j, 0)),
                  pl.BlockSpec((1,) + s0.shape[1:], lambda i, j: (i, 0, 0, 0)),
                  _const_spec(sums.shape), _const_spec(lb_param.shape), _const_spec((1, wc))],
        out_specs=[pl.BlockSpec((1, c, wc), lambda i, j: (i, j, 0)),
                   pl.BlockSpec((1,) + s0.shape[1:], lambda i, j: (i, 0, 0, 0))],
        scratch_shapes=[pltpu.VMEM((heads, dv, dk), F32)],
        compiler_params=_params(2),
        name="hgrn2_mixer",
    )(p, s0, sums, lb_param, norm_w.reshape(1, wc))


def _round_up(n, m):
    return -(-n // m) * m


def _prepare_weights(w):
    a_cols = w["mu_a"].shape[1]
    b_cols = w["w_in_ab"].shape[2] - a_cols
    b_pad = _round_up(b_cols, LANE) - b_cols
    return dict(
        w_in_a=w["w_in_ab"][:, :, :a_cols].astype(BF16),
        w_in_b=jnp.pad(w["w_in_ab"][:, :, a_cols:], ((0, 0), (0, 0), (0, b_pad))).astype(BF16),
        w_out_ab=w["w_out_ab"].astype(BF16),
        w_in_c=w["w_in_c"].astype(BF16),
        w_out_c=w["w_out_c"].astype(BF16),
        w_gate=w["w_gate"].astype(BF16),
        w_up=w["w_up"].astype(BF16),
        w_down=w["w_down"].astype(BF16),
    )


def _trunk(x, mod, st_rwkv, st_shift, st_ssm, st_conv, st_hgrn, w, wb):
    depth = mod.shape[0]
    b, t, d = x.shape
    groups = (st_conv.shape[-1] - st_ssm.shape[2] * st_ssm.shape[3]) // (2 * st_ssm.shape[4])
    new_rwkv, new_shift, new_ssm, new_conv, new_hgrn = [], [], [], [], []
    for layer in range(depth):
        j = layer // 2
        sh_m, sc_m, g_m, sh_f, sc_f, g_f = (mod[layer, :, None, i * d:(i + 1) * d] for i in range(6))
        nw_mix = w["norm_mix_w"][layer].reshape(1, d)
        nw_ffn = w["norm_ffn_w"][layer].reshape(1, d)
        if layer % 2 == 0:
            pa, pb = _inproj_call(x, nw_mix, sc_m, sh_m, [wb["w_in_a"][j], wb["w_in_b"][j]])
            prm = dict(mu=w["mu_a"][j], w0=w["w0"][j], w2=w["w2"][j], a0=w["a0"][j], a2=w["a2"][j],
                       g2=w["g2"][j], k_k=w["k_k"][j], k_a=w["k_a"][j], r_k=w["r_k"][j],
                       lnx_w=w["lnx_w"][j], lnx_b=w["lnx_b"][j])
            oa, shift_new, rwkv_new = _rwkv_call(pa, st_shift[:, j], st_rwkv[:, j], prm)
            prm_b = dict(conv_w=w["conv_w"][j], conv_b=w["conv_b"][j], dt_bias=w["dt_bias"][j],
                         a_log=w["a_log"][j], d_skip=w["d_skip"][j], norm_b_w=w["norm_b_w"][j])
            ob, conv_new, ssm_new = _mamba_call(pb, st_conv[:, j], st_ssm[:, j], prm_b, groups)
            mixes, wout = [oa, ob], wb["w_out_ab"][j]
            new_rwkv.append(rwkv_new)
            new_shift.append(shift_new[:, 0])
            new_ssm.append(ssm_new)
            new_conv.append(conv_new)
        else:
            (pc,) = _inproj_call(x, nw_mix, sc_m, sh_m, [wb["w_in_c"][j]])
            oc, hgrn_new = _hgrn_call(pc, st_hgrn[:, j], w["lb_param"], w["norm_c_w"][j], j)
            mixes, wout = [oc], wb["w_out_c"][j]
            new_hgrn.append(hgrn_new)
        final_w = w["norm_out_w"].reshape(1, d) if layer == depth - 1 else None
        x = _post_call(x, mixes, wout, g_m, nw_ffn, sc_f, sh_f, g_f,
                       wb["w_gate"][layer], wb["w_up"][layer], wb["w_down"][layer], final_w)
    return (x, jnp.stack(new_rwkv, axis=1), jnp.stack(new_shift, axis=1), jnp.stack(new_ssm, axis=1),
            jnp.stack(new_conv, axis=1), jnp.stack(new_hgrn, axis=1))


def _run(x_prompt, x_sample, state_rwkv, state_rwkv_shift, state_ssm, state_conv, state_hgrn,
         c_prompt, c_sample, w):
    bp, bs = x_prompt.shape[0], x_sample.shape[0]
    rows = _round_up(bp + bs, 8)
    c_all = jnp.pad(jnp.concatenate([c_prompt, c_sample], axis=0), ((0, rows - bp - bs), (0, 0)))
    mod = _ada_call(c_all, w["ada_w"], w["ada_b"])
    wb = _prepare_weights(w)
    zeros = lambda s: jnp.zeros((bp,) + s.shape[1:], F32)
    outs_p = _trunk(x_prompt, mod[:, :bp], zeros(state_rwkv), zeros(state_rwkv_shift), zeros(state_ssm),
                    zeros(state_conv), zeros(state_hgrn), w, wb)
    outs_s = _trunk(x_sample, mod[:, bp:bp + bs], state_rwkv, state_rwkv_shift, state_ssm,
                    state_conv, state_hgrn, w, wb)
    return (outs_p[0], outs_s[0]) + outs_p[1:] + outs_s[1:]


def kernel(x_prompt, x_sample, state_rwkv, state_rwkv_shift, state_ssm, state_conv, state_hgrn, c_prompt, c_sample, norm_mix_w, norm_ffn_w, norm_out_w, ada_w, ada_b, w_in_ab, w_out_ab, mu_a, w0, w2, a0, a2, g2, k_k, k_a, r_k, lnx_w, lnx_b, conv_w, conv_b, dt_bias, a_log, d_skip, norm_b_w, w_in_c, w_out_c, lb_param, norm_c_w, w_gate, w_up, w_down):
    w = dict(norm_mix_w=norm_mix_w, norm_ffn_w=norm_ffn_w, norm_out_w=norm_out_w, ada_w=ada_w, ada_b=ada_b,
             w_in_ab=w_in_ab, w_out_ab=w_out_ab, mu_a=mu_a, w0=w0, w2=w2, a0=a0, a2=a2, g2=g2, k_k=k_k,
             k_a=k_a, r_k=r_k, lnx_w=lnx_w, lnx_b=lnx_b, conv_w=conv_w, conv_b=conv_b, dt_bias=dt_bias,
             a_log=a_log, d_skip=d_skip, norm_b_w=norm_b_w, w_in_c=w_in_c, w_out_c=w_out_c,
             lb_param=lb_param, norm_c_w=norm_c_w, w_gate=w_gate, w_up=w_up, w_down=w_down)
    return _run(x_prompt, x_sample, state_rwkv, state_rwkv_shift, state_ssm, state_conv, state_hgrn,
                c_prompt, c_sample, w)
```

```python
import functools
import math

import numpy as np
import jax
import jax.numpy as jnp
from jax import lax
from jax.experimental import pallas as pl
from jax.experimental.pallas import tpu as pltpu

F32 = jnp.float32
BF16 = jnp.bfloat16

LANE = 128
VMEM_LIMIT = 56 * 2**20
ROW_TILE = 512
COL_CHUNK = 512
RWKV_CHUNK = 64
MAMBA_CHUNK = 128
HGRN_CHUNK = 64
MIX_SEQS = 2
LNX_EPS = 64e-5
RMS_EPS = 1e-6
GROUP_RMS_EPS = 1e-5


def _bdot(a, b):
    return jnp.dot(a.astype(BF16), b.astype(BF16), preferred_element_type=F32)


def _bdot_nt(a, b):
    return lax.dot_general(a.astype(BF16), b.astype(BF16), (((1,), (1,)), ((), ())),
                           preferred_element_type=F32)


def _bdot_tn(a, b):
    return lax.dot_general(a.astype(BF16), b.astype(BF16), (((0,), (0,)), ((), ())),
                           preferred_element_type=F32)


def _silu(x):
    return x * jax.nn.sigmoid(x)


def _const_spec(shape, single_buffer=False):
    nd = len(shape)
    if single_buffer:
        return pl.BlockSpec(shape, lambda *_: (0,) * nd, pipeline_mode=pl.Buffered(1))
    return pl.BlockSpec(shape, lambda *_: (0,) * nd)


def _params(n_axes):
    return pltpu.CompilerParams(dimension_semantics=("arbitrary",) * n_axes,
                                vmem_limit_bytes=VMEM_LIMIT)


def _row_tiling(b, t):
    if t >= ROW_TILE:
        assert t % ROW_TILE == 0
        return 1, ROW_TILE
    bb = max(1, min(b, ROW_TILE // t))
    while b % bb:
        bb -= 1
    return bb, t


def _col_chunks(n):
    return [(n0, min(n0 + COL_CHUNK, n)) for n0 in range(0, n, COL_CHUNK)]


def _ada_body(c_ref, w_ref, b_ref, o_ref):
    o_ref[0] = _bdot(_silu(c_ref[...]), w_ref[0]) + b_ref[0]


def _ada_call(c_all, ada_w, ada_b):
    depth, d, n = ada_w.shape
    r = c_all.shape[0]
    tn = 1024
    assert n % tn == 0
    return pl.pallas_call(
        _ada_body,
        out_shape=jax.ShapeDtypeStruct((depth, r, n), F32),
        grid=(depth, n // tn),
        in_specs=[pl.BlockSpec((r, d), lambda l, j: (0, 0)),
                  pl.BlockSpec((1, d, tn), lambda l, j: (l, 0, j)),
                  pl.BlockSpec((1, 1, tn), lambda l, j: (l, 0, j))],
        out_specs=pl.BlockSpec((1, r, tn), lambda l, j: (l, 0, j)),
        compiler_params=_params(2),
        name="ada_mod",
    )(c_all, ada_w, ada_b.reshape(depth, 1, n))


def _norm_mod(x, nw, sc, sh):
    ms = jnp.mean(x * x, axis=-1, keepdims=True)
    y = x * lax.rsqrt(ms + RMS_EPS) * nw
    return y * (1.0 + sc) + sh


def _inproj_body(*refs, n_out):
    x_ref, nw_ref, sc_ref, sh_ref = refs[:4]
    w_refs = refs[4:4 + n_out]
    o_refs = refs[4 + n_out:]
    bb, tt, d = x_ref.shape
    h = _norm_mod(x_ref[...], nw_ref[...], sc_ref[...], sh_ref[...])
    hb = h.reshape(bb * tt, d).astype(BF16)
    for w_ref, o_ref in zip(w_refs, o_refs):
        for n0, n1 in _col_chunks(w_ref.shape[1]):
            o_ref[:, :, n0:n1] = jnp.dot(hb, w_ref[:, n0:n1],
                                         preferred_element_type=F32).reshape(bb, tt, n1 - n0)


def _inproj_call(x, nw, sc, sh, weights):
    b, t, d = x.shape
    bb, tt = _row_tiling(b, t)
    n_out = len(weights)
    return pl.pallas_call(
        functools.partial(_inproj_body, n_out=n_out),
        out_shape=[jax.ShapeDtypeStruct((b, t, w.shape[1]), F32) for w in weights],
        grid=(b // bb, t // tt),
        in_specs=[pl.BlockSpec((bb, tt, d), lambda i, j: (i, j, 0)),
                  _const_spec((1, d)),
                  pl.BlockSpec((bb, 1, d), lambda i, j: (i, 0, 0)),
                  pl.BlockSpec((bb, 1, d), lambda i, j: (i, 0, 0))]
                 + [_const_spec(w.shape, single_buffer=True) for w in weights],
        out_specs=[pl.BlockSpec((bb, tt, w.shape[1]), lambda i, j: (i, j, 0)) for w in weights],
        compiler_params=_params(2),
        name="in_proj",
    )(x, nw, sc, sh, *weights)


def _post_body(*refs, n_mix, final):
    x_ref = refs[0]
    mix_refs = refs[1:1 + n_mix]
    (wout_ref, gm_ref, nw_ref, sc_ref, sh_ref, gf_ref, wg_ref, wu_ref, wd_ref) = refs[1 + n_mix:10 + n_mix]
    rest = refs[10 + n_mix:]
    if final:
        fw_ref, o_ref, act_ref = rest
    else:
        o_ref, act_ref = rest
    bb, tt, d = x_ref.shape
    rows = bb * tt

    mix = None
    off = 0
    for m_ref in mix_refs:
        wdt = m_ref.shape[-1]
        part = jnp.dot(m_ref[...].reshape(rows, wdt).astype(BF16), wout_ref[off:off + wdt, :],
                       preferred_element_type=F32)
        mix = part if mix is None else mix + part
        off += wdt
    x1 = x_ref[...] + gm_ref[...] * mix.reshape(bb, tt, d)

    h = _norm_mod(x1, nw_ref[...], sc_ref[...], sh_ref[...])
    hb = h.reshape(rows, d).astype(BF16)
    for f0, f1 in _col_chunks(wg_ref.shape[1]):
        gate = jnp.dot(hb, wg_ref[:, f0:f1], preferred_element_type=F32)
        up = jnp.dot(hb, wu_ref[:, f0:f1], preferred_element_type=F32)
        act_ref[:, f0:f1] = (_silu(gate) * up).astype(BF16)
    ffn = jnp.dot(act_ref[...], wd_ref[...], preferred_element_type=F32)
    x2 = x1 + gf_ref[...] * ffn.reshape(bb, tt, d)
    if final:
        ms = jnp.mean(x2 * x2, axis=-1, keepdims=True)
        x2 = x2 * lax.rsqrt(ms + RMS_EPS) * fw_ref[...]
    o_ref[...] = x2


def _post_call(x, mixes, wout, gm, nw, sc, sh, gf, wg, wu, wd, final_w):
    b, t, d = x.shape
    bb, tt = _row_tiling(b, t)
    final = final_w is not None
    row_spec = lambda w: pl.BlockSpec((bb, tt, w), lambda i, j: (i, j, 0))
    seq_spec = pl.BlockSpec((bb, 1, d), lambda i, j: (i, 0, 0))
    in_specs = ([row_spec(d)] + [row_spec(m.shape[-1]) for m in mixes]
                + [_const_spec(wout.shape, True), seq_spec, _const_spec((1, d)), seq_spec, seq_spec, seq_spec,
                   _const_spec(wg.shape, True), _const_spec(wu.shape, True), _const_spec(wd.shape, True)])
    args = [x, *mixes, wout, gm, nw, sc, sh, gf, wg, wu, wd]
    if final:
        in_specs.append(_const_spec((1, d)))
        args.append(final_w)
    return pl.pallas_call(
        functools.partial(_post_body, n_mix=len(mixes), final=final),
        out_shape=jax.ShapeDtypeStruct((b, t, d), F32),
        grid=(b // bb, t // tt),
        in_specs=in_specs,
        out_specs=row_spec(d),
        scratch_shapes=[pltpu.VMEM((bb * tt, wg.shape[1]), BF16)],
        compiler_params=_params(2),
        name="post_ffn",
    )(*args)


def _tri(c, strict=False, reps=1):
    row = lax.broadcasted_iota(jnp.int32, (c, reps * c), 0)
    col = lax.broadcasted_iota(jnp.int32, (c, reps * c), 1) & (c - 1)
    return (row > col) if strict else (row >= col)


def _block_tri(rows, c, upper=False):
    r = lax.broadcasted_iota(jnp.int32, (rows, rows), 0)
    q = lax.broadcasted_iota(jnp.int32, (rows, rows), 1)
    tri = (r <= q) if upper else (r >= q)
    return (tri & ((r ^ q) < c)).astype(BF16)


def _split3(x):
    x1 = x.astype(BF16)
    r1 = x - x1.astype(F32)
    x2 = r1.astype(BF16)
    x3 = (r1 - x2.astype(F32)).astype(BF16)
    return x1, x2, x3


def _split_dot(w01, x):
    x1, x2, x3 = _split3(x)
    dot = lambda piece: jnp.dot(w01, piece, preferred_element_type=F32)
    return dot(x1) + dot(x2) + dot(x3)


def _split_dot_tn(x, w01):
    x1, x2, x3 = _split3(x)
    dot = lambda piece: lax.dot_general(piece, w01, (((0,), (0,)), ((), ())), preferred_element_type=F32)
    return dot(x1) + dot(x2) + dot(x3)


def _expand_cols(x, width):
    k = x.shape[1]
    r = lax.broadcasted_iota(jnp.int32, (k, k * width), 0) * width
    q = lax.broadcasted_iota(jnp.int32, (k, k * width), 1)
    sel = ((q >= r) & (q < r + width)).astype(BF16)
    x1, x2, x3 = _split3(x)
    dot = lambda piece: jnp.dot(piece, sel, preferred_element_type=F32)
    return dot(x1) + dot(x2) + dot(x3)


def _last_rows(x, bb, c):
    lasts = [x[(b + 1) * c - 1:(b + 1) * c, :] for b in range(bb)]
    tiled = [jnp.broadcast_to(l, (c, x.shape[1])) for l in lasts]
    return lasts, (tiled[0] if bb == 1 else jnp.concatenate(tiled, axis=0))


def _mixer_tiling(b, t, chunk):
    bb = MIX_SEQS if b % MIX_SEQS == 0 else 1
    return bb, math.gcd(chunk, t)


def _unit_lower_inverse(a_list, c):
    eye = (lax.broadcasted_iota(jnp.int32, (c, c), 0) == lax.broadcasted_iota(jnp.int32, (c, c), 1)).astype(F32)
    ns = list(a_list)
    ts = [eye + n for n in ns]
    p = 2
    while p < c:
        ns = [_bdot(n, n) for n in ns]
        ts = [t + _bdot(t, n) for t, n in zip(ts, ns)]
        p *= 2
    return ts


def _rwkv_body(p_ref, shift_ref, s0_ref, mu_ref, w0_ref, w2_ref, a0_ref, a2_ref, g2_ref,
               kk_ref, ka_ref, rk_ref, lw_ref, lb_ref,
               o_ref, shift_out_ref, s_ref, prev_ref, *, heads, lora):
    bb, c, cols = p_ref.shape
    rows = bb * c
    hd = s_ref.shape[-1]
    wa = heads * hd
    lw_, la_, lg_ = lora

    @pl.when(pl.program_id(1) == 0)
    def _():
        s_ref[...] = s0_ref[...]
        prev_ref[...] = shift_ref[...]

    p = p_ref[...].reshape(rows, cols)
    row = lax.broadcasted_iota(jnp.int32, (rows, cols), 0)
    p_prev = pltpu.roll(p, 1, axis=0)
    for b in range(bb):
        p_prev = jnp.where(row == b * c, prev_ref[b], p_prev)
        last = p[(b + 1) * c - 1:(b + 1) * c, :]
        prev_ref[b] = last
        shift_out_ref[b] = last
    pm = p + (p_prev - p) * mu_ref[...]

    r = pm[:, 0:wa]
    k = pm[:, wa:2 * wa]
    v = pm[:, 2 * wa:3 * wa]
    o1 = 3 * wa
    xw = pm[:, o1:o1 + lw_]
    xa = pm[:, o1 + lw_:o1 + lw_ + la_]
    xg = pm[:, o1 + lw_ + la_:o1 + lw_ + la_ + lg_]

    w = -jax.nn.softplus(-(w0_ref[...] + _bdot(jnp.tanh(xw), w2_ref[...]))) - 0.5
    logd = -jnp.exp(w)
    a = jax.nn.sigmoid(a0_ref[...] + _bdot(xa, a2_ref[...]))
    g = _bdot(jax.nn.sigmoid(xg), g2_ref[...])
    kk_raw = k * kk_ref[...]
    k2 = k * (1.0 + (a - 1.0) * ka_ref[...])
    rkk = r * k2 * rk_ref[...]

    incl2 = _tri(c, reps=2)
    strict = _tri(c, strict=True)
    cum = _split_dot(_block_tri(rows, c), logd)
    tots, tot_rows = _last_rows(cum, bb, c)
    p_incl = jnp.exp(cum)
    p_excl = jnp.exp(cum - logd)
    p_inv = jnp.exp(-cum)
    p_end = jnp.exp(tot_rows - cum)
    p_tots = [jnp.exp(t) for t in tots]

    pairs = [(b, h) for b in range(bb) for h in range(heads)]
    idx = [(slice(b * c, (b + 1) * c), slice(h * hd, (h + 1) * hd)) for b, h in pairs]
    xs_in, ys_in, ends, vs = [], [], [], []
    for rs, sl in idx:
        kk = kk_raw[rs, sl]
        kk = kk / jnp.maximum(jnp.sqrt(jnp.sum(kk * kk, axis=-1, keepdims=True)), 1e-12)
        kka = kk * a[rs, sl]
        xs_in.append(jnp.concatenate([-kk * p_excl[rs, sl], r[rs, sl] * p_incl[rs, sl]], axis=0).astype(BF16))
        ys_in.append(jnp.concatenate([kka * p_inv[rs, sl], k2[rs, sl] * p_inv[rs, sl]], axis=0).astype(BF16))
        ends.append(jnp.concatenate([kka * p_end[rs, sl], k2[rs, sl] * p_end[rs, sl]], axis=0).astype(BF16))
        vs.append(v[rs, sl])
    s0s = [s_ref[b, h] for b, h in pairs]
    grams = [_bdot_nt(x, y) for x, y in zip(xs_in, ys_in)]
    xss = [_bdot_nt(x, s0) for x, s0 in zip(xs_in, s0s)]
    a_abs = [jnp.where(strict, gm[:c, :c], 0.0) for gm in grams]
    ws = [xs[:c] + _bdot(jnp.where(strict, gm[:c, c:], 0.0), vh) for xs, gm, vh in zip(xss, grams, vs)]
    t_invs = _unit_lower_inverse(a_abs, c)
    us = [_bdot(t_inv, w_) for t_inv, w_ in zip(t_invs, ws)]
    uvs = [jnp.concatenate([u, vh], axis=0).astype(BF16) for u, vh in zip(us, vs)]
    outs = [xs[c:] + _bdot(jnp.where(incl2, gm[c:, :], 0.0), uv) for xs, gm, uv in zip(xss, grams, uvs)]
    news = [s0 * p_tots[b][:, sl] + _bdot_tn(uv, en)
            for s0, (b, _), (_, sl), uv, en in zip(s0s, pairs, idx, uvs, ends)]
    for i, ((b, h), (rs, sl)) in enumerate(zip(pairs, idx)):
        s_ref[b, h] = news[i]
        o = outs[i]
        mean = jnp.mean(o, axis=-1, keepdims=True)
        var = jnp.mean(jnp.square(o - mean), axis=-1, keepdims=True)
        o = (o - mean) * lax.rsqrt(var + LNX_EPS) * lw_ref[:, sl] + lb_ref[:, sl]
        bonus = jnp.sum(rkk[rs, sl], axis=-1, keepdims=True) * vs[i]
        o_ref[b, :, sl] = (o + bonus) * g[rs, sl]


def _rwkv_call(p, shift_prev, s0, prm):
    b, t, cols = p.shape
    heads, hd = s0.shape[1], s0.shape[2]
    wa = heads * hd
    bb, c = _mixer_tiling(b, t, RWKV_CHUNK)
    lora = (prm["w2"].shape[0], prm["a2"].shape[0], prm["g2"].shape[0])
    vec = lambda a: a.reshape(1, -1)
    consts = [vec(prm["mu"]), vec(prm["w0"]), prm["w2"], vec(prm["a0"]), prm["a2"], prm["g2"],
              vec(prm["k_k"]), vec(prm["k_a"]), vec(prm["r_k"]), vec(prm["lnx_w"]), vec(prm["lnx_b"])]
    seq_spec = lambda shape: pl.BlockSpec((bb,) + shape, lambda i, j: (i,) + (0,) * len(shape))
    return pl.pallas_call(
        functools.partial(_rwkv_body, heads=heads, lora=lora),
        out_shape=[jax.ShapeDtypeStruct((b, t, wa), F32),
                   jax.ShapeDtypeStruct((b, 1, cols), F32),
                   jax.ShapeDtypeStruct(s0.shape, F32)],
        grid=(b // bb, t // c),
        in_specs=[pl.BlockSpec((bb, c, cols), lambda i, j: (i, j, 0)),
                  seq_spec((1, cols)), seq_spec(s0.shape[1:])]
                 + [_const_spec(a.shape) for a in consts],
        out_specs=[pl.BlockSpec((bb, c, wa), lambda i, j: (i, j, 0)),
                   seq_spec((1, cols)), seq_spec(s0.shape[1:])],
        scratch_shapes=[pltpu.VMEM((bb, 1, cols), F32)],
        compiler_params=_params(2),
        name="rwkv7_mixer",
    )(p, shift_prev.reshape(b, 1, cols), s0, *consts)


def _mamba_body(p_ref, conv_ref, s0_ref, cw_ref, cb_ref, dtb_ref, alog_ref, dskip_ref, nw_ref,
                o_ref, conv_out_ref, s_ref, ubuf_ref, inter_ref, *, groups, conv_w):
    bb, c, cols = p_ref.shape
    rows = bb * c
    heads, hd, ns = s_ref.shape[1], s_ref.shape[2], s_ref.shape[3]
    wb = heads * hd
    xbc = conv_ref.shape[-1]
    hpg = heads // groups
    pad = 8
    hist = conv_w - 1

    @pl.when(pl.program_id(1) == 0)
    def _():
        s_ref[...] = s0_ref[...]
        for b in range(bb):
            ubuf_ref[b, 0:pad, :] = jnp.zeros((pad, xbc), F32)
            ubuf_ref[b, pad - hist:pad, :] = conv_ref[b]

    ys = []
    for b in range(bb):
        u = p_ref[b, :, wb:wb + xbc]
        ubuf_ref[b, pad:pad + c, :] = u
        y = cb_ref[...] + cw_ref[hist:hist + 1, :] * u
        for i in range(hist):
            y = y + cw_ref[i:i + 1, :] * ubuf_ref[b, pad - hist + i:pad - hist + i + c, :]
        conv_out_ref[b] = ubuf_ref[b, pad + c - hist:pad + c, :]
        ubuf_ref[b, 0:pad, :] = ubuf_ref[b, c:c + pad, :]
        ys.append(y)
    xc = _silu(ys[0] if bb == 1 else jnp.concatenate(ys, axis=0))
    xs = xc[:, 0:wb]
    bm = xc[:, wb:wb + groups * ns]
    cm = xc[:, wb + groups * ns:wb + 2 * groups * ns]
    z = p_ref[:, :, 0:wb].reshape(rows, wb)
    dt_raw = p_ref[:, :, wb + xbc:wb + xbc + heads].reshape(rows, heads)
    dt = jax.nn.softplus(dt_raw + dtb_ref[...])
    la = dt * (-jnp.exp(alog_ref[...]))

    incl = _tri(c)
    acum = _split_dot(_block_tri(rows, c), la)
    acum_t = _split_dot_tn(la, _block_tri(rows, c, upper=True))
    a_lasts, a_last_rows = _last_rows(acum, bb, c)
    e_tots = [jnp.exp(al) for al in a_lasts]
    stack = jnp.concatenate([dt, jnp.exp(acum), jnp.exp(a_last_rows - acum),
                             jnp.broadcast_to(dskip_ref[...], (8, heads))], axis=0)
    full = _expand_cols(stack, hd)
    ecum_f = full[rows:2 * rows]
    dskip_f = full[3 * rows:3 * rows + 1]
    xd_f = xs * full[0:rows]
    xde_f = xd_f * full[2 * rows:3 * rows]
    colb = _expand_cols(acum, c)

    rss = [slice(b * c, (b + 1) * c) for b in range(bb)]
    bgs = [[bm[rs, gi * ns:(gi + 1) * ns].astype(BF16) for gi in range(groups)] for rs in rss]
    cgs = [[cm[rs, gi * ns:(gi + 1) * ns].astype(BF16) for gi in range(groups)] for rs in rss]
    cbs = [[_bdot_nt(cg, bg) for cg, bg in zip(cgb, bgb)] for cgb, bgb in zip(cgs, bgs)]
    pairs = [(b, h) for b in range(bb) for h in range(heads)]
    sls = [slice(h * hd, (h + 1) * hd) for _, h in pairs]
    s0s = [s_ref[b, h] for b, h in pairs]
    mats = []
    for b, h in pairs:
        seg = colb[rss[b], h * c:(h + 1) * c] - acum_t[h:h + 1, rss[b]]
        decay = jnp.where(incl, jnp.exp(jnp.where(incl, seg, 0.0)), 0.0)
        mats.append((cbs[b][h // hpg] * decay).astype(BF16))
    intra = [_bdot(m, xd_f[rss[b], sl]) for m, (b, _), sl in zip(mats, pairs, sls)]
    inter = [_bdot_nt(cgs[b][h // hpg], s0) for (b, h), s0 in zip(pairs, s0s)]
    upd = [_bdot_tn(xde_f[rss[b], sl], bgs[b][h // hpg]) for (b, h), sl in zip(pairs, sls)]
    for i, ((b, h), sl) in enumerate(zip(pairs, sls)):
        s_ref[b, h] = s0s[i] * e_tots[b][:, h:h + 1] + upd[i]
        o_ref[b, :, sl] = intra[i]
        inter_ref[b, :, sl] = inter[i]

    y_all = o_ref[...].reshape(rows, wb) + inter_ref[...].reshape(rows, wb) * ecum_f + dskip_f * xs
    yv = y_all * _silu(z)
    gw = wb // groups
    for gi in range(groups):
        sl = slice(gi * gw, (gi + 1) * gw)
        yg = yv[:, sl]
        ms = jnp.mean(yg * yg, axis=-1, keepdims=True)
        o_ref[:, :, sl] = (yg * lax.rsqrt(ms + GROUP_RMS_EPS) * nw_ref[:, sl]).reshape(bb, c, gw)


def _mamba_call(p, conv_prev, s0, prm, groups):
    b, t, cols = p.shape
    heads, hd, ns = s0.shape[1:]
    wb = heads * hd
    hist, xbc = conv_prev.shape[1:]
    bb, c = _mixer_tiling(b, t, MAMBA_CHUNK)
    vec = lambda a: a.reshape(1, -1)
    consts = [prm["conv_w"], vec(prm["conv_b"]), vec(prm["dt_bias"]), vec(prm["a_log"]),
              vec(prm["d_skip"]), vec(prm["norm_b_w"])]
    seq_spec = lambda shape: pl.BlockSpec((bb,) + shape, lambda i, j: (i,) + (0,) * len(shape))
    return pl.pallas_call(
        functools.partial(_mamba_body, groups=groups, conv_w=hist + 1),
        out_shape=[jax.ShapeDtypeStruct((b, t, wb), F32),
                   jax.ShapeDtypeStruct(conv_prev.shape, F32),
                   jax.ShapeDtypeStruct(s0.shape, F32)],
        grid=(b // bb, t // c),
        in_specs=[pl.BlockSpec((bb, c, cols), lambda i, j: (i, j, 0)),
                  seq_spec((hist, xbc)), seq_spec(s0.shape[1:])]
                 + [_const_spec(a.shape) for a in consts],
        out_specs=[pl.BlockSpec((bb, c, wb), lambda i, j: (i, j, 0)),
                   seq_spec((hist, xbc)), seq_spec(s0.shape[1:])],
        scratch_shapes=[pltpu.VMEM((bb, c + 8, xbc), F32), pltpu.VMEM((bb, c, wb), F32)],
        compiler_params=_params(2),
        name="mamba2_mixer",
    )(p, conv_prev, s0, *consts)


def _hgrn_sum_matrix(c):
    t = np.arange(c)[:, None]
    j = np.arange(c)[None, :]
    blocks = [(j <= t), (j > t)]
    m = c // 2
    while m >= 1:
        mid = (t // (2 * m)) * (2 * m) + m - 1
        right = (t % (2 * m)) >= m
        blocks.append(np.where(right, (j > mid) & (j <= t), (j > t) & (j <= mid)))
        m //= 2
    return np.concatenate(blocks, axis=0).astype(np.float32)


def _hgrn_body(p_ref, s0_ref, sums_ref, lbp_ref, nw_ref, o_ref, s_ref, st_ref, *, layer):
    bb, c, cols = p_ref.shape
    rows = bb * c
    heads, dk, dv = s0_ref.shape[1:]
    wc = heads * dk
    nchunks = pl.num_programs(1)
    pairs = [(b, h) for b in range(bb) for h in range(heads)]

    @pl.when(pl.program_id(1) == 0)
    def _():
        for b, h in pairs:
            st_ref[b, h] = s0_ref[b, h].T

    lbp = lbp_ref[...]
    e = jnp.exp(lbp - jnp.max(lbp, axis=0, keepdims=True))
    soft = e / jnp.sum(e, axis=0, keepdims=True)
    lb = soft[0:1, :]
    for i in range(1, layer + 1):
        lb = lb + soft[i:i + 1, :]
    lb = lb - soft[0:1, :]

    p = p_ref[...].reshape(rows, cols)
    q = _silu(p[:, 0:wc])
    f = p[:, wc:2 * wc]
    v = p[:, 2 * wc:3 * wc]
    gate = p[:, 3 * wc:4 * wc]
    log_f = jnp.log(lb + (1.0 - lb) * jax.nn.sigmoid(f))
    k = (1.0 - lb) * jax.nn.sigmoid(-f)
    qk = q * k

    row = lax.broadcasted_iota(jnp.int32, (c, 1), 0)
    rr = lax.broadcasted_iota(jnp.int32, (c, c), 0)
    cc = lax.broadcasted_iota(jnp.int32, (c, c), 1)
    eye = rr == cc
    rss = [slice(b * c, (b + 1) * c) for b in range(bb)]
    qes, kes, f_tots, levels = [], [], [], []
    for rs in rss:
        sums = _split_dot(sums_ref[...], log_f[rs])
        bcum = sums[0:c]
        qes.append((q[rs] * jnp.exp(bcum)).astype(BF16))
        kes.append((k[rs] * jnp.exp(sums[c:2 * c])).astype(BF16))
        f_tots.append(jnp.exp(bcum[c - 1:c, :]))
        lv = []
        m = c // 2
        i = 2
        while m >= 1:
            wgt = jnp.exp(sums[i * c:(i + 1) * c])
            right = (row & (2 * m - 1)) >= m
            lv.append((jnp.where(right, q[rs] * wgt, 0.0).astype(BF16),
                       jnp.where(right, 0.0, k[rs] * wgt).astype(BF16),
                       (rr ^ cc) < 2 * m))
            m //= 2
            i += 1
        levels.append(lv)

    sls = [slice(h * dk, (h + 1) * dk) for _, h in pairs]
    atts = [jnp.where(eye, jnp.sum(qk[rss[b], sl], axis=-1, keepdims=True), 0.0) for (b, _), sl in zip(pairs, sls)]
    for li in range(len(levels[0])):
        atts = [att + jnp.where(levels[b][li][2], _bdot_nt(levels[b][li][0][:, sl], levels[b][li][1][:, sl]), 0.0)
                for att, (b, _), sl in zip(atts, pairs, sls)]
    vhs = [v[rss[b], sl].astype(BF16) for (b, _), sl in zip(pairs, sls)]
    sts = [st_ref[b, h] for b, h in pairs]
    intra = [_bdot(att, vh) for att, vh in zip(atts, vhs)]
    inter = [_bdot_nt(qes[b][:, sl], st) for (b, _), sl, st in zip(pairs, sls, sts)]
    upd = [_bdot_tn(vh, kes[b][:, sl]) for vh, (b, _), sl in zip(vhs, pairs, sls)]
    for i, ((b, h), sl) in enumerate(zip(pairs, sls)):
        st_ref[b, h] = sts[i] * f_tots[b][:, sl] + upd[i]
        o = intra[i] + inter[i]
        ms = jnp.mean(o * o, axis=-1, keepdims=True)
        o_ref[b, :, sl] = o * lax.rsqrt(ms + GROUP_RMS_EPS) * nw_ref[:, sl] * _silu(gate[rss[b], sl])

    @pl.when(pl.program_id(1) == nchunks - 1)
    def _():
        for b, h in pairs:
            s_ref[b, h] = st_ref[b, h].T


def _hgrn_call(p, s0, lb_param, norm_w, layer):
    b, t, cols = p.shape
    heads, dk, dv = s0.shape[1:]
    assert dk == dv
    wc = heads * dk
    bb, c = _mixer_tiling(b, t, HGRN_CHUNK)
    sums = jnp.asarray(_hgrn_sum_matrix(c), dtype=BF16)
    seq_spec = lambda shape: pl.BlockSpec((bb,) + shape, lambda i, j: (i,) + (0,) * len(shape))
    return pl.pallas_call(
        functools.partial(_hgrn_body, layer=layer),
        out_shape=[jax.ShapeDtypeStruct((b, t, wc), F32),
                   jax.ShapeDtypeStruct(s0.shape, F32)],
        grid=(b // bb, t // c),
        in_specs=[pl.BlockSpec((bb, c, cols), lambda i, j: (i, j, 0)),
                  seq_spec(s0.shape[1:]),
                  _const_spec(sums.shape), _const_spec(lb_param.shape), _const_spec((1, wc))],
        out_specs=[pl.BlockSpec((bb, c, wc), lambda i, j: (i, j, 0)),
                   seq_spec(s0.shape[1:])],
        scratch_shapes=[pltpu.VMEM((bb, heads, dv, dk), F32)],
        compiler_params=_params(2),
        name="hgrn2_mixer",
    )(p, s0, sums, lb_param, norm_w.reshape(1, wc))


def _round_up(n, m):
    return -(-n // m) * m


def _prepare_weights(w):
    a_cols = w["mu_a"].shape[1]
    b_cols = w["w_in_ab"].shape[2] - a_cols
    b_pad = _round_up(b_cols, LANE) - b_cols
    return dict(
        w_in_a=w["w_in_ab"][:, :, :a_cols].astype(BF16),
        w_in_b=jnp.pad(w["w_in_ab"][:, :, a_cols:], ((0, 0), (0, 0), (0, b_pad))).astype(BF16),
        w_out_ab=w["w_out_ab"].astype(BF16),
        w_in_c=w["w_in_c"].astype(BF16),
        w_out_c=w["w_out_c"].astype(BF16),
        w_gate=w["w_gate"].astype(BF16),
        w_up=w["w_up"].astype(BF16),
        w_down=w["w_down"].astype(BF16),
    )


def _trunk(x, mod, st_rwkv, st_shift, st_ssm, st_conv, st_hgrn, w, wb):
    depth = mod.shape[0]
    b, t, d = x.shape
    groups = (st_conv.shape[-1] - st_ssm.shape[2] * st_ssm.shape[3]) // (2 * st_ssm.shape[4])
    new_rwkv, new_shift, new_ssm, new_conv, new_hgrn = [], [], [], [], []
    for layer in range(depth):
        j = layer // 2
        sh_m, sc_m, g_m, sh_f, sc_f, g_f = (mod[layer, :, None, i * d:(i + 1) * d] for i in range(6))
        nw_mix = w["norm_mix_w"][layer].reshape(1, d)
        nw_ffn = w["norm_ffn_w"][layer].reshape(1, d)
        if layer % 2 == 0:
            pa, pb = _inproj_call(x, nw_mix, sc_m, sh_m, [wb["w_in_a"][j], wb["w_in_b"][j]])
            prm = dict(mu=w["mu_a"][j], w0=w["w0"][j], w2=w["w2"][j], a0=w["a0"][j], a2=w["a2"][j],
                       g2=w["g2"][j], k_k=w["k_k"][j], k_a=w["k_a"][j], r_k=w["r_k"][j],
                       lnx_w=w["lnx_w"][j], lnx_b=w["lnx_b"][j])
            oa, shift_new, rwkv_new = _rwkv_call(pa, st_shift[:, j], st_rwkv[:, j], prm)
            prm_b = dict(conv_w=w["conv_w"][j], conv_b=w["conv_b"][j], dt_bias=w["dt_bias"][j],
                         a_log=w["a_log"][j], d_skip=w["d_skip"][j], norm_b_w=w["norm_b_w"][j])
            ob, conv_new, ssm_new = _mamba_call(pb, st_conv[:, j], st_ssm[:, j], prm_b, groups)
            mixes, wout = [oa, ob], wb["w_out_ab"][j]
            new_rwkv.append(rwkv_new)
            new_shift.append(shift_new[:, 0])
            new_ssm.append(ssm_new)
            new_conv.append(conv_new)
        else:
            (pc,) = _inproj_call(x, nw_mix, sc_m, sh_m, [wb["w_in_c"][j]])
            oc, hgrn_new = _hgrn_call(pc, st_hgrn[:, j], w["lb_param"], w["norm_c_w"][j], j)
            mixes, wout = [oc], wb["w_out_c"][j]
            new_hgrn.append(hgrn_new)
        final_w = w["norm_out_w"].reshape(1, d) if layer == depth - 1 else None
        x = _post_call(x, mixes, wout, g_m, nw_ffn, sc_f, sh_f, g_f,
                       wb["w_gate"][layer], wb["w_up"][layer], wb["w_down"][layer], final_w)
    return (x, jnp.stack(new_rwkv, axis=1), jnp.stack(new_shift, axis=1), jnp.stack(new_ssm, axis=1),
            jnp.stack(new_conv, axis=1), jnp.stack(new_hgrn, axis=1))


def _run(x_prompt, x_sample, state_rwkv, state_rwkv_shift, state_ssm, state_conv, state_hgrn,
         c_prompt, c_sample, w):
    bp, bs = x_prompt.shape[0], x_sample.shape[0]
    rows = _round_up(bp + bs, 8)
    c_all = jnp.pad(jnp.concatenate([c_prompt, c_sample], axis=0), ((0, rows - bp - bs), (0, 0)))
    mod = _ada_call(c_all, w["ada_w"], w["ada_b"])
    wb = _prepare_weights(w)
    zeros = lambda s: jnp.zeros((bp,) + s.shape[1:], F32)
    outs_p = _trunk(x_prompt, mod[:, :bp], zeros(state_rwkv), zeros(state_rwkv_shift), zeros(state_ssm),
                    zeros(state_conv), zeros(state_hgrn), w, wb)
    outs_s = _trunk(x_sample, mod[:, bp:bp + bs], state_rwkv, state_rwkv_shift, state_ssm,
                    state_conv, state_hgrn, w, wb)
    return (outs_p[0], outs_s[0]) + outs_p[1:] + outs_s[1:]


def kernel(x_prompt, x_sample, state_rwkv, state_rwkv_shift, state_ssm, state_conv, state_hgrn, c_prompt, c_sample, norm_mix_w, norm_ffn_w, norm_out_w, ada_w, ada_b, w_in_ab, w_out_ab, mu_a, w0, w2, a0, a2, g2, k_k, k_a, r_k, lnx_w, lnx_b, conv_w, conv_b, dt_bias, a_log, d_skip, norm_b_w, w_in_c, w_out_c, lb_param, norm_c_w, w_gate, w_up, w_down):
    w = dict(norm_mix_w=norm_mix_w, norm_ffn_w=norm_ffn_w, norm_out_w=norm_out_w, ada_w=ada_w, ada_b=ada_b,
             w_in_ab=w_in_ab, w_out_ab=w_out_ab, mu_a=mu_a, w0=w0, w2=w2, a0=a0, a2=a2, g2=g2, k_k=k_k,
             k_a=k_a, r_k=r_k, lnx_w=lnx_w, lnx_b=lnx_b, conv_w=conv_w, conv_b=conv_b, dt_bias=dt_bias,
             a_log=a_log, d_skip=d_skip, norm_b_w=norm_b_w, w_in_c=w_in_c, w_out_c=w_out_c,
             lb_param=lb_param, norm_c_w=norm_c_w, w_gate=w_gate, w_up=w_up, w_down=w_down)
    return _run(x_prompt, x_sample, state_rwkv, state_rwkv_shift, state_ssm, state_conv, state_hgrn,
                c_prompt, c_sample, w)
```

```python
import functools
import math

import numpy as np
import jax
import jax.numpy as jnp
from jax import lax
from jax.experimental import pallas as pl
from jax.experimental.pallas import tpu as pltpu

F32 = jnp.float32
BF16 = jnp.bfloat16

LANE = 128
MXU_TILE = 256
VMEM_LIMIT = 56 * 2**20
ROW_TILE = 512
COL_CHUNK = 512
RWKV_CHUNK = 64
MAMBA_CHUNK = 128
HGRN_CHUNK = 64
MIX_SEQS = 2
LNX_EPS = 64e-5
RMS_EPS = 1e-6
GROUP_RMS_EPS = 1e-5


def _bdot(a, b):
    return jnp.dot(a.astype(BF16), b.astype(BF16), preferred_element_type=F32)


def _bdot_nt(a, b):
    return lax.dot_general(a.astype(BF16), b.astype(BF16), (((1,), (1,)), ((), ())),
                           preferred_element_type=F32)


def _bdot_tn(a, b):
    return lax.dot_general(a.astype(BF16), b.astype(BF16), (((0,), (0,)), ((), ())),
                           preferred_element_type=F32)


def _silu(x):
    return x * jax.nn.sigmoid(x)


def _const_spec(shape, single_buffer=False):
    nd = len(shape)
    if single_buffer:
        return pl.BlockSpec(shape, lambda *_: (0,) * nd, pipeline_mode=pl.Buffered(1))
    return pl.BlockSpec(shape, lambda *_: (0,) * nd)


def _params(n_axes):
    return pltpu.CompilerParams(dimension_semantics=("arbitrary",) * n_axes,
                                vmem_limit_bytes=VMEM_LIMIT)


def _row_tiling(b, t):
    if t >= ROW_TILE:
        assert t % ROW_TILE == 0
        return 1, ROW_TILE
    bb = max(1, min(b, ROW_TILE // t))
    while b % bb:
        bb -= 1
    return bb, t


def _col_chunks(n):
    return [(n0, min(n0 + COL_CHUNK, n)) for n0 in range(0, n, COL_CHUNK)]


def _ada_body(c_ref, w_ref, b_ref, o_ref):
    o_ref[0] = _bdot(_silu(c_ref[...]), w_ref[0]) + b_ref[0]


def _ada_call(c_all, ada_w, ada_b):
    depth, d, n = ada_w.shape
    r = c_all.shape[0]
    tn = 1024
    assert n % tn == 0
    return pl.pallas_call(
        _ada_body,
        out_shape=jax.ShapeDtypeStruct((depth, r, n), F32),
        grid=(depth, n // tn),
        in_specs=[pl.BlockSpec((r, d), lambda l, j: (0, 0)),
                  pl.BlockSpec((1, d, tn), lambda l, j: (l, 0, j)),
                  pl.BlockSpec((1, 1, tn), lambda l, j: (l, 0, j))],
        out_specs=pl.BlockSpec((1, r, tn), lambda l, j: (l, 0, j)),
        compiler_params=_params(2),
        name="ada_mod",
    )(c_all, ada_w, ada_b.reshape(depth, 1, n))


def _norm_mod(x, nw, sc, sh):
    ms = jnp.mean(x * x, axis=-1, keepdims=True)
    y = x * lax.rsqrt(ms + RMS_EPS) * nw
    return y * (1.0 + sc) + sh


def _inproj_body(*refs, n_out):
    x_ref, nw_ref, sc_ref, sh_ref = refs[:4]
    w_refs = refs[4:4 + n_out]
    o_refs = refs[4 + n_out:]
    bb, tt, d = x_ref.shape
    h = _norm_mod(x_ref[...], nw_ref[...], sc_ref[...], sh_ref[...])
    hb = h.reshape(bb * tt, d).astype(BF16)
    for w_ref, o_ref in zip(w_refs, o_refs):
        for n0, n1 in _col_chunks(w_ref.shape[1]):
            o_ref[:, :, n0:n1] = jnp.dot(hb, w_ref[:, n0:n1],
                                         preferred_element_type=F32).reshape(bb, tt, n1 - n0)


def _inproj_call(x, nw, sc, sh, weights):
    b, t, d = x.shape
    bb, tt = _row_tiling(b, t)
    n_out = len(weights)
    return pl.pallas_call(
        functools.partial(_inproj_body, n_out=n_out),
        out_shape=[jax.ShapeDtypeStruct((b, t, w.shape[1]), F32) for w in weights],
        grid=(b // bb, t // tt),
        in_specs=[pl.BlockSpec((bb, tt, d), lambda i, j: (i, j, 0)),
                  _const_spec((1, d)),
                  pl.BlockSpec((bb, 1, d), lambda i, j: (i, 0, 0)),
                  pl.BlockSpec((bb, 1, d), lambda i, j: (i, 0, 0))]
                 + [_const_spec(w.shape, single_buffer=True) for w in weights],
        out_specs=[pl.BlockSpec((bb, tt, w.shape[1]), lambda i, j: (i, j, 0)) for w in weights],
        compiler_params=_params(2),
        name="in_proj",
    )(x, nw, sc, sh, *weights)


def _post_body(*refs, n_mix, final):
    x_ref = refs[0]
    mix_refs = refs[1:1 + n_mix]
    (wout_ref, gm_ref, nw_ref, sc_ref, sh_ref, gf_ref, wg_ref, wu_ref, wd_ref) = refs[1 + n_mix:10 + n_mix]
    rest = refs[10 + n_mix:]
    if final:
        fw_ref, o_ref, act_ref = rest
    else:
        o_ref, act_ref = rest
    bb, tt, d = x_ref.shape
    rows = bb * tt

    mix = None
    off = 0
    for m_ref in mix_refs:
        wdt = m_ref.shape[-1]
        part = jnp.dot(m_ref[...].reshape(rows, wdt).astype(BF16), wout_ref[off:off + wdt, :],
                       preferred_element_type=F32)
        mix = part if mix is None else mix + part
        off += wdt
    x1 = x_ref[...] + gm_ref[...] * mix.reshape(bb, tt, d)

    h = _norm_mod(x1, nw_ref[...], sc_ref[...], sh_ref[...])
    hb = h.reshape(rows, d).astype(BF16)
    for f0, f1 in _col_chunks(wg_ref.shape[1]):
        gate = jnp.dot(hb, wg_ref[:, f0:f1], preferred_element_type=F32)
        up = jnp.dot(hb, wu_ref[:, f0:f1], preferred_element_type=F32)
        act_ref[:, f0:f1] = (_silu(gate) * up).astype(BF16)
    ffn = jnp.dot(act_ref[...], wd_ref[...], preferred_element_type=F32)
    x2 = x1 + gf_ref[...] * ffn.reshape(bb, tt, d)
    if final:
        ms = jnp.mean(x2 * x2, axis=-1, keepdims=True)
        x2 = x2 * lax.rsqrt(ms + RMS_EPS) * fw_ref[...]
    o_ref[...] = x2


def _post_call(x, mixes, wout, gm, nw, sc, sh, gf, wg, wu, wd, final_w):
    b, t, d = x.shape
    bb, tt = _row_tiling(b, t)
    final = final_w is not None
    row_spec = lambda w: pl.BlockSpec((bb, tt, w), lambda i, j: (i, j, 0))
    seq_spec = pl.BlockSpec((bb, 1, d), lambda i, j: (i, 0, 0))
    in_specs = ([row_spec(d)] + [row_spec(m.shape[-1]) for m in mixes]
                + [_const_spec(wout.shape, True), seq_spec, _const_spec((1, d)), seq_spec, seq_spec, seq_spec,
                   _const_spec(wg.shape, True), _const_spec(wu.shape, True), _const_spec(wd.shape, True)])
    args = [x, *mixes, wout, gm, nw, sc, sh, gf, wg, wu, wd]
    if final:
        in_specs.append(_const_spec((1, d)))
        args.append(final_w)
    return pl.pallas_call(
        functools.partial(_post_body, n_mix=len(mixes), final=final),
        out_shape=jax.ShapeDtypeStruct((b, t, d), F32),
        grid=(b // bb, t // tt),
        in_specs=in_specs,
        out_specs=row_spec(d),
        scratch_shapes=[pltpu.VMEM((bb * tt, wg.shape[1]), BF16)],
        compiler_params=_params(2),
        name="post_ffn",
    )(*args)


def _tri(c, strict=False, reps=1):
    row = lax.broadcasted_iota(jnp.int32, (c, reps * c), 0)
    col = lax.broadcasted_iota(jnp.int32, (c, reps * c), 1) & (c - 1)
    return (row > col) if strict else (row >= col)


def _block_tri(rows, c, upper=False):
    r = lax.broadcasted_iota(jnp.int32, (rows, rows), 0)
    q = lax.broadcasted_iota(jnp.int32, (rows, rows), 1)
    tri = (r <= q) if upper else (r >= q)
    return (tri & ((r ^ q) < c)).astype(BF16)


def _split3(x):
    x1 = x.astype(BF16)
    r1 = x - x1.astype(F32)
    x2 = r1.astype(BF16)
    x3 = (r1 - x2.astype(F32)).astype(BF16)
    return x1, x2, x3


def _split_dot(w01, x):
    x1, x2, x3 = _split3(x)
    dot = lambda piece: jnp.dot(w01, piece, preferred_element_type=F32)
    return dot(x1) + dot(x2) + dot(x3)


def _split_dot_tn(x, w01):
    x1, x2, x3 = _split3(x)
    dot = lambda piece: lax.dot_general(piece, w01, (((0,), (0,)), ((), ())), preferred_element_type=F32)
    return dot(x1) + dot(x2) + dot(x3)


def _expand_cols(x, width):
    k = x.shape[1]
    r = lax.broadcasted_iota(jnp.int32, (k, k * width), 0) * width
    q = lax.broadcasted_iota(jnp.int32, (k, k * width), 1)
    sel = ((q >= r) & (q < r + width)).astype(BF16)
    x1, x2, x3 = _split3(x)
    dot = lambda piece: jnp.dot(piece, sel, preferred_element_type=F32)
    return dot(x1) + dot(x2) + dot(x3)


def _group_sums(x, width):
    w = x.shape[1]
    tile = MXU_TILE if (w % MXU_TILE == 0 and MXU_TILE % width == 0) else w
    r = lax.broadcasted_iota(jnp.int32, (tile, tile), 0)
    q = lax.broadcasted_iota(jnp.int32, (tile, tile), 1)
    ones = ((r ^ q) < width).astype(BF16)
    pieces = _split3(x)
    cols = []
    for t0 in range(0, w, tile):
        acc = None
        for piece in pieces:
            part = jnp.dot(piece[:, t0:t0 + tile], ones, preferred_element_type=F32)
            acc = part if acc is None else acc + part
        cols.append(acc)
    return cols[0] if len(cols) == 1 else jnp.concatenate(cols, axis=1)


def _last_rows(x, bb, c):
    lasts = [x[(b + 1) * c - 1:(b + 1) * c, :] for b in range(bb)]
    tiled = [jnp.broadcast_to(l, (c, x.shape[1])) for l in lasts]
    return lasts, (tiled[0] if bb == 1 else jnp.concatenate(tiled, axis=0))


def _mixer_tiling(b, t, chunk):
    bb = MIX_SEQS if b % MIX_SEQS == 0 else 1
    return bb, math.gcd(chunk, t)


def _unit_lower_inverse(a_list, c):
    eye = (lax.broadcasted_iota(jnp.int32, (c, c), 0) == lax.broadcasted_iota(jnp.int32, (c, c), 1)).astype(F32)
    ns = list(a_list)
    ts = [eye + n for n in ns]
    p = 2
    while p < c:
        ns = [_bdot(n, n) for n in ns]
        ts = [t + _bdot(t, n) for t, n in zip(ts, ns)]
        p *= 2
    return ts


def _rwkv_body(p_ref, shift_ref, s0_ref, mu_ref, w0_ref, w2_ref, a0_ref, a2_ref, g2_ref,
               kk_ref, ka_ref, rk_ref, lw_ref, lb_ref,
               o_ref, shift_out_ref, s_ref, prev_ref, *, heads, lora):
    bb, c, cols = p_ref.shape
    rows = bb * c
    hd = s_ref.shape[-1]
    wa = heads * hd
    lw_, la_, lg_ = lora

    @pl.when(pl.program_id(1) == 0)
    def _():
        s_ref[...] = s0_ref[...]
        prev_ref[...] = shift_ref[...]

    p = p_ref[...].reshape(rows, cols)
    row = lax.broadcasted_iota(jnp.int32, (rows, cols), 0)
    p_prev = pltpu.roll(p, 1, axis=0)
    for b in range(bb):
        p_prev = jnp.where(row == b * c, prev_ref[b], p_prev)
        last = p[(b + 1) * c - 1:(b + 1) * c, :]
        prev_ref[b] = last
        shift_out_ref[b] = last
    pm = p + (p_prev - p) * mu_ref[...]

    r = pm[:, 0:wa]
    k = pm[:, wa:2 * wa]
    v = pm[:, 2 * wa:3 * wa]
    o1 = 3 * wa
    xw = pm[:, o1:o1 + lw_]
    xa = pm[:, o1 + lw_:o1 + lw_ + la_]
    xg = pm[:, o1 + lw_ + la_:o1 + lw_ + la_ + lg_]

    w = -jax.nn.softplus(-(w0_ref[...] + _bdot(jnp.tanh(xw), w2_ref[...]))) - 0.5
    logd = -jnp.exp(w)
    a = jax.nn.sigmoid(a0_ref[...] + _bdot(xa, a2_ref[...]))
    g = _bdot(jax.nn.sigmoid(xg), g2_ref[...])
    kk_raw = k * kk_ref[...]
    kk = kk_raw / jnp.maximum(jnp.sqrt(_group_sums(kk_raw * kk_raw, hd)), 1e-12)
    kka = kk * a
    k2 = k * (1.0 + (a - 1.0) * ka_ref[...])

    incl2 = _tri(c, reps=2)
    strict = _tri(c, strict=True)
    cum = _split_dot(_block_tri(rows, c), logd)
    tots, tot_rows = _last_rows(cum, bb, c)
    p_inv = jnp.exp(-cum)
    p_end = jnp.exp(tot_rows - cum)
    p_tots = [jnp.exp(t) for t in tots]
    at = (-kk * jnp.exp(cum - logd)).astype(BF16)
    rt = (r * jnp.exp(cum)).astype(BF16)
    bt = (kka * p_inv).astype(BF16)
    kt = (k2 * p_inv).astype(BF16)
    be = (kka * p_end).astype(BF16)
    ke = (k2 * p_end).astype(BF16)
    vb = v.astype(BF16)

    pairs = [(b, h) for b in range(bb) for h in range(heads)]
    idx = [(slice(b * c, (b + 1) * c), slice(h * hd, (h + 1) * hd)) for b, h in pairs]
    xs_in = [jnp.concatenate([at[rs, sl], rt[rs, sl]], axis=0) for rs, sl in idx]
    ys_in = [jnp.concatenate([bt[rs, sl], kt[rs, sl]], axis=0) for rs, sl in idx]
    ends = [jnp.concatenate([be[rs, sl], ke[rs, sl]], axis=0) for rs, sl in idx]
    vs = [vb[rs, sl] for rs, sl in idx]
    s0s = [s_ref[b, h] for b, h in pairs]
    grams = [_bdot_nt(x, y) for x, y in zip(xs_in, ys_in)]
    xss = [_bdot_nt(x, s0) for x, s0 in zip(xs_in, s0s)]
    a_abs = [jnp.where(strict, gm[:c, :c], 0.0) for gm in grams]
    ws = [xs[:c] + _bdot(jnp.where(strict, gm[:c, c:], 0.0), vh) for xs, gm, vh in zip(xss, grams, vs)]
    t_invs = _unit_lower_inverse(a_abs, c)
    us = [_bdot(t_inv, w_) for t_inv, w_ in zip(t_invs, ws)]
    uvs = [jnp.concatenate([u.astype(BF16), vh], axis=0) for u, vh in zip(us, vs)]
    outs = [xs[c:] + _bdot(jnp.where(incl2, gm[c:, :], 0.0), uv) for xs, gm, uv in zip(xss, grams, uvs)]
    news = [s0 * p_tots[b][:, sl] + _bdot_tn(uv, en)
            for s0, (b, _), (_, sl), uv, en in zip(s0s, pairs, idx, uvs, ends)]
    for i, ((b, h), (rs, sl)) in enumerate(zip(pairs, idx)):
        s_ref[b, h] = news[i]
        o_ref[b, :, sl] = outs[i]

    o = o_ref[...].reshape(rows, wa)
    dev = o - _group_sums(o, hd) * (1.0 / hd)
    var = _group_sums(dev * dev, hd) * (1.0 / hd)
    o = dev * lax.rsqrt(var + LNX_EPS) * lw_ref[...] + lb_ref[...]
    bonus = _group_sums(r * k2 * rk_ref[...], hd) * v
    o_ref[...] = ((o + bonus) * g).reshape(bb, c, wa)


def _rwkv_call(p, shift_prev, s0, prm):
    b, t, cols = p.shape
    heads, hd = s0.shape[1], s0.shape[2]
    wa = heads * hd
    bb, c = _mixer_tiling(b, t, RWKV_CHUNK)
    lora = (prm["w2"].shape[0], prm["a2"].shape[0], prm["g2"].shape[0])
    vec = lambda a: a.reshape(1, -1)
    consts = [vec(prm["mu"]), vec(prm["w0"]), prm["w2"], vec(prm["a0"]), prm["a2"], prm["g2"],
              vec(prm["k_k"]), vec(prm["k_a"]), vec(prm["r_k"]), vec(prm["lnx_w"]), vec(prm["lnx_b"])]
    seq_spec = lambda shape: pl.BlockSpec((bb,) + shape, lambda i, j: (i,) + (0,) * len(shape))
    return pl.pallas_call(
        functools.partial(_rwkv_body, heads=heads, lora=lora),
        out_shape=[jax.ShapeDtypeStruct((b, t, wa), F32),
                   jax.ShapeDtypeStruct((b, 1, cols), F32),
                   jax.ShapeDtypeStruct(s0.shape, F32)],
        grid=(b // bb, t // c),
        in_specs=[pl.BlockSpec((bb, c, cols), lambda i, j: (i, j, 0)),
                  seq_spec((1, cols)), seq_spec(s0.shape[1:])]
                 + [_const_spec(a.shape) for a in consts],
        out_specs=[pl.BlockSpec((bb, c, wa), lambda i, j: (i, j, 0)),
                   seq_spec((1, cols)), seq_spec(s0.shape[1:])],
        scratch_shapes=[pltpu.VMEM((bb, 1, cols), F32)],
        compiler_params=_params(2),
        name="rwkv7_mixer",
    )(p, shift_prev.reshape(b, 1, cols), s0, *consts)


def _mamba_body(p_ref, conv_ref, s0_ref, cw_ref, cb_ref, dtb_ref, alog_ref, dskip_ref, nw_ref,
                o_ref, conv_out_ref, s_ref, ubuf_ref, inter_ref, *, groups, conv_w):
    bb, c, cols = p_ref.shape
    rows = bb * c
    heads, hd, ns = s_ref.shape[1], s_ref.shape[2], s_ref.shape[3]
    wb = heads * hd
    xbc = conv_ref.shape[-1]
    hpg = heads // groups
    pad = 8
    hist = conv_w - 1

    @pl.when(pl.program_id(1) == 0)
    def _():
        s_ref[...] = s0_ref[...]
        for b in range(bb):
            ubuf_ref[b, 0:pad, :] = jnp.zeros((pad, xbc), F32)
            ubuf_ref[b, pad - hist:pad, :] = conv_ref[b]

    ys = []
    for b in range(bb):
        u = p_ref[b, :, wb:wb + xbc]
        ubuf_ref[b, pad:pad + c, :] = u
        y = cb_ref[...] + cw_ref[hist:hist + 1, :] * u
        for i in range(hist):
            y = y + cw_ref[i:i + 1, :] * ubuf_ref[b, pad - hist + i:pad - hist + i + c, :]
        conv_out_ref[b] = ubuf_ref[b, pad + c - hist:pad + c, :]
        ubuf_ref[b, 0:pad, :] = ubuf_ref[b, c:c + pad, :]
        ys.append(y)
    xc = _silu(ys[0] if bb == 1 else jnp.concatenate(ys, axis=0))
    xs = xc[:, 0:wb]
    bm = xc[:, wb:wb + groups * ns]
    cm = xc[:, wb + groups * ns:wb + 2 * groups * ns]
    z = p_ref[:, :, 0:wb].reshape(rows, wb)
    dt_raw = p_ref[:, :, wb + xbc:wb + xbc + heads].reshape(rows, heads)
    dt = jax.nn.softplus(dt_raw + dtb_ref[...])
    la = dt * (-jnp.exp(alog_ref[...]))

    incl = _tri(c)
    acum = _split_dot(_block_tri(rows, c), la)
    acum_t = _split_dot_tn(la, _block_tri(rows, c, upper=True))
    a_lasts, a_last_rows = _last_rows(acum, bb, c)
    e_tots = [jnp.exp(al) for al in a_lasts]
    stack = jnp.concatenate([dt, jnp.exp(acum), jnp.exp(a_last_rows - acum),
                             jnp.broadcast_to(dskip_ref[...], (8, heads))], axis=0)
    full = _expand_cols(stack, hd)
    ecum_f = full[rows:2 * rows]
    dskip_f = full[3 * rows:3 * rows + 1]
    xd_f = xs * full[0:rows]
    xde_f = xd_f * full[2 * rows:3 * rows]
    colb = _expand_cols(acum, c)

    rss = [slice(b * c, (b + 1) * c) for b in range(bb)]
    bgs = [[bm[rs, gi * ns:(gi + 1) * ns].astype(BF16) for gi in range(groups)] for rs in rss]
    cgs = [[cm[rs, gi * ns:(gi + 1) * ns].astype(BF16) for gi in range(groups)] for rs in rss]
    cbs = [[_bdot_nt(cg, bg) for cg, bg in zip(cgb, bgb)] for cgb, bgb in zip(cgs, bgs)]
    pairs = [(b, h) for b in range(bb) for h in range(heads)]
    sls = [slice(h * hd, (h + 1) * hd) for _, h in pairs]
    s0s = [s_ref[b, h] for b, h in pairs]
    mats = []
    for b, h in pairs:
        seg = colb[rss[b], h * c:(h + 1) * c] - acum_t[h:h + 1, rss[b]]
        decay = jnp.where(incl, jnp.exp(jnp.where(incl, seg, 0.0)), 0.0)
        mats.append((cbs[b][h // hpg] * decay).astype(BF16))
    intra = [_bdot(m, xd_f[rss[b], sl]) for m, (b, _), sl in zip(mats, pairs, sls)]
    inter = [_bdot_nt(cgs[b][h // hpg], s0) for (b, h), s0 in zip(pairs, s0s)]
    upd = [_bdot_tn(xde_f[rss[b], sl], bgs[b][h // hpg]) for (b, h), sl in zip(pairs, sls)]
    for i, ((b, h), sl) in enumerate(zip(pairs, sls)):
        s_ref[b, h] = s0s[i] * e_tots[b][:, h:h + 1] + upd[i]
        o_ref[b, :, sl] = intra[i]
        inter_ref[b, :, sl] = inter[i]

    y_all = o_ref[...].reshape(rows, wb) + inter_ref[...].reshape(rows, wb) * ecum_f + dskip_f * xs
    yv = y_all * _silu(z)
    gw = wb // groups
    for gi in range(groups):
        sl = slice(gi * gw, (gi + 1) * gw)
        yg = yv[:, sl]
        ms = jnp.mean(yg * yg, axis=-1, keepdims=True)
        o_ref[:, :, sl] = (yg * lax.rsqrt(ms + GROUP_RMS_EPS) * nw_ref[:, sl]).reshape(bb, c, gw)


def _mamba_call(p, conv_prev, s0, prm, groups):
    b, t, cols = p.shape
    heads, hd, ns = s0.shape[1:]
    wb = heads * hd
    hist, xbc = conv_prev.shape[1:]
    bb, c = _mixer_tiling(b, t, MAMBA_CHUNK)
    vec = lambda a: a.reshape(1, -1)
    consts = [prm["conv_w"], vec(prm["conv_b"]), vec(prm["dt_bias"]), vec(prm["a_log"]),
              vec(prm["d_skip"]), vec(prm["norm_b_w"])]
    seq_spec = lambda shape: pl.BlockSpec((bb,) + shape, lambda i, j: (i,) + (0,) * len(shape))
    return pl.pallas_call(
        functools.partial(_mamba_body, groups=groups, conv_w=hist + 1),
        out_shape=[jax.ShapeDtypeStruct((b, t, wb), F32),
                   jax.ShapeDtypeStruct(conv_prev.shape, F32),
                   jax.ShapeDtypeStruct(s0.shape, F32)],
        grid=(b // bb, t // c),
        in_specs=[pl.BlockSpec((bb, c, cols), lambda i, j: (i, j, 0)),
                  seq_spec((hist, xbc)), seq_spec(s0.shape[1:])]
                 + [_const_spec(a.shape) for a in consts],
        out_specs=[pl.BlockSpec((bb, c, wb), lambda i, j: (i, j, 0)),
                   seq_spec((hist, xbc)), seq_spec(s0.shape[1:])],
        scratch_shapes=[pltpu.VMEM((bb, c + 8, xbc), F32), pltpu.VMEM((bb, c, wb), F32)],
        compiler_params=_params(2),
        name="mamba2_mixer",
    )(p, conv_prev, s0, *consts)


def _hgrn_sum_matrix(c):
    t = np.arange(c)[:, None]
    j = np.arange(c)[None, :]
    blocks = [(j <= t), (j > t)]
    m = c // 2
    while m >= 1:
        mid = (t // (2 * m)) * (2 * m) + m - 1
        right = (t % (2 * m)) >= m
        blocks.append(np.where(right, (j > mid) & (j <= t), (j > t) & (j <= mid)))
        m //= 2
    return np.concatenate(blocks, axis=0).astype(np.float32)


def _hgrn_body(p_ref, s0_ref, sums_ref, lbp_ref, nw_ref, o_ref, s_ref, st_ref, *, layer):
    bb, c, cols = p_ref.shape
    rows = bb * c
    heads, dk, dv = s0_ref.shape[1:]
    wc = heads * dk
    nchunks = pl.num_programs(1)
    pairs = [(b, h) for b in range(bb) for h in range(heads)]

    @pl.when(pl.program_id(1) == 0)
    def _():
        for b, h in pairs:
            st_ref[b, h] = s0_ref[b, h].T

    lbp = lbp_ref[...]
    e = jnp.exp(lbp - jnp.max(lbp, axis=0, keepdims=True))
    soft = e / jnp.sum(e, axis=0, keepdims=True)
    lb = soft[0:1, :]
    for i in range(1, layer + 1):
        lb = lb + soft[i:i + 1, :]
    lb = lb - soft[0:1, :]

    p = p_ref[...].reshape(rows, cols)
    q = _silu(p[:, 0:wc])
    f = p[:, wc:2 * wc]
    v = p[:, 2 * wc:3 * wc]
    gate = p[:, 3 * wc:4 * wc]
    log_f = jnp.log(lb + (1.0 - lb) * jax.nn.sigmoid(f))
    k = (1.0 - lb) * jax.nn.sigmoid(-f)
    qk = q * k

    row = lax.broadcasted_iota(jnp.int32, (c, 1), 0)
    rr = lax.broadcasted_iota(jnp.int32, (c, c), 0)
    cc = lax.broadcasted_iota(jnp.int32, (c, c), 1)
    eye = rr == cc
    rss = [slice(b * c, (b + 1) * c) for b in range(bb)]
    qes, kes, f_tots, levels = [], [], [], []
    for rs in rss:
        sums = _split_dot(sums_ref[...], log_f[rs])
        bcum = sums[0:c]
        qes.append((q[rs] * jnp.exp(bcum)).astype(BF16))
        kes.append((k[rs] * jnp.exp(sums[c:2 * c])).astype(BF16))
        f_tots.append(jnp.exp(bcum[c - 1:c, :]))
        lv = []
        m = c // 2
        i = 2
        while m >= 1:
            wgt = jnp.exp(sums[i * c:(i + 1) * c])
            right = (row & (2 * m - 1)) >= m
            lv.append((jnp.where(right, q[rs] * wgt, 0.0).astype(BF16),
                       jnp.where(right, 0.0, k[rs] * wgt).astype(BF16),
                       (rr ^ cc) < 2 * m))
            m //= 2
            i += 1
        levels.append(lv)

    sls = [slice(h * dk, (h + 1) * dk) for _, h in pairs]
    atts = [jnp.where(eye, jnp.sum(qk[rss[b], sl], axis=-1, keepdims=True), 0.0) for (b, _), sl in zip(pairs, sls)]
    for li in range(len(levels[0])):
        atts = [att + jnp.where(levels[b][li][2], _bdot_nt(levels[b][li][0][:, sl], levels[b][li][1][:, sl]), 0.0)
                for att, (b, _), sl in zip(atts, pairs, sls)]
    vhs = [v[rss[b], sl].astype(BF16) for (b, _), sl in zip(pairs, sls)]
    sts = [st_ref[b, h] for b, h in pairs]
    intra = [_bdot(att, vh) for att, vh in zip(atts, vhs)]
    inter = [_bdot_nt(qes[b][:, sl], st) for (b, _), sl, st in zip(pairs, sls, sts)]
    upd = [_bdot_tn(vh, kes[b][:, sl]) for vh, (b, _), sl in zip(vhs, pairs, sls)]
    for i, ((b, h), sl) in enumerate(zip(pairs, sls)):
        st_ref[b, h] = sts[i] * f_tots[b][:, sl] + upd[i]
        o = intra[i] + inter[i]
        ms = jnp.mean(o * o, axis=-1, keepdims=True)
        o_ref[b, :, sl] = o * lax.rsqrt(ms + GROUP_RMS_EPS) * nw_ref[:, sl] * _silu(gate[rss[b], sl])

    @pl.when(pl.program_id(1) == nchunks - 1)
    def _():
        for b, h in pairs:
            s_ref[b, h] = st_ref[b, h].T


def _hgrn_call(p, s0, lb_param, norm_w, layer):
    b, t, cols = p.shape
    heads, dk, dv = s0.shape[1:]
    assert dk == dv
    wc = heads * dk
    bb, c = _mixer_tiling(b, t, HGRN_CHUNK)
    sums = jnp.asarray(_hgrn_sum_matrix(c), dtype=BF16)
    seq_spec = lambda shape: pl.BlockSpec((bb,) + shape, lambda i, j: (i,) + (0,) * len(shape))
    return pl.pallas_call(
        functools.partial(_hgrn_body, layer=layer),
        out_shape=[jax.ShapeDtypeStruct((b, t, wc), F32),
                   jax.ShapeDtypeStruct(s0.shape, F32)],
        grid=(b // bb, t // c),
        in_specs=[pl.BlockSpec((bb, c, cols), lambda i, j: (i, j, 0)),
                  seq_spec(s0.shape[1:]),
                  _const_spec(sums.shape), _const_spec(lb_param.shape), _const_spec((1, wc))],
        out_specs=[pl.BlockSpec((bb, c, wc), lambda i, j: (i, j, 0)),
                   seq_spec(s0.shape[1:])],
        scratch_shapes=[pltpu.VMEM((bb, heads, dv, dk), F32)],
        compiler_params=_params(2),
        name="hgrn2_mixer",
    )(p, s0, sums, lb_param, norm_w.reshape(1, wc))


def _round_up(n, m):
    return -(-n // m) * m


def _prepare_weights(w):
    a_cols = w["mu_a"].shape[1]
    b_cols = w["w_in_ab"].shape[2] - a_cols
    b_pad = _round_up(b_cols, LANE) - b_cols
    return dict(
        w_in_a=w["w_in_ab"][:, :, :a_cols].astype(BF16),
        w_in_b=jnp.pad(w["w_in_ab"][:, :, a_cols:], ((0, 0), (0, 0), (0, b_pad))).astype(BF16),
        w_out_ab=w["w_out_ab"].astype(BF16),
        w_in_c=w["w_in_c"].astype(BF16),
        w_out_c=w["w_out_c"].astype(BF16),
        w_gate=w["w_gate"].astype(BF16),
        w_up=w["w_up"].astype(BF16),
        w_down=w["w_down"].astype(BF16),
    )


def _trunk(x, mod, st_rwkv, st_shift, st_ssm, st_conv, st_hgrn, w, wb):
    depth = mod.shape[0]
    b, t, d = x.shape
    groups = (st_conv.shape[-1] - st_ssm.shape[2] * st_ssm.shape[3]) // (2 * st_ssm.shape[4])
    new_rwkv, new_shift, new_ssm, new_conv, new_hgrn = [], [], [], [], []
    for layer in range(depth):
        j = layer // 2
        sh_m, sc_m, g_m, sh_f, sc_f, g_f = (mod[layer, :, None, i * d:(i + 1) * d] for i in range(6))
        nw_mix = w["norm_mix_w"][layer].reshape(1, d)
        nw_ffn = w["norm_ffn_w"][layer].reshape(1, d)
        if layer % 2 == 0:
            pa, pb = _inproj_call(x, nw_mix, sc_m, sh_m, [wb["w_in_a"][j], wb["w_in_b"][j]])
            prm = dict(mu=w["mu_a"][j], w0=w["w0"][j], w2=w["w2"][j], a0=w["a0"][j], a2=w["a2"][j],
                       g2=w["g2"][j], k_k=w["k_k"][j], k_a=w["k_a"][j], r_k=w["r_k"][j],
                       lnx_w=w["lnx_w"][j], lnx_b=w["lnx_b"][j])
            oa, shift_new, rwkv_new = _rwkv_call(pa, st_shift[:, j], st_rwkv[:, j], prm)
            prm_b = dict(conv_w=w["conv_w"][j], conv_b=w["conv_b"][j], dt_bias=w["dt_bias"][j],
                         a_log=w["a_log"][j], d_skip=w["d_skip"][j], norm_b_w=w["norm_b_w"][j])
            ob, conv_new, ssm_new = _mamba_call(pb, st_conv[:, j], st_ssm[:, j], prm_b, groups)
            mixes, wout = [oa, ob], wb["w_out_ab"][j]
            new_rwkv.append(rwkv_new)
            new_shift.append(shift_new[:, 0])
            new_ssm.append(ssm_new)
            new_conv.append(conv_new)
        else:
            (pc,) = _inproj_call(x, nw_mix, sc_m, sh_m, [wb["w_in_c"][j]])
            oc, hgrn_new = _hgrn_call(pc, st_hgrn[:, j], w["lb_param"], w["norm_c_w"][j], j)
            mixes, wout = [oc], wb["w_out_c"][j]
            new_hgrn.append(hgrn_new)
        final_w = w["norm_out_w"].reshape(1, d) if layer == depth - 1 else None
        x = _post_call(x, mixes, wout, g_m, nw_ffn, sc_f, sh_f, g_f,
                       wb["w_gate"][layer], wb["w_up"][layer], wb["w_down"][layer], final_w)
    return (x, jnp.stack(new_rwkv, axis=1), jnp.stack(new_shift, axis=1), jnp.stack(new_ssm, axis=1),
            jnp.stack(new_conv, axis=1), jnp.stack(new_hgrn, axis=1))


def _run(x_prompt, x_sample, state_rwkv, state_rwkv_shift, state_ssm, state_conv, state_hgrn,
         c_prompt, c_sample, w):
    bp, bs = x_prompt.shape[0], x_sample.shape[0]
    rows = _round_up(bp + bs, 8)
    c_all = jnp.pad(jnp.concatenate([c_prompt, c_sample], axis=0), ((0, rows - bp - bs), (0, 0)))
    mod = _ada_call(c_all, w["ada_w"], w["ada_b"])
    wb = _prepare_weights(w)
    zeros = lambda s: jnp.zeros((bp,) + s.shape[1:], F32)
    outs_p = _trunk(x_prompt, mod[:, :bp], zeros(state_rwkv), zeros(state_rwkv_shift), zeros(state_ssm),
                    zeros(state_conv), zeros(state_hgrn), w, wb)
    outs_s = _trunk(x_sample, mod[:, bp:bp + bs], state_rwkv, state_rwkv_shift, state_ssm,
                    state_conv, state_hgrn, w, wb)
    return (outs_p[0], outs_s[0]) + outs_p[1:] + outs_s[1:]


def kernel(x_prompt, x_sample, state_rwkv, state_rwkv_shift, state_ssm, state_conv, state_hgrn, c_prompt, c_sample, norm_mix_w, norm_ffn_w, norm_out_w, ada_w, ada_b, w_in_ab, w_out_ab, mu_a, w0, w2, a0, a2, g2, k_k, k_a, r_k, lnx_w, lnx_b, conv_w, conv_b, dt_bias, a_log, d_skip, norm_b_w, w_in_c, w_out_c, lb_param, norm_c_w, w_gate, w_up, w_down):
    w = dict(norm_mix_w=norm_mix_w, norm_ffn_w=norm_ffn_w, norm_out_w=norm_out_w, ada_w=ada_w, ada_b=ada_b,
             w_in_ab=w_in_ab, w_out_ab=w_out_ab, mu_a=mu_a, w0=w0, w2=w2, a0=a0, a2=a2, g2=g2, k_k=k_k,
             k_a=k_a, r_k=r_k, lnx_w=lnx_w, lnx_b=lnx_b, conv_w=conv_w, conv_b=conv_b, dt_bias=dt_bias,
             a_log=a_log, d_skip=d_skip, norm_b_w=norm_b_w, w_in_c=w_in_c, w_out_c=w_out_c,
             lb_param=lb_param, norm_c_w=norm_c_w, w_gate=w_gate, w_up=w_up, w_down=w_down)
    return _run(x_prompt, x_sample, state_rwkv, state_rwkv_shift, state_ssm, state_conv, state_hgrn,
                c_prompt, c_sample, w)
```

```python
import functools
import math

import numpy as np
import jax
import jax.numpy as jnp
from jax import lax
from jax.experimental import pallas as pl
from jax.experimental.pallas import tpu as pltpu

F32 = jnp.float32
BF16 = jnp.bfloat16

LANE = 128
MXU_TILE = 256
VMEM_LIMIT = 56 * 2**20
ROW_TILE = 512
COL_CHUNK = 512
RWKV_CHUNK = 64
MAMBA_CHUNK = 128
HGRN_CHUNK = 64
MIX_SEQS = 2
LNX_EPS = 64e-5
RMS_EPS = 1e-6
GROUP_RMS_EPS = 1e-5


def _bdot(a, b):
    return jnp.dot(a.astype(BF16), b.astype(BF16), preferred_element_type=F32)


def _bdot_nt(a, b):
    return lax.dot_general(a.astype(BF16), b.astype(BF16), (((1,), (1,)), ((), ())),
                           preferred_element_type=F32)


def _bdot_tn(a, b):
    return lax.dot_general(a.astype(BF16), b.astype(BF16), (((0,), (0,)), ((), ())),
                           preferred_element_type=F32)


def _silu(x):
    return x * jax.nn.sigmoid(x)


def _const_spec(shape, single_buffer=False):
    nd = len(shape)
    if single_buffer:
        return pl.BlockSpec(shape, lambda *_: (0,) * nd, pipeline_mode=pl.Buffered(1))
    return pl.BlockSpec(shape, lambda *_: (0,) * nd)


def _params(n_axes):
    return pltpu.CompilerParams(dimension_semantics=("arbitrary",) * n_axes,
                                vmem_limit_bytes=VMEM_LIMIT)


def _row_tiling(b, t):
    if t >= ROW_TILE:
        assert t % ROW_TILE == 0
        return 1, ROW_TILE
    bb = max(1, min(b, ROW_TILE // t))
    while b % bb:
        bb -= 1
    return bb, t


def _col_chunks(n):
    return [(n0, min(n0 + COL_CHUNK, n)) for n0 in range(0, n, COL_CHUNK)]


def _ada_body(c_ref, w_ref, b_ref, o_ref):
    o_ref[0] = _bdot(_silu(c_ref[...]), w_ref[0]) + b_ref[0]


def _ada_call(c_all, ada_w, ada_b):
    depth, d, n = ada_w.shape
    r = c_all.shape[0]
    tn = 1024
    assert n % tn == 0
    return pl.pallas_call(
        _ada_body,
        out_shape=jax.ShapeDtypeStruct((depth, r, n), F32),
        grid=(depth, n // tn),
        in_specs=[pl.BlockSpec((r, d), lambda l, j: (0, 0)),
                  pl.BlockSpec((1, d, tn), lambda l, j: (l, 0, j)),
                  pl.BlockSpec((1, 1, tn), lambda l, j: (l, 0, j))],
        out_specs=pl.BlockSpec((1, r, tn), lambda l, j: (l, 0, j)),
        compiler_params=_params(2),
        name="ada_mod",
    )(c_all, ada_w, ada_b.reshape(depth, 1, n))


def _norm_mod(x, nw, sc, sh):
    ms = jnp.mean(x * x, axis=-1, keepdims=True)
    y = x * lax.rsqrt(ms + RMS_EPS) * nw
    return y * (1.0 + sc) + sh


class _Ticker:
    def __init__(self, thunks, points):
        self.thunks, self.total, self.points, self.calls = list(thunks), len(thunks), points, 0

    def __call__(self):
        self.calls += 1
        due = min(self.total, -(-self.calls * self.total // self.points))
        while self.total - len(self.thunks) < due:
            self.thunks.pop(0)()

    def flush(self):
        while self.thunks:
            self.thunks.pop(0)()


def _fused_body(*refs, init, core, n_in, n_out, points):
    x_cur_ref, x_next_ref, nw_ref, sc_ref, sh_ref, w_ref = refs[:6]
    ins = refs[6:6 + n_in]
    outs = refs[6 + n_in:6 + n_in + n_out]
    scr = refs[6 + n_in + n_out:-2]
    bufs = refs[-2:]
    bb, c, d = x_cur_ref.shape
    rows = bb * c
    j = pl.program_id(1)

    def projection(x_ref, dst_ref):
        h = _norm_mod(x_ref[...], nw_ref[...], sc_ref[...], sh_ref[...])
        hb = h.reshape(rows, d).astype(BF16)

        def piece(n0, n1):
            def run():
                dst_ref[:, n0:n1] = jnp.dot(hb, w_ref[:, n0:n1], preferred_element_type=F32)
            return run
        return [piece(n0, min(n0 + MXU_TILE, w_ref.shape[1])) for n0 in range(0, w_ref.shape[1], MXU_TILE)]

    @pl.when(j == 0)
    def _():
        init(ins, outs, scr)
        for run in projection(x_cur_ref, bufs[0]):
            run()

    for parity in range(2):
        @pl.when(lax.rem(j, 2) == parity)
        def _(parity=parity):
            tick = _Ticker(projection(x_next_ref, bufs[1 - parity]), points)
            core(bufs[parity][...], ins, outs, scr, tick)
            tick.flush()


def _fused_call(name, init, core, points, x, nw, sc, sh, w_in, chunk, ins, in_specs, out_shapes, out_specs,
                scratch):
    b, t, d = x.shape
    bb, c = _mixer_tiling(b, t, chunk)
    n = t // c
    cols = w_in.shape[1]
    seq_spec = pl.BlockSpec((bb, 1, d), lambda i, j: (i, 0, 0))
    return pl.pallas_call(
        functools.partial(_fused_body, init=init, core=core, n_in=len(ins), n_out=len(out_shapes), points=points),
        out_shape=out_shapes,
        grid=(b // bb, n),
        in_specs=[pl.BlockSpec((bb, c, d), lambda i, j: (i, j, 0)),
                  pl.BlockSpec((bb, c, d), lambda i, j: (i, jnp.minimum(j + 1, n - 1), 0)),
                  _const_spec((1, d)), seq_spec, seq_spec, _const_spec(w_in.shape, single_buffer=True)]
                 + in_specs(bb, c),
        out_specs=out_specs(bb, c),
        scratch_shapes=scratch(bb, c) + [pltpu.VMEM((bb * c, cols), F32)] * 2,
        compiler_params=_params(2),
        name=name,
    )(x, x, nw, sc, sh, w_in, *ins)


def _post_body(*refs, n_mix, final):
    x_ref = refs[0]
    mix_refs = refs[1:1 + n_mix]
    (wout_ref, gm_ref, nw_ref, sc_ref, sh_ref, gf_ref, wg_ref, wu_ref, wd_ref) = refs[1 + n_mix:10 + n_mix]
    rest = refs[10 + n_mix:]
    if final:
        fw_ref, o_ref, act_ref = rest
    else:
        o_ref, act_ref = rest
    bb, tt, d = x_ref.shape
    rows = bb * tt

    mix = None
    off = 0
    for m_ref in mix_refs:
        wdt = m_ref.shape[-1]
        part = jnp.dot(m_ref[...].reshape(rows, wdt).astype(BF16), wout_ref[off:off + wdt, :],
                       preferred_element_type=F32)
        mix = part if mix is None else mix + part
        off += wdt
    x1 = x_ref[...] + gm_ref[...] * mix.reshape(bb, tt, d)

    h = _norm_mod(x1, nw_ref[...], sc_ref[...], sh_ref[...])
    hb = h.reshape(rows, d).astype(BF16)
    for f0, f1 in _col_chunks(wg_ref.shape[1]):
        gate = jnp.dot(hb, wg_ref[:, f0:f1], preferred_element_type=F32)
        up = jnp.dot(hb, wu_ref[:, f0:f1], preferred_element_type=F32)
        act_ref[:, f0:f1] = (_silu(gate) * up).astype(BF16)
    ffn = jnp.dot(act_ref[...], wd_ref[...], preferred_element_type=F32)
    x2 = x1 + gf_ref[...] * ffn.reshape(bb, tt, d)
    if final:
        ms = jnp.mean(x2 * x2, axis=-1, keepdims=True)
        x2 = x2 * lax.rsqrt(ms + RMS_EPS) * fw_ref[...]
    o_ref[...] = x2


def _post_call(x, mixes, wout, gm, nw, sc, sh, gf, wg, wu, wd, final_w):
    b, t, d = x.shape
    bb, tt = _row_tiling(b, t)
    final = final_w is not None
    row_spec = lambda w: pl.BlockSpec((bb, tt, w), lambda i, j: (i, j, 0))
    seq_spec = pl.BlockSpec((bb, 1, d), lambda i, j: (i, 0, 0))
    in_specs = ([row_spec(d)] + [row_spec(m.shape[-1]) for m in mixes]
                + [_const_spec(wout.shape, True), seq_spec, _const_spec((1, d)), seq_spec, seq_spec, seq_spec,
                   _const_spec(wg.shape, True), _const_spec(wu.shape, True), _const_spec(wd.shape, True)])
    args = [x, *mixes, wout, gm, nw, sc, sh, gf, wg, wu, wd]
    if final:
        in_specs.append(_const_spec((1, d)))
        args.append(final_w)
    return pl.pallas_call(
        functools.partial(_post_body, n_mix=len(mixes), final=final),
        out_shape=jax.ShapeDtypeStruct((b, t, d), F32),
        grid=(b // bb, t // tt),
        in_specs=in_specs,
        out_specs=row_spec(d),
        scratch_shapes=[pltpu.VMEM((bb * tt, wg.shape[1]), BF16)],
        compiler_params=_params(2),
        name="post_ffn",
    )(*args)


def _tri(c, strict=False, reps=1):
    row = lax.broadcasted_iota(jnp.int32, (c, reps * c), 0)
    col = lax.broadcasted_iota(jnp.int32, (c, reps * c), 1) & (c - 1)
    return (row > col) if strict else (row >= col)


def _block_tri(rows, c, upper=False):
    r = lax.broadcasted_iota(jnp.int32, (rows, rows), 0)
    q = lax.broadcasted_iota(jnp.int32, (rows, rows), 1)
    tri = (r <= q) if upper else (r >= q)
    return (tri & ((r ^ q) < c)).astype(BF16)


def _split3(x):
    x1 = x.astype(BF16)
    r1 = x - x1.astype(F32)
    x2 = r1.astype(BF16)
    x3 = (r1 - x2.astype(F32)).astype(BF16)
    return x1, x2, x3


def _split_dot(w01, x):
    x1, x2, x3 = _split3(x)
    dot = lambda piece: jnp.dot(w01, piece, preferred_element_type=F32)
    return dot(x1) + dot(x2) + dot(x3)


def _split_dot_tn(x, w01):
    x1, x2, x3 = _split3(x)
    dot = lambda piece: lax.dot_general(piece, w01, (((0,), (0,)), ((), ())), preferred_element_type=F32)
    return dot(x1) + dot(x2) + dot(x3)


def _expand_cols(x, width):
    k = x.shape[1]
    r = lax.broadcasted_iota(jnp.int32, (k, k * width), 0) * width
    q = lax.broadcasted_iota(jnp.int32, (k, k * width), 1)
    sel = ((q >= r) & (q < r + width)).astype(BF16)
    x1, x2, x3 = _split3(x)
    dot = lambda piece: jnp.dot(piece, sel, preferred_element_type=F32)
    return dot(x1) + dot(x2) + dot(x3)


def _group_sums(x, width):
    w = x.shape[1]
    tile = MXU_TILE if (w % MXU_TILE == 0 and MXU_TILE % width == 0) else w
    r = lax.broadcasted_iota(jnp.int32, (tile, tile), 0)
    q = lax.broadcasted_iota(jnp.int32, (tile, tile), 1)
    ones = ((r ^ q) < width).astype(BF16)
    pieces = _split3(x)
    cols = []
    for t0 in range(0, w, tile):
        acc = None
        for piece in pieces:
            part = jnp.dot(piece[:, t0:t0 + tile], ones, preferred_element_type=F32)
            acc = part if acc is None else acc + part
        cols.append(acc)
    return cols[0] if len(cols) == 1 else jnp.concatenate(cols, axis=1)


def _last_rows(x, bb, c):
    lasts = [x[(b + 1) * c - 1:(b + 1) * c, :] for b in range(bb)]
    tiled = [jnp.broadcast_to(l, (c, x.shape[1])) for l in lasts]
    return lasts, (tiled[0] if bb == 1 else jnp.concatenate(tiled, axis=0))


def _mixer_tiling(b, t, chunk):
    bb = MIX_SEQS if b % MIX_SEQS == 0 else 1
    return bb, math.gcd(chunk, t)


def _unit_lower_inverse(a_list, c, tick):
    eye = (lax.broadcasted_iota(jnp.int32, (c, c), 0) == lax.broadcasted_iota(jnp.int32, (c, c), 1)).astype(F32)
    ns = list(a_list)
    ts = [eye + n for n in ns]
    p = 2
    while p < c:
        ns = [_bdot(n, n) for n in ns]
        tick()
        ts = [t + _bdot(t, n) for t, n in zip(ts, ns)]
        tick()
        p *= 2
    return ts


def _rwkv_init(ins, outs, scr):
    shift_ref, s0_ref = ins[:2]
    s_ref = outs[2]
    (prev_ref,) = scr
    s_ref[...] = s0_ref[...]
    prev_ref[...] = shift_ref[...]


def _rwkv_core(p, ins, outs, scr, tick, *, heads, lora):
    (shift_ref, s0_ref, mu_ref, w0_ref, w2_ref, a0_ref, a2_ref, g2_ref,
     kk_ref, ka_ref, rk_ref, lw_ref, lb_ref) = ins
    o_ref, shift_out_ref, s_ref = outs
    (prev_ref,) = scr
    bb, c, wa = o_ref.shape
    rows, cols = p.shape
    hd = s_ref.shape[-1]
    lw_, la_, lg_ = lora

    row = lax.broadcasted_iota(jnp.int32, (rows, cols), 0)
    p_prev = pltpu.roll(p, 1, axis=0)
    for b in range(bb):
        p_prev = jnp.where(row == b * c, prev_ref[b], p_prev)
        last = p[(b + 1) * c - 1:(b + 1) * c, :]
        prev_ref[b] = last
        shift_out_ref[b] = last
    pm = p + (p_prev - p) * mu_ref[...]

    r = pm[:, 0:wa]
    k = pm[:, wa:2 * wa]
    v = pm[:, 2 * wa:3 * wa]
    o1 = 3 * wa
    xw = pm[:, o1:o1 + lw_]
    xa = pm[:, o1 + lw_:o1 + lw_ + la_]
    xg = pm[:, o1 + lw_ + la_:o1 + lw_ + la_ + lg_]

    w = -jax.nn.softplus(-(w0_ref[...] + _bdot(jnp.tanh(xw), w2_ref[...]))) - 0.5
    logd = -jnp.exp(w)
    tick()
    a = jax.nn.sigmoid(a0_ref[...] + _bdot(xa, a2_ref[...]))
    g = _bdot(jax.nn.sigmoid(xg), g2_ref[...])
    kk_raw = k * kk_ref[...]
    kk = kk_raw / jnp.maximum(jnp.sqrt(_group_sums(kk_raw * kk_raw, hd)), 1e-12)
    kka = kk * a
    k2 = k * (1.0 + (a - 1.0) * ka_ref[...])

    incl2 = _tri(c, reps=2)
    strict = _tri(c, strict=True)
    cum = _split_dot(_block_tri(rows, c), logd)
    tick()
    tots, tot_rows = _last_rows(cum, bb, c)
    p_inv = jnp.exp(-cum)
    p_end = jnp.exp(tot_rows - cum)
    p_tots = [jnp.exp(t) for t in tots]
    at = (-kk * jnp.exp(cum - logd)).astype(BF16)
    rt = (r * jnp.exp(cum)).astype(BF16)
    bt = (kka * p_inv).astype(BF16)
    kt = (k2 * p_inv).astype(BF16)
    be = (kka * p_end).astype(BF16)
    ke = (k2 * p_end).astype(BF16)
    vb = v.astype(BF16)

    pairs = [(b, h) for b in range(bb) for h in range(heads)]
    idx = [(slice(b * c, (b + 1) * c), slice(h * hd, (h + 1) * hd)) for b, h in pairs]
    xs_in = [jnp.concatenate([at[rs, sl], rt[rs, sl]], axis=0) for rs, sl in idx]
    ys_in = [jnp.concatenate([bt[rs, sl], kt[rs, sl]], axis=0) for rs, sl in idx]
    ends = [jnp.concatenate([be[rs, sl], ke[rs, sl]], axis=0) for rs, sl in idx]
    vs = [vb[rs, sl] for rs, sl in idx]
    s0s = [s_ref[b, h] for b, h in pairs]
    tick()
    grams = [_bdot_nt(x, y) for x, y in zip(xs_in, ys_in)]
    tick()
    xss = [_bdot_nt(x, s0) for x, s0 in zip(xs_in, s0s)]
    tick()
    a_abs = [jnp.where(strict, gm[:c, :c], 0.0) for gm in grams]
    ws = [xs[:c] + _bdot(jnp.where(strict, gm[:c, c:], 0.0), vh) for xs, gm, vh in zip(xss, grams, vs)]
    tick()
    t_invs = _unit_lower_inverse(a_abs, c, tick)
    us = [_bdot(t_inv, w_) for t_inv, w_ in zip(t_invs, ws)]
    tick()
    uvs = [jnp.concatenate([u.astype(BF16), vh], axis=0) for u, vh in zip(us, vs)]
    res = [xs[c:] + _bdot(jnp.where(incl2, gm[c:, :], 0.0), uv) for xs, gm, uv in zip(xss, grams, uvs)]
    tick()
    news = [s0 * p_tots[b][:, sl] + _bdot_tn(uv, en)
            for s0, (b, _), (_, sl), uv, en in zip(s0s, pairs, idx, uvs, ends)]
    tick()
    for i, ((b, h), (rs, sl)) in enumerate(zip(pairs, idx)):
        s_ref[b, h] = news[i]
        o_ref[b, :, sl] = res[i]

    o = o_ref[...].reshape(rows, wa)
    dev = o - _group_sums(o, hd) * (1.0 / hd)
    var = _group_sums(dev * dev, hd) * (1.0 / hd)
    o = dev * lax.rsqrt(var + LNX_EPS) * lw_ref[...] + lb_ref[...]
    tick()
    bonus = _group_sums(r * k2 * rk_ref[...], hd) * v
    o_ref[...] = ((o + bonus) * g).reshape(bb, c, wa)


RWKV_POINTS = 20


def _rwkv_call(x, nw, sc, sh, w_in, shift_prev, s0, prm):
    b, t, _ = x.shape
    cols = w_in.shape[1]
    heads, hd = s0.shape[1], s0.shape[2]
    wa = heads * hd
    lora = (prm["w2"].shape[0], prm["a2"].shape[0], prm["g2"].shape[0])
    vec = lambda a: a.reshape(1, -1)
    consts = [vec(prm["mu"]), vec(prm["w0"]), prm["w2"], vec(prm["a0"]), prm["a2"], prm["g2"],
              vec(prm["k_k"]), vec(prm["k_a"]), vec(prm["r_k"]), vec(prm["lnx_w"]), vec(prm["lnx_b"])]
    seq = lambda bb, shape: pl.BlockSpec((bb,) + shape, lambda i, j: (i,) + (0,) * len(shape))
    return _fused_call(
        "rwkv7_mixer", _rwkv_init, functools.partial(_rwkv_core, heads=heads, lora=lora), RWKV_POINTS,
        x, nw, sc, sh, w_in, RWKV_CHUNK,
        ins=[shift_prev.reshape(b, 1, cols), s0, *consts],
        in_specs=lambda bb, c: [seq(bb, (1, cols)), seq(bb, s0.shape[1:])] + [_const_spec(a.shape) for a in consts],
        out_shapes=[jax.ShapeDtypeStruct((b, t, wa), F32),
                    jax.ShapeDtypeStruct((b, 1, cols), F32),
                    jax.ShapeDtypeStruct(s0.shape, F32)],
        out_specs=lambda bb, c: [pl.BlockSpec((bb, c, wa), lambda i, j: (i, j, 0)),
                                 seq(bb, (1, cols)), seq(bb, s0.shape[1:])],
        scratch=lambda bb, c: [pltpu.VMEM((bb, 1, cols), F32)])


def _mamba_init(ins, outs, scr, *, conv_w):
    conv_ref, s0_ref = ins[:2]
    s_ref = outs[2]
    ubuf_ref = scr[0]
    bb, _, xbc = conv_ref.shape
    pad, hist = 8, conv_w - 1
    s_ref[...] = s0_ref[...]
    for b in range(bb):
        ubuf_ref[b, 0:pad, :] = jnp.zeros((pad, xbc), F32)
        ubuf_ref[b, pad - hist:pad, :] = conv_ref[b]


def _mamba_core(p, ins, outs, scr, tick, *, groups, conv_w):
    conv_ref, s0_ref, cw_ref, cb_ref, dtb_ref, alog_ref, dskip_ref, nw_ref = ins
    o_ref, conv_out_ref, s_ref = outs
    ubuf_ref, inter_ref = scr
    bb, c, wb = o_ref.shape
    rows = bb * c
    heads, hd, ns = s_ref.shape[1], s_ref.shape[2], s_ref.shape[3]
    xbc = conv_ref.shape[-1]
    hpg = heads // groups
    pad = 8
    hist = conv_w - 1

    ys = []
    for b in range(bb):
        u = p[b * c:(b + 1) * c, wb:wb + xbc]
        ubuf_ref[b, pad:pad + c, :] = u
        y = cb_ref[...] + cw_ref[hist:hist + 1, :] * u
        for i in range(hist):
            y = y + cw_ref[i:i + 1, :] * ubuf_ref[b, pad - hist + i:pad - hist + i + c, :]
        conv_out_ref[b] = ubuf_ref[b, pad + c - hist:pad + c, :]
        ubuf_ref[b, 0:pad, :] = ubuf_ref[b, c:c + pad, :]
        ys.append(y)
        tick()
    xc = _silu(ys[0] if bb == 1 else jnp.concatenate(ys, axis=0))
    xs = xc[:, 0:wb]
    bm = xc[:, wb:wb + groups * ns]
    cm = xc[:, wb + groups * ns:wb + 2 * groups * ns]
    z = p[:, 0:wb]
    dt_raw = p[:, wb + xbc:wb + xbc + heads]
    dt = jax.nn.softplus(dt_raw + dtb_ref[...])
    la = dt * (-jnp.exp(alog_ref[...]))

    incl = _tri(c)
    acum = _split_dot(_block_tri(rows, c), la)
    acum_t = _split_dot_tn(la, _block_tri(rows, c, upper=True))
    a_lasts, a_last_rows = _last_rows(acum, bb, c)
    e_tots = [jnp.exp(al) for al in a_lasts]
    stack = jnp.concatenate([dt, jnp.exp(acum), jnp.exp(a_last_rows - acum),
                             jnp.broadcast_to(dskip_ref[...], (8, heads))], axis=0)
    full = _expand_cols(stack, hd)
    ecum_f = full[rows:2 * rows]
    dskip_f = full[3 * rows:3 * rows + 1]
    xd_f = xs * full[0:rows]
    xde_f = xd_f * full[2 * rows:3 * rows]
    colb = _expand_cols(acum, c)
    tick()

    rss = [slice(b * c, (b + 1) * c) for b in range(bb)]
    bgs = [[bm[rs, gi * ns:(gi + 1) * ns].astype(BF16) for gi in range(groups)] for rs in rss]
    cgs = [[cm[rs, gi * ns:(gi + 1) * ns].astype(BF16) for gi in range(groups)] for rs in rss]
    cbs = [[_bdot_nt(cg, bg) for cg, bg in zip(cgb, bgb)] for cgb, bgb in zip(cgs, bgs)]
    tick()
    pairs = [(b, h) for b in range(bb) for h in range(heads)]
    sls = [slice(h * hd, (h + 1) * hd) for _, h in pairs]
    s0s = [s_ref[b, h] for b, h in pairs]
    mats = []
    for b, h in pairs:
        seg = colb[rss[b], h * c:(h + 1) * c] - acum_t[h:h + 1, rss[b]]
        decay = jnp.where(incl, jnp.exp(jnp.where(incl, seg, 0.0)), 0.0)
        mats.append((cbs[b][h // hpg] * decay).astype(BF16))
        if h % 4 == 3:
            tick()
    intra = [_bdot(m, xd_f[rss[b], sl]) for m, (b, _), sl in zip(mats, pairs, sls)]
    tick()
    inter = [_bdot_nt(cgs[b][h // hpg], s0) for (b, h), s0 in zip(pairs, s0s)]
    tick()
    upd = [_bdot_tn(xde_f[rss[b], sl], bgs[b][h // hpg]) for (b, h), sl in zip(pairs, sls)]
    tick()
    for i, ((b, h), sl) in enumerate(zip(pairs, sls)):
        s_ref[b, h] = s0s[i] * e_tots[b][:, h:h + 1] + upd[i]
        o_ref[b, :, sl] = intra[i]
        inter_ref[b, :, sl] = inter[i]

    y_all = o_ref[...].reshape(rows, wb) + inter_ref[...].reshape(rows, wb) * ecum_f + dskip_f * xs
    yv = y_all * _silu(z)
    gw = wb // groups
    for gi in range(groups):
        sl = slice(gi * gw, (gi + 1) * gw)
        yg = yv[:, sl]
        ms = jnp.mean(yg * yg, axis=-1, keepdims=True)
        o_ref[:, :, sl] = (yg * lax.rsqrt(ms + GROUP_RMS_EPS) * nw_ref[:, sl]).reshape(bb, c, gw)


MAMBA_POINTS = 16


def _mamba_call(x, nw, sc, sh, w_in, conv_prev, s0, prm, groups):
    b, t, _ = x.shape
    heads, hd, ns = s0.shape[1:]
    wb = heads * hd
    hist, xbc = conv_prev.shape[1:]
    vec = lambda a: a.reshape(1, -1)
    consts = [prm["conv_w"], vec(prm["conv_b"]), vec(prm["dt_bias"]), vec(prm["a_log"]),
              vec(prm["d_skip"]), vec(prm["norm_b_w"])]
    seq = lambda bb, shape: pl.BlockSpec((bb,) + shape, lambda i, j: (i,) + (0,) * len(shape))
    return _fused_call(
        "mamba2_mixer", functools.partial(_mamba_init, conv_w=hist + 1),
        functools.partial(_mamba_core, groups=groups, conv_w=hist + 1), MAMBA_POINTS,
        x, nw, sc, sh, w_in, MAMBA_CHUNK,
        ins=[conv_prev, s0, *consts],
        in_specs=lambda bb, c: [seq(bb, (hist, xbc)), seq(bb, s0.shape[1:])] + [_const_spec(a.shape) for a in consts],
        out_shapes=[jax.ShapeDtypeStruct((b, t, wb), F32),
                    jax.ShapeDtypeStruct(conv_prev.shape, F32),
                    jax.ShapeDtypeStruct(s0.shape, F32)],
        out_specs=lambda bb, c: [pl.BlockSpec((bb, c, wb), lambda i, j: (i, j, 0)),
                                 seq(bb, (hist, xbc)), seq(bb, s0.shape[1:])],
        scratch=lambda bb, c: [pltpu.VMEM((bb, c + 8, xbc), F32), pltpu.VMEM((bb, c, wb), F32)])


def _hgrn_sum_matrix(c):
    t = np.arange(c)[:, None]
    j = np.arange(c)[None, :]
    blocks = [(j <= t), (j > t)]
    m = c // 2
    while m >= 1:
        mid = (t // (2 * m)) * (2 * m) + m - 1
        right = (t % (2 * m)) >= m
        blocks.append(np.where(right, (j > mid) & (j <= t), (j > t) & (j <= mid)))
        m //= 2
    return np.concatenate(blocks, axis=0).astype(np.float32)


def _hgrn_init(ins, outs, scr):
    s0_ref = ins[0]
    (st_ref,) = scr
    bb, heads = s0_ref.shape[:2]
    for b in range(bb):
        for h in range(heads):
            st_ref[b, h] = s0_ref[b, h].T


def _hgrn_core(p, ins, outs, scr, tick, *, layer):
    s0_ref, sums_ref, lbp_ref, nw_ref = ins
    o_ref, s_ref = outs
    (st_ref,) = scr
    bb, c, wc = o_ref.shape
    rows, cols = p.shape
    heads, dk, dv = s0_ref.shape[1:]
    nchunks = pl.num_programs(1)
    pairs = [(b, h) for b in range(bb) for h in range(heads)]

    lbp = lbp_ref[...]
    e = jnp.exp(lbp - jnp.max(lbp, axis=0, keepdims=True))
    soft = e / jnp.sum(e, axis=0, keepdims=True)
    lb = soft[0:1, :]
    for i in range(1, layer + 1):
        lb = lb + soft[i:i + 1, :]
    lb = lb - soft[0:1, :]

    q = _silu(p[:, 0:wc])
    f = p[:, wc:2 * wc]
    v = p[:, 2 * wc:3 * wc]
    gate = p[:, 3 * wc:4 * wc]
    log_f = jnp.log(lb + (1.0 - lb) * jax.nn.sigmoid(f))
    k = (1.0 - lb) * jax.nn.sigmoid(-f)
    qk = q * k

    row = lax.broadcasted_iota(jnp.int32, (c, 1), 0)
    rr = lax.broadcasted_iota(jnp.int32, (c, c), 0)
    cc = lax.broadcasted_iota(jnp.int32, (c, c), 1)
    eye = rr == cc
    rss = [slice(b * c, (b + 1) * c) for b in range(bb)]
    qes, kes, f_tots, levels = [], [], [], []
    for rs in rss:
        sums = _split_dot(sums_ref[...], log_f[rs])
        bcum = sums[0:c]
        tick()
        qes.append((q[rs] * jnp.exp(bcum)).astype(BF16))
        kes.append((k[rs] * jnp.exp(sums[c:2 * c])).astype(BF16))
        f_tots.append(jnp.exp(bcum[c - 1:c, :]))
        lv = []
        m = c // 2
        i = 2
        while m >= 1:
            wgt = jnp.exp(sums[i * c:(i + 1) * c])
            right = (row & (2 * m - 1)) >= m
            lv.append((jnp.where(right, q[rs] * wgt, 0.0).astype(BF16),
                       jnp.where(right, 0.0, k[rs] * wgt).astype(BF16),
                       (rr ^ cc) < 2 * m))
            m //= 2
            i += 1
            tick()
        levels.append(lv)

    sls = [slice(h * dk, (h + 1) * dk) for _, h in pairs]
    atts = [jnp.where(eye, jnp.sum(qk[rss[b], sl], axis=-1, keepdims=True), 0.0) for (b, _), sl in zip(pairs, sls)]
    for li in range(len(levels[0])):
        atts = [att + jnp.where(levels[b][li][2], _bdot_nt(levels[b][li][0][:, sl], levels[b][li][1][:, sl]), 0.0)
                for att, (b, _), sl in zip(atts, pairs, sls)]
        tick()
    vhs = [v[rss[b], sl].astype(BF16) for (b, _), sl in zip(pairs, sls)]
    sts = [st_ref[b, h] for b, h in pairs]
    intra = [_bdot(att, vh) for att, vh in zip(atts, vhs)]
    tick()
    inter = [_bdot_nt(qes[b][:, sl], st) for (b, _), sl, st in zip(pairs, sls, sts)]
    tick()
    upd = [_bdot_tn(vh, kes[b][:, sl]) for vh, (b, _), sl in zip(vhs, pairs, sls)]
    tick()
    for i, ((b, h), sl) in enumerate(zip(pairs, sls)):
        st_ref[b, h] = sts[i] * f_tots[b][:, sl] + upd[i]
        o = intra[i] + inter[i]
        ms = jnp.mean(o * o, axis=-1, keepdims=True)
        o_ref[b, :, sl] = o * lax.rsqrt(ms + GROUP_RMS_EPS) * nw_ref[:, sl] * _silu(gate[rss[b], sl])

    @pl.when(pl.program_id(1) == nchunks - 1)
    def _():
        for b, h in pairs:
            s_ref[b, h] = st_ref[b, h].T


HGRN_POINTS = 23


def _hgrn_call(x, nw, sc, sh, w_in, s0, lb_param, norm_w, layer):
    b, t, _ = x.shape
    heads, dk, dv = s0.shape[1:]
    assert dk == dv
    wc = heads * dk
    sums = jnp.asarray(_hgrn_sum_matrix(math.gcd(HGRN_CHUNK, t)), dtype=BF16)
    ins = [s0, sums, lb_param, norm_w.reshape(1, wc)]
    seq = lambda bb, shape: pl.BlockSpec((bb,) + shape, lambda i, j: (i,) + (0,) * len(shape))
    return _fused_call(
        "hgrn2_mixer", _hgrn_init, functools.partial(_hgrn_core, layer=layer), HGRN_POINTS,
        x, nw, sc, sh, w_in, HGRN_CHUNK,
        ins=ins,
        in_specs=lambda bb, c: [seq(bb, s0.shape[1:])] + [_const_spec(a.shape) for a in ins[1:]],
        out_shapes=[jax.ShapeDtypeStruct((b, t, wc), F32), jax.ShapeDtypeStruct(s0.shape, F32)],
        out_specs=lambda bb, c: [pl.BlockSpec((bb, c, wc), lambda i, j: (i, j, 0)), seq(bb, s0.shape[1:])],
        scratch=lambda bb, c: [pltpu.VMEM((bb, heads, dv, dk), F32)])


def _round_up(n, m):
    return -(-n // m) * m


def _prepare_weights(w):
    a_cols = w["mu_a"].shape[1]
    b_cols = w["w_in_ab"].shape[2] - a_cols
    b_pad = _round_up(b_cols, LANE) - b_cols
    return dict(
        w_in_a=w["w_in_ab"][:, :, :a_cols].astype(BF16),
        w_in_b=jnp.pad(w["w_in_ab"][:, :, a_cols:], ((0, 0), (0, 0), (0, b_pad))).astype(BF16),
        w_out_ab=w["w_out_ab"].astype(BF16),
        w_in_c=w["w_in_c"].astype(BF16),
        w_out_c=w["w_out_c"].astype(BF16),
        w_gate=w["w_gate"].astype(BF16),
        w_up=w["w_up"].astype(BF16),
        w_down=w["w_down"].astype(BF16),
    )


def _trunk(x, mod, st_rwkv, st_shift, st_ssm, st_conv, st_hgrn, w, wb):
    depth = mod.shape[0]
    b, t, d = x.shape
    groups = (st_conv.shape[-1] - st_ssm.shape[2] * st_ssm.shape[3]) // (2 * st_ssm.shape[4])
    new_rwkv, new_shift, new_ssm, new_conv, new_hgrn = [], [], [], [], []
    for layer in range(depth):
        j = layer // 2
        sh_m, sc_m, g_m, sh_f, sc_f, g_f = (mod[layer, :, None, i * d:(i + 1) * d] for i in range(6))
        nw_mix = w["norm_mix_w"][layer].reshape(1, d)
        nw_ffn = w["norm_ffn_w"][layer].reshape(1, d)
        if layer % 2 == 0:
            prm = dict(mu=w["mu_a"][j], w0=w["w0"][j], w2=w["w2"][j], a0=w["a0"][j], a2=w["a2"][j],
                       g2=w["g2"][j], k_k=w["k_k"][j], k_a=w["k_a"][j], r_k=w["r_k"][j],
                       lnx_w=w["lnx_w"][j], lnx_b=w["lnx_b"][j])
            oa, shift_new, rwkv_new = _rwkv_call(x, nw_mix, sc_m, sh_m, wb["w_in_a"][j],
                                                 st_shift[:, j], st_rwkv[:, j], prm)
            prm_b = dict(conv_w=w["conv_w"][j], conv_b=w["conv_b"][j], dt_bias=w["dt_bias"][j],
                         a_log=w["a_log"][j], d_skip=w["d_skip"][j], norm_b_w=w["norm_b_w"][j])
            ob, conv_new, ssm_new = _mamba_call(x, nw_mix, sc_m, sh_m, wb["w_in_b"][j],
                                                st_conv[:, j], st_ssm[:, j], prm_b, groups)
            mixes, wout = [oa, ob], wb["w_out_ab"][j]
            new_rwkv.append(rwkv_new)
            new_shift.append(shift_new[:, 0])
            new_ssm.append(ssm_new)
            new_conv.append(conv_new)
        else:
            oc, hgrn_new = _hgrn_call(x, nw_mix, sc_m, sh_m, wb["w_in_c"][j],
                                      st_hgrn[:, j], w["lb_param"], w["norm_c_w"][j], j)
            mixes, wout = [oc], wb["w_out_c"][j]
            new_hgrn.append(hgrn_new)
        final_w = w["norm_out_w"].reshape(1, d) if layer == depth - 1 else None
        x = _post_call(x, mixes, wout, g_m, nw_ffn, sc_f, sh_f, g_f,
                       wb["w_gate"][layer], wb["w_up"][layer], wb["w_down"][layer], final_w)
    return (x, jnp.stack(new_rwkv, axis=1), jnp.stack(new_shift, axis=1), jnp.stack(new_ssm, axis=1),
            jnp.stack(new_conv, axis=1), jnp.stack(new_hgrn, axis=1))


def _run(x_prompt, x_sample, state_rwkv, state_rwkv_shift, state_ssm, state_conv, state_hgrn,
         c_prompt, c_sample, w):
    bp, bs = x_prompt.shape[0], x_sample.shape[0]
    rows = _round_up(bp + bs, 8)
    c_all = jnp.pad(jnp.concatenate([c_prompt, c_sample], axis=0), ((0, rows - bp - bs), (0, 0)))
    mod = _ada_call(c_all, w["ada_w"], w["ada_b"])
    wb = _prepare_weights(w)
    zeros = lambda s: jnp.zeros((bp,) + s.shape[1:], F32)
    outs_p = _trunk(x_prompt, mod[:, :bp], zeros(state_rwkv), zeros(state_rwkv_shift), zeros(state_ssm),
                    zeros(state_conv), zeros(state_hgrn), w, wb)
    outs_s = _trunk(x_sample, mod[:, bp:bp + bs], state_rwkv, state_rwkv_shift, state_ssm,
                    state_conv, state_hgrn, w, wb)
    return (outs_p[0], outs_s[0]) + outs_p[1:] + outs_s[1:]


def kernel(x_prompt, x_sample, state_rwkv, state_rwkv_shift, state_ssm, state_conv, state_hgrn, c_prompt, c_sample, norm_mix_w, norm_ffn_w, norm_out_w, ada_w, ada_b, w_in_ab, w_out_ab, mu_a, w0, w2, a0, a2, g2, k_k, k_a, r_k, lnx_w, lnx_b, conv_w, conv_b, dt_bias, a_log, d_skip, norm_b_w, w_in_c, w_out_c, lb_param, norm_c_w, w_gate, w_up, w_down):
    w = dict(norm_mix_w=norm_mix_w, norm_ffn_w=norm_ffn_w, norm_out_w=norm_out_w, ada_w=ada_w, ada_b=ada_b,
             w_in_ab=w_in_ab, w_out_ab=w_out_ab, mu_a=mu_a, w0=w0, w2=w2, a0=a0, a2=a2, g2=g2, k_k=k_k,
             k_a=k_a, r_k=r_k, lnx_w=lnx_w, lnx_b=lnx_b, conv_w=conv_w, conv_b=conv_b, dt_bias=dt_bias,
             a_log=a_log, d_skip=d_skip, norm_b_w=norm_b_w, w_in_c=w_in_c, w_out_c=w_out_c,
             lb_param=lb_param, norm_c_w=norm_c_w, w_gate=w_gate, w_up=w_up, w_down=w_down)
    return _run(x_prompt, x_sample, state_rwkv, state_rwkv_shift, state_ssm, state_conv, state_hgrn,
                c_prompt, c_sample, w)
```

```python
import functools
import math

import numpy as np
import jax
import jax.numpy as jnp
from jax import lax
from jax.experimental import pallas as pl
from jax.experimental.pallas import tpu as pltpu

F32 = jnp.float32
BF16 = jnp.bfloat16

LANE = 128
MXU_TILE = 256
VMEM_LIMIT = 56 * 2**20
ROW_TILE = 512
COL_CHUNK = 512
RWKV_CHUNK = 64
MAMBA_CHUNK = 128
HGRN_CHUNK = 64
MIX_SEQS = 2
LNX_EPS = 64e-5
RMS_EPS = 1e-6
GROUP_RMS_EPS = 1e-5


def _bdot(a, b):
    return jnp.dot(a.astype(BF16), b.astype(BF16), preferred_element_type=F32)


def _bdot_nt(a, b):
    return lax.dot_general(a.astype(BF16), b.astype(BF16), (((1,), (1,)), ((), ())),
                           preferred_element_type=F32)


def _bdot_tn(a, b):
    return lax.dot_general(a.astype(BF16), b.astype(BF16), (((0,), (0,)), ((), ())),
                           preferred_element_type=F32)


def _silu(x):
    return x * jax.nn.sigmoid(x)


def _const_spec(shape, single_buffer=False):
    nd = len(shape)
    if single_buffer:
        return pl.BlockSpec(shape, lambda *_: (0,) * nd, pipeline_mode=pl.Buffered(1))
    return pl.BlockSpec(shape, lambda *_: (0,) * nd)


def _params(n_axes):
    return pltpu.CompilerParams(dimension_semantics=("arbitrary",) * n_axes,
                                vmem_limit_bytes=VMEM_LIMIT)


def _row_tiling(b, t):
    if t >= ROW_TILE:
        assert t % ROW_TILE == 0
        return 1, ROW_TILE
    bb = max(1, min(b, ROW_TILE // t))
    while b % bb:
        bb -= 1
    return bb, t


def _col_chunks(n):
    return [(n0, min(n0 + COL_CHUNK, n)) for n0 in range(0, n, COL_CHUNK)]


def _ada_body(c_ref, w_ref, b_ref, o_ref):
    o_ref[0] = _bdot(_silu(c_ref[...]), w_ref[0]) + b_ref[0]


def _ada_call(c_all, ada_w, ada_b):
    depth, d, n = ada_w.shape
    r = c_all.shape[0]
    tn = 1024
    assert n % tn == 0
    return pl.pallas_call(
        _ada_body,
        out_shape=jax.ShapeDtypeStruct((depth, r, n), F32),
        grid=(depth, n // tn),
        in_specs=[pl.BlockSpec((r, d), lambda l, j: (0, 0)),
                  pl.BlockSpec((1, d, tn), lambda l, j: (l, 0, j)),
                  pl.BlockSpec((1, 1, tn), lambda l, j: (l, 0, j))],
        out_specs=pl.BlockSpec((1, r, tn), lambda l, j: (l, 0, j)),
        compiler_params=_params(2),
        name="ada_mod",
    )(c_all, ada_w, ada_b.reshape(depth, 1, n))


def _norm_mod(x, nw, sc, sh):
    ms = jnp.mean(x * x, axis=-1, keepdims=True)
    y = x * lax.rsqrt(ms + RMS_EPS) * nw
    return y * (1.0 + sc) + sh


class _Ticker:
    def __init__(self, thunks, points):
        self.thunks, self.total, self.points, self.calls = list(thunks), len(thunks), points, 0

    def __call__(self):
        self.calls += 1
        due = min(self.total, -(-self.calls * self.total // self.points))
        while self.total - len(self.thunks) < due:
            self.thunks.pop(0)()

    def flush(self):
        while self.thunks:
            self.thunks.pop(0)()


def _fused_body(*refs, init, core, n_in, n_out, points):
    x_cur_ref, x_next_ref, nw_ref, sc_ref, sh_ref, w_ref = refs[:6]
    ins = refs[6:6 + n_in]
    outs = refs[6 + n_in:6 + n_in + n_out]
    scr = refs[6 + n_in + n_out:-2]
    bufs = refs[-2:]
    bb, c, d = x_cur_ref.shape
    rows = bb * c
    j = pl.program_id(1)

    def projection(x_ref, dst_ref):
        h = _norm_mod(x_ref[...], nw_ref[...], sc_ref[...], sh_ref[...])
        hb = h.reshape(rows, d).astype(BF16)

        def piece(n0, n1):
            def run():
                dst_ref[:, n0:n1] = jnp.dot(hb, w_ref[:, n0:n1], preferred_element_type=F32)
            return run
        return [piece(n0, min(n0 + MXU_TILE, w_ref.shape[1])) for n0 in range(0, w_ref.shape[1], MXU_TILE)]

    @pl.when(j == 0)
    def _():
        init(ins, outs, scr)
        for run in projection(x_cur_ref, bufs[0]):
            run()

    for parity in range(2):
        @pl.when(lax.rem(j, 2) == parity)
        def _(parity=parity):
            tick = _Ticker(projection(x_next_ref, bufs[1 - parity]), points)
            core(bufs[parity][...], ins, outs, scr, tick)
            tick.flush()


def _fused_call(name, init, core, points, x, nw, sc, sh, w_in, chunk, ins, in_specs, out_shapes, out_specs,
                scratch):
    b, t, d = x.shape
    bb, c = _mixer_tiling(b, t, chunk)
    n = t // c
    cols = w_in.shape[1]
    seq_spec = pl.BlockSpec((bb, 1, d), lambda i, j: (i, 0, 0))
    return pl.pallas_call(
        functools.partial(_fused_body, init=init, core=core, n_in=len(ins), n_out=len(out_shapes), points=points),
        out_shape=out_shapes,
        grid=(b // bb, n),
        in_specs=[pl.BlockSpec((bb, c, d), lambda i, j: (i, j, 0)),
                  pl.BlockSpec((bb, c, d), lambda i, j: (i, jnp.minimum(j + 1, n - 1), 0)),
                  _const_spec((1, d)), seq_spec, seq_spec, _const_spec(w_in.shape, single_buffer=True)]
                 + in_specs(bb, c),
        out_specs=out_specs(bb, c),
        scratch_shapes=scratch(bb, c) + [pltpu.VMEM((bb * c, cols), F32)] * 2,
        compiler_params=_params(2),
        name=name,
    )(x, x, nw, sc, sh, w_in, *ins)


def _post_body(*refs, n_mix, final):
    x_ref = refs[0]
    mix_refs = refs[1:1 + n_mix]
    (wout_ref, gm_ref, nw_ref, sc_ref, sh_ref, gf_ref, wg_ref, wu_ref, wd_ref) = refs[1 + n_mix:10 + n_mix]
    rest = refs[10 + n_mix:]
    if final:
        fw_ref, o_ref, act_ref = rest
    else:
        o_ref, act_ref = rest
    bb, tt, d = x_ref.shape
    rows = bb * tt

    mix = None
    off = 0
    for m_ref in mix_refs:
        wdt = m_ref.shape[-1]
        part = jnp.dot(m_ref[...].reshape(rows, wdt).astype(BF16), wout_ref[off:off + wdt, :],
                       preferred_element_type=F32)
        mix = part if mix is None else mix + part
        off += wdt
    x1 = x_ref[...] + gm_ref[...] * mix.reshape(bb, tt, d)

    h = _norm_mod(x1, nw_ref[...], sc_ref[...], sh_ref[...])
    hb = h.reshape(rows, d).astype(BF16)
    for f0, f1 in _col_chunks(wg_ref.shape[1]):
        gate = jnp.dot(hb, wg_ref[:, f0:f1], preferred_element_type=F32)
        up = jnp.dot(hb, wu_ref[:, f0:f1], preferred_element_type=F32)
        act_ref[:, f0:f1] = (_silu(gate) * up).astype(BF16)
    ffn = jnp.dot(act_ref[...], wd_ref[...], preferred_element_type=F32)
    x2 = x1 + gf_ref[...] * ffn.reshape(bb, tt, d)
    if final:
        ms = jnp.mean(x2 * x2, axis=-1, keepdims=True)
        x2 = x2 * lax.rsqrt(ms + RMS_EPS) * fw_ref[...]
    o_ref[...] = x2


def _post_call(x, mixes, wout, gm, nw, sc, sh, gf, wg, wu, wd, final_w):
    b, t, d = x.shape
    bb, tt = _row_tiling(b, t)
    final = final_w is not None
    row_spec = lambda w: pl.BlockSpec((bb, tt, w), lambda i, j: (i, j, 0))
    seq_spec = pl.BlockSpec((bb, 1, d), lambda i, j: (i, 0, 0))
    in_specs = ([row_spec(d)] + [row_spec(m.shape[-1]) for m in mixes]
                + [_const_spec(wout.shape, True), seq_spec, _const_spec((1, d)), seq_spec, seq_spec, seq_spec,
                   _const_spec(wg.shape, True), _const_spec(wu.shape, True), _const_spec(wd.shape, True)])
    args = [x, *mixes, wout, gm, nw, sc, sh, gf, wg, wu, wd]
    if final:
        in_specs.append(_const_spec((1, d)))
        args.append(final_w)
    return pl.pallas_call(
        functools.partial(_post_body, n_mix=len(mixes), final=final),
        out_shape=jax.ShapeDtypeStruct((b, t, d), F32),
        grid=(b // bb, t // tt),
        in_specs=in_specs,
        out_specs=row_spec(d),
        scratch_shapes=[pltpu.VMEM((bb * tt, wg.shape[1]), BF16)],
        compiler_params=_params(2),
        name="post_ffn",
    )(*args)


def _tri(c, strict=False, reps=1):
    row = lax.broadcasted_iota(jnp.int32, (c, reps * c), 0)
    col = lax.broadcasted_iota(jnp.int32, (c, reps * c), 1) & (c - 1)
    return (row > col) if strict else (row >= col)


def _block_tri(rows, c, upper=False):
    r = lax.broadcasted_iota(jnp.int32, (rows, rows), 0)
    q = lax.broadcasted_iota(jnp.int32, (rows, rows), 1)
    tri = (r <= q) if upper else (r >= q)
    return (tri & ((r ^ q) < c)).astype(BF16)


def _split3(x):
    x1 = x.astype(BF16)
    r1 = x - x1.astype(F32)
    x2 = r1.astype(BF16)
    x3 = (r1 - x2.astype(F32)).astype(BF16)
    return x1, x2, x3


def _split_dot(w01, x):
    x1, x2, x3 = _split3(x)
    dot = lambda piece: jnp.dot(w01, piece, preferred_element_type=F32)
    return dot(x1) + dot(x2) + dot(x3)


def _split_dot_tn(x, w01):
    x1, x2, x3 = _split3(x)
    dot = lambda piece: lax.dot_general(piece, w01, (((0,), (0,)), ((), ())), preferred_element_type=F32)
    return dot(x1) + dot(x2) + dot(x3)


def _expand_cols(x, width):
    k = x.shape[1]
    r = lax.broadcasted_iota(jnp.int32, (k, k * width), 0) * width
    q = lax.broadcasted_iota(jnp.int32, (k, k * width), 1)
    sel = ((q >= r) & (q < r + width)).astype(BF16)
    x1, x2, x3 = _split3(x)
    dot = lambda piece: jnp.dot(piece, sel, preferred_element_type=F32)
    return dot(x1) + dot(x2) + dot(x3)


def _group_sums(x, width):
    w = x.shape[1]
    tile = MXU_TILE if (w % MXU_TILE == 0 and MXU_TILE % width == 0) else w
    r = lax.broadcasted_iota(jnp.int32, (tile, tile), 0)
    q = lax.broadcasted_iota(jnp.int32, (tile, tile), 1)
    ones = ((r ^ q) < width).astype(BF16)
    pieces = _split3(x)
    cols = []
    for t0 in range(0, w, tile):
        acc = None
        for piece in pieces:
            part = jnp.dot(piece[:, t0:t0 + tile], ones, preferred_element_type=F32)
            acc = part if acc is None else acc + part
        cols.append(acc)
    return cols[0] if len(cols) == 1 else jnp.concatenate(cols, axis=1)


def _last_rows(x, bb, c):
    lasts = [x[(b + 1) * c - 1:(b + 1) * c, :] for b in range(bb)]
    tiled = [jnp.broadcast_to(l, (c, x.shape[1])) for l in lasts]
    return lasts, (tiled[0] if bb == 1 else jnp.concatenate(tiled, axis=0))


def _mixer_tiling(b, t, chunk):
    bb = MIX_SEQS if b % MIX_SEQS == 0 else 1
    return bb, math.gcd(chunk, t)


def _rwkv_init(ins, outs, scr):
    shift_ref, s0_ref = ins[:2]
    prev_ref, sbd_ref = scr
    bb, heads, hd = s0_ref.shape[:3]
    prev_ref[...] = shift_ref[...]
    zero = jnp.zeros((hd, hd), F32)
    for b in range(bb):
        for j in range(heads // 2):
            sbd_ref[b, j] = jnp.concatenate(
                [jnp.concatenate([s0_ref[b, 2 * j], zero], axis=1),
                 jnp.concatenate([zero, s0_ref[b, 2 * j + 1]], axis=1)], axis=0)


def _rwkv_core(p, ins, outs, scr, tick, *, heads, lora):
    (shift_ref, s0_ref, mu_ref, w0_ref, w2_ref, a0_ref, a2_ref, g2_ref,
     kk_ref, ka_ref, rk_ref, lw_ref, lb_ref) = ins
    o_ref, shift_out_ref, s_ref = outs
    prev_ref, sbd_ref = scr
    bb, c, wa = o_ref.shape
    rows, cols = p.shape
    hd = s_ref.shape[-1]
    lw_, la_, lg_ = lora

    row = lax.broadcasted_iota(jnp.int32, (rows, cols), 0)
    p_prev = pltpu.roll(p, 1, axis=0)
    for b in range(bb):
        p_prev = jnp.where(row == b * c, prev_ref[b], p_prev)
        last = p[(b + 1) * c - 1:(b + 1) * c, :]
        prev_ref[b] = last
        shift_out_ref[b] = last
    pm = p + (p_prev - p) * mu_ref[...]

    r = pm[:, 0:wa]
    k = pm[:, wa:2 * wa]
    v = pm[:, 2 * wa:3 * wa]
    o1 = 3 * wa
    xw = pm[:, o1:o1 + lw_]
    xa = pm[:, o1 + lw_:o1 + lw_ + la_]
    xg = pm[:, o1 + lw_ + la_:o1 + lw_ + la_ + lg_]

    w = -jax.nn.softplus(-(w0_ref[...] + _bdot(jnp.tanh(xw), w2_ref[...]))) - 0.5
    logd = -jnp.exp(w)
    tick()
    a = jax.nn.sigmoid(a0_ref[...] + _bdot(xa, a2_ref[...]))
    g = _bdot(jax.nn.sigmoid(xg), g2_ref[...])
    kk_raw = k * kk_ref[...]
    kk = kk_raw / jnp.maximum(jnp.sqrt(_group_sums(kk_raw * kk_raw, hd)), 1e-12)
    kka = kk * a
    k2 = k * (1.0 + (a - 1.0) * ka_ref[...])

    cum = _split_dot(_block_tri(rows, c), logd)
    tick()
    tots, tot_rows = _last_rows(cum, bb, c)
    p_inv = jnp.exp(-cum)
    p_end = jnp.exp(tot_rows - cum)
    p_tots = [jnp.exp(t) for t in tots]
    at = (-kk * jnp.exp(cum - logd)).astype(BF16)
    rt = (r * jnp.exp(cum)).astype(BF16)
    bt = (kka * p_inv).astype(BF16)
    kt = (k2 * p_inv).astype(BF16)
    be = (kka * p_end).astype(BF16)
    ke = (k2 * p_end).astype(BF16)
    vb = v.astype(BF16)

    pw = 2 * hd
    iota = lambda shape, dim: lax.broadcasted_iota(jnp.int32, shape, dim)
    first_c = iota((c, 2 * c), 1) < c
    first_v = iota((1, pw), 1) < hd
    bd_cc = (iota((2 * c, 2 * c), 0) < c) == (iota((2 * c, 2 * c), 1) < c)
    bd_cv = (iota((2 * c, pw), 0) < c) == (iota((2 * c, pw), 1) < hd)
    bd_vv = (iota((pw, pw), 0) < hd) == (iota((pw, pw), 1) < hd)
    first_cv4 = (iota((4 * c, pw), 0) < 2 * c) == (iota((4 * c, pw), 1) < hd)
    strict2 = _tri(c, strict=True, reps=2)
    incl4 = _tri(c, reps=4)
    eye2 = ((iota((c, 2 * c), 1) & (c - 1)) == iota((c, 2 * c), 0)).astype(F32)
    stack2 = lambda m: jnp.concatenate([m, m], axis=0)

    units = [(b, j) for b in range(bb) for j in range(heads // 2)]
    idx = [(slice(b * c, (b + 1) * c), slice(j * pw, (j + 1) * pw)) for b, j in units]
    xps = [jnp.concatenate([at[rs, sl], rt[rs, sl]], axis=0) for rs, sl in idx]
    ybds = [jnp.concatenate([jnp.where(first_v, jnp.concatenate([bt[rs, sl], kt[rs, sl]], axis=0), 0),
                             jnp.where(first_v, 0, jnp.concatenate([kt[rs, sl], bt[rs, sl]], axis=0))], axis=0)
            for rs, sl in idx]
    eps = [jnp.concatenate([be[rs, sl], ke[rs, sl]], axis=0) for rs, sl in idx]
    vps = [vb[rs, sl] for rs, sl in idx]
    sbds = [sbd_ref[b, j] for b, j in units]
    tick()
    gps = [_bdot_nt(xp, ybd) for xp, ybd in zip(xps, ybds)]
    tick()
    xsps = [_bdot_nt(xp, sbd) for xp, sbd in zip(xps, sbds)]
    tick()
    a_ps = [jnp.where(strict2, jnp.where(first_c, gp[:c, 0:2 * c], gp[:c, 2 * c:4 * c]), 0.0) for gp in gps]
    k_ps = [jnp.where(strict2, jnp.where(first_c, gp[:c, 2 * c:4 * c], gp[:c, 0:2 * c]), 0.0) for gp in gps]
    w_ps = [xsp[:c] + _bdot(k_p, jnp.where(bd_cv, 0, stack2(vp))) for xsp, k_p, vp in zip(xsps, k_ps, vps)]
    tick()
    ns = a_ps
    ts = [eye2 + n for n in ns]
    step = 2
    while step < c:
        nbds = [jnp.where(bd_cc, stack2(n.astype(BF16)), 0) for n in ns]
        ns = [_bdot(n, nbd) for n, nbd in zip(ns, nbds)]
        tick()
        ts = [t + _bdot(t, jnp.where(bd_cc, stack2(n.astype(BF16)), 0)) for t, n in zip(ts, ns)]
        tick()
        step *= 2
    u_ps = [_bdot(t, jnp.where(bd_cv, stack2(w_p.astype(BF16)), 0)) for t, w_p in zip(ts, w_ps)]
    tick()
    uvs = [(u_p.astype(BF16), vp) for u_p, vp in zip(u_ps, vps)]
    res = [xsp[c:] + _bdot(jnp.where(incl4, gp[c:, :], 0.0),
                           jnp.where(first_cv4, jnp.concatenate([u, vp, vp, u], axis=0), 0))
           for xsp, gp, (u, vp) in zip(xsps, gps, uvs)]
    tick()
    news = [sbd * p_tots[b][:, sl] + jnp.where(bd_vv, _bdot_tn(jnp.concatenate([u, vp], axis=0), ep), 0.0)
            for sbd, (b, _), (_, sl), (u, vp), ep in zip(sbds, units, idx, uvs, eps)]
    tick()
    for i, ((b, j), (rs, sl)) in enumerate(zip(units, idx)):
        sbd_ref[b, j] = news[i]
        o_ref[b, :, sl] = res[i]

    @pl.when(pl.program_id(1) == pl.num_programs(1) - 1)
    def _():
        for b, j in units:
            s_ref[b, 2 * j] = sbd_ref[b, j, 0:hd, 0:hd]
            s_ref[b, 2 * j + 1] = sbd_ref[b, j, hd:pw, hd:pw]

    o = o_ref[...].reshape(rows, wa)
    dev = o - _group_sums(o, hd) * (1.0 / hd)
    var = _group_sums(dev * dev, hd) * (1.0 / hd)
    o = dev * lax.rsqrt(var + LNX_EPS) * lw_ref[...] + lb_ref[...]
    tick()
    bonus = _group_sums(r * k2 * rk_ref[...], hd) * v
    o_ref[...] = ((o + bonus) * g).reshape(bb, c, wa)


RWKV_POINTS = 20


def _rwkv_call(x, nw, sc, sh, w_in, shift_prev, s0, prm):
    b, t, _ = x.shape
    cols = w_in.shape[1]
    heads, hd = s0.shape[1], s0.shape[2]
    wa = heads * hd
    lora = (prm["w2"].shape[0], prm["a2"].shape[0], prm["g2"].shape[0])
    vec = lambda a: a.reshape(1, -1)
    consts = [vec(prm["mu"]), vec(prm["w0"]), prm["w2"], vec(prm["a0"]), prm["a2"], prm["g2"],
              vec(prm["k_k"]), vec(prm["k_a"]), vec(prm["r_k"]), vec(prm["lnx_w"]), vec(prm["lnx_b"])]
    seq = lambda bb, shape: pl.BlockSpec((bb,) + shape, lambda i, j: (i,) + (0,) * len(shape))
    return _fused_call(
        "rwkv7_mixer", _rwkv_init, functools.partial(_rwkv_core, heads=heads, lora=lora), RWKV_POINTS,
        x, nw, sc, sh, w_in, RWKV_CHUNK,
        ins=[shift_prev.reshape(b, 1, cols), s0, *consts],
        in_specs=lambda bb, c: [seq(bb, (1, cols)), seq(bb, s0.shape[1:])] + [_const_spec(a.shape) for a in consts],
        out_shapes=[jax.ShapeDtypeStruct((b, t, wa), F32),
                    jax.ShapeDtypeStruct((b, 1, cols), F32),
                    jax.ShapeDtypeStruct(s0.shape, F32)],
        out_specs=lambda bb, c: [pl.BlockSpec((bb, c, wa), lambda i, j: (i, j, 0)),
                                 seq(bb, (1, cols)), seq(bb, s0.shape[1:])],
        scratch=lambda bb, c: [pltpu.VMEM((bb, 1, cols), F32),
                               pltpu.VMEM((bb, heads // 2, 2 * hd, 2 * hd), F32)])


def _mamba_init(ins, outs, scr, *, conv_w):
    conv_ref, s0_ref = ins[:2]
    s_ref = outs[2]
    ubuf_ref = scr[0]
    bb, _, xbc = conv_ref.shape
    pad, hist = 8, conv_w - 1
    s_ref[...] = s0_ref[...]
    for b in range(bb):
        ubuf_ref[b, 0:pad, :] = jnp.zeros((pad, xbc), F32)
        ubuf_ref[b, pad - hist:pad, :] = conv_ref[b]


def _mamba_core(p, ins, outs, scr, tick, *, groups, conv_w):
    conv_ref, s0_ref, cw_ref, cb_ref, dtb_ref, alog_ref, dskip_ref, nw_ref = ins
    o_ref, conv_out_ref, s_ref = outs
    ubuf_ref, inter_ref = scr
    bb, c, wb = o_ref.shape
    rows = bb * c
    heads, hd, ns = s_ref.shape[1], s_ref.shape[2], s_ref.shape[3]
    xbc = conv_ref.shape[-1]
    hpg = heads // groups
    pad = 8
    hist = conv_w - 1

    ys = []
    for b in range(bb):
        u = p[b * c:(b + 1) * c, wb:wb + xbc]
        ubuf_ref[b, pad:pad + c, :] = u
        y = cb_ref[...] + cw_ref[hist:hist + 1, :] * u
        for i in range(hist):
            y = y + cw_ref[i:i + 1, :] * ubuf_ref[b, pad - hist + i:pad - hist + i + c, :]
        conv_out_ref[b] = ubuf_ref[b, pad + c - hist:pad + c, :]
        ubuf_ref[b, 0:pad, :] = ubuf_ref[b, c:c + pad, :]
        ys.append(y)
        tick()
    xc = _silu(ys[0] if bb == 1 else jnp.concatenate(ys, axis=0))
    xs = xc[:, 0:wb]
    bm = xc[:, wb:wb + groups * ns]
    cm = xc[:, wb + groups * ns:wb + 2 * groups * ns]
    z = p[:, 0:wb]
    dt_raw = p[:, wb + xbc:wb + xbc + heads]
    dt = jax.nn.softplus(dt_raw + dtb_ref[...])
    la = dt * (-jnp.exp(alog_ref[...]))

    incl = _tri(c)
    acum = _split_dot(_block_tri(rows, c), la)
    acum_t = _split_dot_tn(la, _block_tri(rows, c, upper=True))
    a_lasts, a_last_rows = _last_rows(acum, bb, c)
    e_tots = [jnp.exp(al) for al in a_lasts]
    stack = jnp.concatenate([dt, jnp.exp(acum), jnp.exp(a_last_rows - acum),
                             jnp.broadcast_to(dskip_ref[...], (8, heads))], axis=0)
    full = _expand_cols(stack, hd)
    ecum_f = full[rows:2 * rows]
    dskip_f = full[3 * rows:3 * rows + 1]
    xd_f = xs * full[0:rows]
    xde_f = xd_f * full[2 * rows:3 * rows]
    colb = _expand_cols(acum, c)
    tick()

    rss = [slice(b * c, (b + 1) * c) for b in range(bb)]
    bgs = [[bm[rs, gi * ns:(gi + 1) * ns].astype(BF16) for gi in range(groups)] for rs in rss]
    cgs = [[cm[rs, gi * ns:(gi + 1) * ns].astype(BF16) for gi in range(groups)] for rs in rss]
    cbs = [[_bdot_nt(cg, bg) for cg, bg in zip(cgb, bgb)] for cgb, bgb in zip(cgs, bgs)]
    tick()
    pairs = [(b, h) for b in range(bb) for h in range(heads)]
    sls = [slice(h * hd, (h + 1) * hd) for _, h in pairs]
    s0s = [s_ref[b, h] for b, h in pairs]
    mats = []
    for b, h in pairs:
        seg = colb[rss[b], h * c:(h + 1) * c] - acum_t[h:h + 1, rss[b]]
        decay = jnp.where(incl, jnp.exp(jnp.where(incl, seg, 0.0)), 0.0)
        mats.append((cbs[b][h // hpg] * decay).astype(BF16))
        if h % 4 == 3:
            tick()
    intra = [_bdot(m, xd_f[rss[b], sl]) for m, (b, _), sl in zip(mats, pairs, sls)]
    tick()
    inter = [_bdot_nt(cgs[b][h // hpg], s0) for (b, h), s0 in zip(pairs, s0s)]
    tick()
    upd = [_bdot_tn(xde_f[rss[b], sl], bgs[b][h // hpg]) for (b, h), sl in zip(pairs, sls)]
    tick()
    for i, ((b, h), sl) in enumerate(zip(pairs, sls)):
        s_ref[b, h] = s0s[i] * e_tots[b][:, h:h + 1] + upd[i]
        o_ref[b, :, sl] = intra[i]
        inter_ref[b, :, sl] = inter[i]

    y_all = o_ref[...].reshape(rows, wb) + inter_ref[...].reshape(rows, wb) * ecum_f + dskip_f * xs
    yv = y_all * _silu(z)
    gw = wb // groups
    for gi in range(groups):
        sl = slice(gi * gw, (gi + 1) * gw)
        yg = yv[:, sl]
        ms = jnp.mean(yg * yg, axis=-1, keepdims=True)
        o_ref[:, :, sl] = (yg * lax.rsqrt(ms + GROUP_RMS_EPS) * nw_ref[:, sl]).reshape(bb, c, gw)


MAMBA_POINTS = 16


def _mamba_call(x, nw, sc, sh, w_in, conv_prev, s0, prm, groups):
    b, t, _ = x.shape
    heads, hd, ns = s0.shape[1:]
    wb = heads * hd
    hist, xbc = conv_prev.shape[1:]
    vec = lambda a: a.reshape(1, -1)
    consts = [prm["conv_w"], vec(prm["conv_b"]), vec(prm["dt_bias"]), vec(prm["a_log"]),
              vec(prm["d_skip"]), vec(prm["norm_b_w"])]
    seq = lambda bb, shape: pl.BlockSpec((bb,) + shape, lambda i, j: (i,) + (0,) * len(shape))
    return _fused_call(
        "mamba2_mixer", functools.partial(_mamba_init, conv_w=hist + 1),
        functools.partial(_mamba_core, groups=groups, conv_w=hist + 1), MAMBA_POINTS,
        x, nw, sc, sh, w_in, MAMBA_CHUNK,
        ins=[conv_prev, s0, *consts],
        in_specs=lambda bb, c: [seq(bb, (hist, xbc)), seq(bb, s0.shape[1:])] + [_const_spec(a.shape) for a in consts],
        out_shapes=[jax.ShapeDtypeStruct((b, t, wb), F32),
                    jax.ShapeDtypeStruct(conv_prev.shape, F32),
                    jax.ShapeDtypeStruct(s0.shape, F32)],
        out_specs=lambda bb, c: [pl.BlockSpec((bb, c, wb), lambda i, j: (i, j, 0)),
                                 seq(bb, (hist, xbc)), seq(bb, s0.shape[1:])],
        scratch=lambda bb, c: [pltpu.VMEM((bb, c + 8, xbc), F32), pltpu.VMEM((bb, c, wb), F32)])


def _hgrn_sum_matrix(c):
    t = np.arange(c)[:, None]
    j = np.arange(c)[None, :]
    blocks = [(j <= t), (j > t)]
    m = c // 2
    while m >= 1:
        mid = (t // (2 * m)) * (2 * m) + m - 1
        right = (t % (2 * m)) >= m
        blocks.append(np.where(right, (j > mid) & (j <= t), (j > t) & (j <= mid)))
        m //= 2
    return np.concatenate(blocks, axis=0).astype(np.float32)


def _hgrn_init(ins, outs, scr):
    s0_ref = ins[0]
    (st_ref,) = scr
    bb, heads = s0_ref.shape[:2]
    for b in range(bb):
        for h in range(heads):
            st_ref[b, h] = s0_ref[b, h].T


def _hgrn_core(p, ins, outs, scr, tick, *, layer):
    s0_ref, sums_ref, lbp_ref, nw_ref = ins
    o_ref, s_ref = outs
    (st_ref,) = scr
    bb, c, wc = o_ref.shape
    rows, cols = p.shape
    heads, dk, dv = s0_ref.shape[1:]
    nchunks = pl.num_programs(1)
    pairs = [(b, h) for b in range(bb) for h in range(heads)]

    lbp = lbp_ref[...]
    e = jnp.exp(lbp - jnp.max(lbp, axis=0, keepdims=True))
    soft = e / jnp.sum(e, axis=0, keepdims=True)
    lb = soft[0:1, :]
    for i in range(1, layer + 1):
        lb = lb + soft[i:i + 1, :]
    lb = lb - soft[0:1, :]

    q = _silu(p[:, 0:wc])
    f = p[:, wc:2 * wc]
    v = p[:, 2 * wc:3 * wc]
    gate = p[:, 3 * wc:4 * wc]
    log_f = jnp.log(lb + (1.0 - lb) * jax.nn.sigmoid(f))
    k = (1.0 - lb) * jax.nn.sigmoid(-f)
    qk = q * k

    row = lax.broadcasted_iota(jnp.int32, (c, 1), 0)
    rr = lax.broadcasted_iota(jnp.int32, (c, c), 0)
    cc = lax.broadcasted_iota(jnp.int32, (c, c), 1)
    eye = rr == cc
    rss = [slice(b * c, (b + 1) * c) for b in range(bb)]
    qes, kes, f_tots, levels = [], [], [], []
    for rs in rss:
        sums = _split_dot(sums_ref[...], log_f[rs])
        bcum = sums[0:c]
        tick()
        qes.append((q[rs] * jnp.exp(bcum)).astype(BF16))
        kes.append((k[rs] * jnp.exp(sums[c:2 * c])).astype(BF16))
        f_tots.append(jnp.exp(bcum[c - 1:c, :]))
        lv = []
        m = c // 2
        i = 2
        while m >= 1:
            wgt = jnp.exp(sums[i * c:(i + 1) * c])
            right = (row & (2 * m - 1)) >= m
            lv.append((jnp.where(right, q[rs] * wgt, 0.0).astype(BF16),
                       jnp.where(right, 0.0, k[rs] * wgt).astype(BF16),
                       (rr ^ cc) < 2 * m))
            m //= 2
            i += 1
            tick()
        levels.append(lv)

    sls = [slice(h * dk, (h + 1) * dk) for _, h in pairs]
    atts = [jnp.where(eye, jnp.sum(qk[rss[b], sl], axis=-1, keepdims=True), 0.0) for (b, _), sl in zip(pairs, sls)]
    for li in range(len(levels[0])):
        atts = [att + jnp.where(levels[b][li][2], _bdot_nt(levels[b][li][0][:, sl], levels[b][li][1][:, sl]), 0.0)
                for att, (b, _), sl in zip(atts, pairs, sls)]
        tick()
    vhs = [v[rss[b], sl].astype(BF16) for (b, _), sl in zip(pairs, sls)]
    sts = [st_ref[b, h] for b, h in pairs]
    intra = [_bdot(att, vh) for att, vh in zip(atts, vhs)]
    tick()
    inter = [_bdot_nt(qes[b][:, sl], st) for (b, _), sl, st in zip(pairs, sls, sts)]
    tick()
    upd = [_bdot_tn(vh, kes[b][:, sl]) for vh, (b, _), sl in zip(vhs, pairs, sls)]
    tick()
    for i, ((b, h), sl) in enumerate(zip(pairs, sls)):
        st_ref[b, h] = sts[i] * f_tots[b][:, sl] + upd[i]
        o = intra[i] + inter[i]
        ms = jnp.mean(o * o, axis=-1, keepdims=True)
        o_ref[b, :, sl] = o * lax.rsqrt(ms + GROUP_RMS_EPS) * nw_ref[:, sl] * _silu(gate[rss[b], sl])

    @pl.when(pl.program_id(1) == nchunks - 1)
    def _():
        for b, h in pairs:
            s_ref[b, h] = st_ref[b, h].T


HGRN_POINTS = 23


def _hgrn_call(x, nw, sc, sh, w_in, s0, lb_param, norm_w, layer):
    b, t, _ = x.shape
    heads, dk, dv = s0.shape[1:]
    assert dk == dv
    wc = heads * dk
    sums = jnp.asarray(_hgrn_sum_matrix(math.gcd(HGRN_CHUNK, t)), dtype=BF16)
    ins = [s0, sums, lb_param, norm_w.reshape(1, wc)]
    seq = lambda bb, shape: pl.BlockSpec((bb,) + shape, lambda i, j: (i,) + (0,) * len(shape))
    return _fused_call(
        "hgrn2_mixer", _hgrn_init, functools.partial(_hgrn_core, layer=layer), HGRN_POINTS,
        x, nw, sc, sh, w_in, HGRN_CHUNK,
        ins=ins,
        in_specs=lambda bb, c: [seq(bb, s0.shape[1:])] + [_const_spec(a.shape) for a in ins[1:]],
        out_shapes=[jax.ShapeDtypeStruct((b, t, wc), F32), jax.ShapeDtypeStruct(s0.shape, F32)],
        out_specs=lambda bb, c: [pl.BlockSpec((bb, c, wc), lambda i, j: (i, j, 0)), seq(bb, s0.shape[1:])],
        scratch=lambda bb, c: [pltpu.VMEM((bb, heads, dv, dk), F32)])


def _round_up(n, m):
    return -(-n // m) * m


def _prepare_weights(w):
    a_cols = w["mu_a"].shape[1]
    b_cols = w["w_in_ab"].shape[2] - a_cols
    b_pad = _round_up(b_cols, LANE) - b_cols
    return dict(
        w_in_a=w["w_in_ab"][:, :, :a_cols].astype(BF16),
        w_in_b=jnp.pad(w["w_in_ab"][:, :, a_cols:], ((0, 0), (0, 0), (0, b_pad))).astype(BF16),
        w_out_ab=w["w_out_ab"].astype(BF16),
        w_in_c=w["w_in_c"].astype(BF16),
        w_out_c=w["w_out_c"].astype(BF16),
        w_gate=w["w_gate"].astype(BF16),
        w_up=w["w_up"].astype(BF16),
        w_down=w["w_down"].astype(BF16),
    )


def _trunk(x, mod, st_rwkv, st_shift, st_ssm, st_conv, st_hgrn, w, wb):
    depth = mod.shape[0]
    b, t, d = x.shape
    groups = (st_conv.shape[-1] - st_ssm.shape[2] * st_ssm.shape[3]) // (2 * st_ssm.shape[4])
    new_rwkv, new_shift, new_ssm, new_conv, new_hgrn = [], [], [], [], []
    for layer in range(depth):
        j = layer // 2
        sh_m, sc_m, g_m, sh_f, sc_f, g_f = (mod[layer, :, None, i * d:(i + 1) * d] for i in range(6))
        nw_mix = w["norm_mix_w"][layer].reshape(1, d)
        nw_ffn = w["norm_ffn_w"][layer].reshape(1, d)
        if layer % 2 == 0:
            prm = dict(mu=w["mu_a"][j], w0=w["w0"][j], w2=w["w2"][j], a0=w["a0"][j], a2=w["a2"][j],
                       g2=w["g2"][j], k_k=w["k_k"][j], k_a=w["k_a"][j], r_k=w["r_k"][j],
                       lnx_w=w["lnx_w"][j], lnx_b=w["lnx_b"][j])
            oa, shift_new, rwkv_new = _rwkv_call(x, nw_mix, sc_m, sh_m, wb["w_in_a"][j],
                                                 st_shift[:, j], st_rwkv[:, j], prm)
            prm_b = dict(conv_w=w["conv_w"][j], conv_b=w["conv_b"][j], dt_bias=w["dt_bias"][j],
                         a_log=w["a_log"][j], d_skip=w["d_skip"][j], norm_b_w=w["norm_b_w"][j])
            ob, conv_new, ssm_new = _mamba_call(x, nw_mix, sc_m, sh_m, wb["w_in_b"][j],
                                                st_conv[:, j], st_ssm[:, j], prm_b, groups)
            mixes, wout = [oa, ob], wb["w_out_ab"][j]
            new_rwkv.append(rwkv_new)
            new_shift.append(shift_new[:, 0])
            new_ssm.append(ssm_new)
            new_conv.append(conv_new)
        else:
            oc, hgrn_new = _hgrn_call(x, nw_mix, sc_m, sh_m, wb["w_in_c"][j],
                                      st_hgrn[:, j], w["lb_param"], w["norm_c_w"][j], j)
            mixes, wout = [oc], wb["w_out_c"][j]
            new_hgrn.append(hgrn_new)
        final_w = w["norm_out_w"].reshape(1, d) if layer == depth - 1 else None
        x = _post_call(x, mixes, wout, g_m, nw_ffn, sc_f, sh_f, g_f,
                       wb["w_gate"][layer], wb["w_up"][layer], wb["w_down"][layer], final_w)
    return (x, jnp.stack(new_rwkv, axis=1), jnp.stack(new_shift, axis=1), jnp.stack(new_ssm, axis=1),
            jnp.stack(new_conv, axis=1), jnp.stack(new_hgrn, axis=1))


def _run(x_prompt, x_sample, state_rwkv, state_rwkv_shift, state_ssm, state_conv, state_hgrn,
         c_prompt, c_sample, w):
    bp, bs = x_prompt.shape[0], x_sample.shape[0]
    rows = _round_up(bp + bs, 8)
    c_all = jnp.pad(jnp.concatenate([c_prompt, c_sample], axis=0), ((0, rows - bp - bs), (0, 0)))
    mod = _ada_call(c_all, w["ada_w"], w["ada_b"])
    wb = _prepare_weights(w)
    zeros = lambda s: jnp.zeros((bp,) + s.shape[1:], F32)
    outs_p = _trunk(x_prompt, mod[:, :bp], zeros(state_rwkv), zeros(state_rwkv_shift), zeros(state_ssm),
                    zeros(state_conv), zeros(state_hgrn), w, wb)
    outs_s = _trunk(x_sample, mod[:, bp:bp + bs], state_rwkv, state_rwkv_shift, state_ssm,
                    state_conv, state_hgrn, w, wb)
    return (outs_p[0], outs_s[0]) + outs_p[1:] + outs_s[1:]


def kernel(x_prompt, x_sample, state_rwkv, state_rwkv_shift, state_ssm, state_conv, state_hgrn, c_prompt, c_sample, norm_mix_w, norm_ffn_w, norm_out_w, ada_w, ada_b, w_in_ab, w_out_ab, mu_a, w0, w2, a0, a2, g2, k_k, k_a, r_k, lnx_w, lnx_b, conv_w, conv_b, dt_bias, a_log, d_skip, norm_b_w, w_in_c, w_out_c, lb_param, norm_c_w, w_gate, w_up, w_down):
    w = dict(norm_mix_w=norm_mix_w, norm_ffn_w=norm_ffn_w, norm_out_w=norm_out_w, ada_w=ada_w, ada_b=ada_b,
             w_in_ab=w_in_ab, w_out_ab=w_out_ab, mu_a=mu_a, w0=w0, w2=w2, a0=a0, a2=a2, g2=g2, k_k=k_k,
             k_a=k_a, r_k=r_k, lnx_w=lnx_w, lnx_b=lnx_b, conv_w=conv_w, conv_b=conv_b, dt_bias=dt_bias,
             a_log=a_log, d_skip=d_skip, norm_b_w=norm_b_w, w_in_c=w_in_c, w_out_c=w_out_c,
             lb_param=lb_param, norm_c_w=norm_c_w, w_gate=w_gate, w_up=w_up, w_down=w_down)
    return _run(x_prompt, x_sample, state_rwkv, state_rwkv_shift, state_ssm, state_conv, state_hgrn,
                c_prompt, c_sample, w)
```

```python
import functools
import math

import numpy as np
import jax
import jax.numpy as jnp
from jax import lax
from jax.experimental import pallas as pl
from jax.experimental.pallas import tpu as pltpu

F32 = jnp.float32
BF16 = jnp.bfloat16

LANE = 128
MXU_TILE = 256
VMEM_LIMIT = 56 * 2**20
ROW_TILE = 512
COL_CHUNK = 512
RWKV_CHUNK = 64
RWKV_GROUP = 4
MAMBA_CHUNK = 128
HGRN_CHUNK = 64
MIX_SEQS = 2
LNX_EPS = 64e-5
RMS_EPS = 1e-6
GROUP_RMS_EPS = 1e-5


def _bdot(a, b):
    return jnp.dot(a.astype(BF16), b.astype(BF16), preferred_element_type=F32)


def _bdot_nt(a, b):
    return lax.dot_general(a.astype(BF16), b.astype(BF16), (((1,), (1,)), ((), ())),
                           preferred_element_type=F32)


def _bdot_tn(a, b):
    return lax.dot_general(a.astype(BF16), b.astype(BF16), (((0,), (0,)), ((), ())),
                           preferred_element_type=F32)


def _silu(x):
    return x * jax.nn.sigmoid(x)


def _const_spec(shape, single_buffer=False):
    nd = len(shape)
    if single_buffer:
        return pl.BlockSpec(shape, lambda *_: (0,) * nd, pipeline_mode=pl.Buffered(1))
    return pl.BlockSpec(shape, lambda *_: (0,) * nd)


def _params(n_axes):
    return pltpu.CompilerParams(dimension_semantics=("arbitrary",) * n_axes,
                                vmem_limit_bytes=VMEM_LIMIT)


def _row_tiling(b, t):
    if t >= ROW_TILE:
        assert t % ROW_TILE == 0
        return 1, ROW_TILE
    bb = max(1, min(b, ROW_TILE // t))
    while b % bb:
        bb -= 1
    return bb, t


def _col_chunks(n):
    return [(n0, min(n0 + COL_CHUNK, n)) for n0 in range(0, n, COL_CHUNK)]


def _ada_body(c_ref, w_ref, b_ref, o_ref):
    o_ref[0] = _bdot(_silu(c_ref[...]), w_ref[0]) + b_ref[0]


def _ada_call(c_all, ada_w, ada_b):
    depth, d, n = ada_w.shape
    r = c_all.shape[0]
    tn = 1024
    assert n % tn == 0
    return pl.pallas_call(
        _ada_body,
        out_shape=jax.ShapeDtypeStruct((depth, r, n), F32),
        grid=(depth, n // tn),
        in_specs=[pl.BlockSpec((r, d), lambda l, j: (0, 0)),
                  pl.BlockSpec((1, d, tn), lambda l, j: (l, 0, j)),
                  pl.BlockSpec((1, 1, tn), lambda l, j: (l, 0, j))],
        out_specs=pl.BlockSpec((1, r, tn), lambda l, j: (l, 0, j)),
        compiler_params=_params(2),
        name="ada_mod",
    )(c_all, ada_w, ada_b.reshape(depth, 1, n))


def _norm_mod(x, nw, sc, sh):
    ms = jnp.mean(x * x, axis=-1, keepdims=True)
    y = x * lax.rsqrt(ms + RMS_EPS) * nw
    return y * (1.0 + sc) + sh


class _Ticker:
    def __init__(self, thunks, points):
        self.thunks, self.total, self.points, self.calls = list(thunks), len(thunks), points, 0

    def __call__(self):
        self.calls += 1
        due = min(self.total, -(-self.calls * self.total // self.points))
        while self.total - len(self.thunks) < due:
            self.thunks.pop(0)()

    def flush(self):
        while self.thunks:
            self.thunks.pop(0)()


def _fused_body(*refs, init, core, n_in, n_out, points):
    x_cur_ref, x_next_ref, nw_ref, sc_ref, sh_ref, w_ref = refs[:6]
    ins = refs[6:6 + n_in]
    outs = refs[6 + n_in:6 + n_in + n_out]
    scr = refs[6 + n_in + n_out:-2]
    bufs = refs[-2:]
    bb, c, d = x_cur_ref.shape
    rows = bb * c
    j = pl.program_id(1)

    def projection(x_ref, dst_ref):
        h = _norm_mod(x_ref[...], nw_ref[...], sc_ref[...], sh_ref[...])
        hb = h.reshape(rows, d).astype(BF16)

        def piece(n0, n1):
            def run():
                dst_ref[:, n0:n1] = jnp.dot(hb, w_ref[:, n0:n1], preferred_element_type=F32)
            return run
        return [piece(n0, min(n0 + MXU_TILE, w_ref.shape[1])) for n0 in range(0, w_ref.shape[1], MXU_TILE)]

    @pl.when(j == 0)
    def _():
        init(ins, outs, scr)
        for run in projection(x_cur_ref, bufs[0]):
            run()

    for parity in range(2):
        @pl.when(lax.rem(j, 2) == parity)
        def _(parity=parity):
            tick = _Ticker(projection(x_next_ref, bufs[1 - parity]), points)
            core(bufs[parity][...], ins, outs, scr, tick)
            tick.flush()


def _fused_call(name, init, core, points, x, nw, sc, sh, w_in, chunk, ins, in_specs, out_shapes, out_specs,
                scratch):
    b, t, d = x.shape
    bb, c = _mixer_tiling(b, t, chunk)
    n = t // c
    cols = w_in.shape[1]
    seq_spec = pl.BlockSpec((bb, 1, d), lambda i, j: (i, 0, 0))
    return pl.pallas_call(
        functools.partial(_fused_body, init=init, core=core, n_in=len(ins), n_out=len(out_shapes), points=points),
        out_shape=out_shapes,
        grid=(b // bb, n),
        in_specs=[pl.BlockSpec((bb, c, d), lambda i, j: (i, j, 0)),
                  pl.BlockSpec((bb, c, d), lambda i, j: (i, jnp.minimum(j + 1, n - 1), 0)),
                  _const_spec((1, d)), seq_spec, seq_spec, _const_spec(w_in.shape, single_buffer=True)]
                 + in_specs(bb, c),
        out_specs=out_specs(bb, c),
        scratch_shapes=scratch(bb, c) + [pltpu.VMEM((bb * c, cols), F32)] * 2,
        compiler_params=_params(2),
        name=name,
    )(x, x, nw, sc, sh, w_in, *ins)


def _post_body(*refs, n_mix, final):
    x_ref = refs[0]
    mix_refs = refs[1:1 + n_mix]
    (wout_ref, gm_ref, nw_ref, sc_ref, sh_ref, gf_ref, wg_ref, wu_ref, wd_ref) = refs[1 + n_mix:10 + n_mix]
    rest = refs[10 + n_mix:]
    if final:
        fw_ref, o_ref, act_ref = rest
    else:
        o_ref, act_ref = rest
    bb, tt, d = x_ref.shape
    rows = bb * tt

    mix = None
    off = 0
    for m_ref in mix_refs:
        wdt = m_ref.shape[-1]
        part = jnp.dot(m_ref[...].reshape(rows, wdt).astype(BF16), wout_ref[off:off + wdt, :],
                       preferred_element_type=F32)
        mix = part if mix is None else mix + part
        off += wdt
    x1 = x_ref[...] + gm_ref[...] * mix.reshape(bb, tt, d)

    h = _norm_mod(x1, nw_ref[...], sc_ref[...], sh_ref[...])
    hb = h.reshape(rows, d).astype(BF16)
    for f0, f1 in _col_chunks(wg_ref.shape[1]):
        gate = jnp.dot(hb, wg_ref[:, f0:f1], preferred_element_type=F32)
        up = jnp.dot(hb, wu_ref[:, f0:f1], preferred_element_type=F32)
        act_ref[:, f0:f1] = (_silu(gate) * up).astype(BF16)
    ffn = jnp.dot(act_ref[...], wd_ref[...], preferred_element_type=F32)
    x2 = x1 + gf_ref[...] * ffn.reshape(bb, tt, d)
    if final:
        ms = jnp.mean(x2 * x2, axis=-1, keepdims=True)
        x2 = x2 * lax.rsqrt(ms + RMS_EPS) * fw_ref[...]
    o_ref[...] = x2


def _post_call(x, mixes, wout, gm, nw, sc, sh, gf, wg, wu, wd, final_w):
    b, t, d = x.shape
    bb, tt = _row_tiling(b, t)
    final = final_w is not None
    row_spec = lambda w: pl.BlockSpec((bb, tt, w), lambda i, j: (i, j, 0))
    seq_spec = pl.BlockSpec((bb, 1, d), lambda i, j: (i, 0, 0))
    in_specs = ([row_spec(d)] + [row_spec(m.shape[-1]) for m in mixes]
                + [_const_spec(wout.shape, True), seq_spec, _const_spec((1, d)), seq_spec, seq_spec, seq_spec,
                   _const_spec(wg.shape, True), _const_spec(wu.shape, True), _const_spec(wd.shape, True)])
    args = [x, *mixes, wout, gm, nw, sc, sh, gf, wg, wu, wd]
    if final:
        in_specs.append(_const_spec((1, d)))
        args.append(final_w)
    return pl.pallas_call(
        functools.partial(_post_body, n_mix=len(mixes), final=final),
        out_shape=jax.ShapeDtypeStruct((b, t, d), F32),
        grid=(b // bb, t // tt),
        in_specs=in_specs,
        out_specs=row_spec(d),
        scratch_shapes=[pltpu.VMEM((bb * tt, wg.shape[1]), BF16)],
        compiler_params=_params(2),
        name="post_ffn",
    )(*args)


def _tri(c, strict=False, reps=1):
    row = lax.broadcasted_iota(jnp.int32, (c, reps * c), 0)
    col = lax.broadcasted_iota(jnp.int32, (c, reps * c), 1) & (c - 1)
    return (row > col) if strict else (row >= col)


def _block_tri(rows, c, upper=False):
    r = lax.broadcasted_iota(jnp.int32, (rows, rows), 0)
    q = lax.broadcasted_iota(jnp.int32, (rows, rows), 1)
    tri = (r <= q) if upper else (r >= q)
    return (tri & ((r ^ q) < c)).astype(BF16)


def _split3(x):
    x1 = x.astype(BF16)
    r1 = x - x1.astype(F32)
    x2 = r1.astype(BF16)
    x3 = (r1 - x2.astype(F32)).astype(BF16)
    return x1, x2, x3


def _split_dot(w01, x):
    x1, x2, x3 = _split3(x)
    dot = lambda piece: jnp.dot(w01, piece, preferred_element_type=F32)
    return dot(x1) + dot(x2) + dot(x3)


def _split_dot_tn(x, w01):
    x1, x2, x3 = _split3(x)
    dot = lambda piece: lax.dot_general(piece, w01, (((0,), (0,)), ((), ())), preferred_element_type=F32)
    return dot(x1) + dot(x2) + dot(x3)


def _expand_cols(x, width):
    k = x.shape[1]
    r = lax.broadcasted_iota(jnp.int32, (k, k * width), 0) * width
    q = lax.broadcasted_iota(jnp.int32, (k, k * width), 1)
    sel = ((q >= r) & (q < r + width)).astype(BF16)
    x1, x2, x3 = _split3(x)
    dot = lambda piece: jnp.dot(piece, sel, preferred_element_type=F32)
    return dot(x1) + dot(x2) + dot(x3)


def _group_sums(x, width):
    w = x.shape[1]
    tile = MXU_TILE if (w % MXU_TILE == 0 and MXU_TILE % width == 0) else w
    r = lax.broadcasted_iota(jnp.int32, (tile, tile), 0)
    q = lax.broadcasted_iota(jnp.int32, (tile, tile), 1)
    ones = ((r ^ q) < width).astype(BF16)
    pieces = _split3(x)
    cols = []
    for t0 in range(0, w, tile):
        acc = None
        for piece in pieces:
            part = jnp.dot(piece[:, t0:t0 + tile], ones, preferred_element_type=F32)
            acc = part if acc is None else acc + part
        cols.append(acc)
    return cols[0] if len(cols) == 1 else jnp.concatenate(cols, axis=1)


def _last_rows(x, bb, c):
    lasts = [x[(b + 1) * c - 1:(b + 1) * c, :] for b in range(bb)]
    tiled = [jnp.broadcast_to(l, (c, x.shape[1])) for l in lasts]
    return lasts, (tiled[0] if bb == 1 else jnp.concatenate(tiled, axis=0))


def _mixer_tiling(b, t, chunk):
    bb = MIX_SEQS if b % MIX_SEQS == 0 else 1
    return bb, math.gcd(chunk, t)


def _rwkv_init(ins, outs, scr):
    shift_ref, s0_ref = ins[:2]
    prev_ref, sbd_ref = scr
    bb, heads, hd = s0_ref.shape[:3]
    prev_ref[...] = shift_ref[...]
    zero = jnp.zeros((hd, hd), F32)
    for b in range(bb):
        for j in range(heads // 2):
            sbd_ref[b, j] = jnp.concatenate(
                [jnp.concatenate([s0_ref[b, 2 * j], zero], axis=1),
                 jnp.concatenate([zero, s0_ref[b, 2 * j + 1]], axis=1)], axis=0)


def _rwkv_core(p, ins, outs, scr, tick, *, heads, lora):
    (shift_ref, s0_ref, mu_ref, w0_ref, w2_ref, a0_ref, a2_ref, g2_ref,
     kk_ref, ka_ref, rk_ref, lw_ref, lb_ref) = ins
    o_ref, shift_out_ref, s_ref = outs
    prev_ref, sbd_ref = scr
    bb, cs, wa = o_ref.shape
    c = min(RWKV_CHUNK, cs)
    nsub = cs // c
    rows, cols = p.shape
    hd = s_ref.shape[-1]
    lw_, la_, lg_ = lora

    row = lax.broadcasted_iota(jnp.int32, (rows, cols), 0)
    p_prev = pltpu.roll(p, 1, axis=0)
    for b in range(bb):
        p_prev = jnp.where(row == b * cs, prev_ref[b], p_prev)
        last = p[(b + 1) * cs - 1:(b + 1) * cs, :]
        prev_ref[b] = last
        shift_out_ref[b] = last
    pm = p + (p_prev - p) * mu_ref[...]

    r = pm[:, 0:wa]
    k = pm[:, wa:2 * wa]
    v = pm[:, 2 * wa:3 * wa]
    o1 = 3 * wa
    xw = pm[:, o1:o1 + lw_]
    xa = pm[:, o1 + lw_:o1 + lw_ + la_]
    xg = pm[:, o1 + lw_ + la_:o1 + lw_ + la_ + lg_]

    w = -jax.nn.softplus(-(w0_ref[...] + _bdot(jnp.tanh(xw), w2_ref[...]))) - 0.5
    logd = -jnp.exp(w)
    tick()
    a = jax.nn.sigmoid(a0_ref[...] + _bdot(xa, a2_ref[...]))
    g = _bdot(jax.nn.sigmoid(xg), g2_ref[...])
    kk_raw = k * kk_ref[...]
    kk = kk_raw / jnp.maximum(jnp.sqrt(_group_sums(kk_raw * kk_raw, hd)), 1e-12)
    kka = kk * a
    k2 = k * (1.0 + (a - 1.0) * ka_ref[...])

    slab = min(rows, 2 * c)
    tri = _block_tri(slab, c)
    cums = [_split_dot(tri, logd[r0:r0 + slab]) for r0 in range(0, rows, slab)]
    cum = cums[0] if len(cums) == 1 else jnp.concatenate(cums, axis=0)
    tick()
    tots, tot_rows = _last_rows(cum, bb * nsub, c)
    p_inv = jnp.exp(-cum)
    p_end = jnp.exp(tot_rows - cum)
    at = (-kk * jnp.exp(cum - logd)).astype(BF16)
    rt = (r * jnp.exp(cum)).astype(BF16)
    bt = (kka * p_inv).astype(BF16)
    kt = (k2 * p_inv).astype(BF16)
    be = (kka * p_end).astype(BF16)
    ke = (k2 * p_end).astype(BF16)
    vb = v.astype(BF16)

    pw = 2 * hd
    iota = lambda shape, dim: lax.broadcasted_iota(jnp.int32, shape, dim)
    first_c = iota((c, 2 * c), 1) < c
    first_v = iota((1, pw), 1) < hd
    bd_cc = (iota((2 * c, 2 * c), 0) < c) == (iota((2 * c, 2 * c), 1) < c)
    bd_cv = (iota((2 * c, pw), 0) < c) == (iota((2 * c, pw), 1) < hd)
    bd_vv = (iota((pw, pw), 0) < hd) == (iota((pw, pw), 1) < hd)
    bd_cv4 = (iota((4 * c, pw), 0) < 2 * c) == (iota((4 * c, pw), 1) < hd)
    outer4 = (iota((4 * c, 1), 0) < c) | (iota((4 * c, 1), 0) >= 3 * c)
    strict2 = _tri(c, strict=True, reps=2)
    incl4 = _tri(c, reps=4)
    eye2 = ((iota((c, 2 * c), 1) & (c - 1)) == iota((c, 2 * c), 0)).astype(F32)
    stack2 = lambda m: jnp.concatenate([m, m], axis=0)
    bd2 = lambda m: jnp.where(bd_cv, stack2(m), 0)

    units = [(b, ci, j) for b in range(bb) for ci in range(nsub) for j in range(heads // 2)]
    idx = [(slice(b * cs + ci * c, b * cs + (ci + 1) * c), slice(j * pw, (j + 1) * pw)) for b, ci, j in units]
    xps = [jnp.concatenate([at[rs, sl], rt[rs, sl]], axis=0) for rs, sl in idx]
    ybds = [jnp.concatenate([jnp.where(first_v, jnp.concatenate([bt[rs, sl], kt[rs, sl]], axis=0), 0),
                             jnp.where(first_v, 0, jnp.concatenate([kt[rs, sl], bt[rs, sl]], axis=0))], axis=0)
            for rs, sl in idx]
    eps = [jnp.concatenate([be[rs, sl], ke[rs, sl]], axis=0) for rs, sl in idx]
    vps = [vb[rs, sl] for rs, sl in idx]
    tick()
    gps = [_bdot_nt(xp, ybd) for xp, ybd in zip(xps, ybds)]
    tick()
    a_ps = [jnp.where(strict2, jnp.where(first_c, gp[:c, 0:2 * c], gp[:c, 2 * c:4 * c]), 0.0) for gp in gps]
    k_ps = [jnp.where(strict2, jnp.where(first_c, gp[:c, 2 * c:4 * c], gp[:c, 0:2 * c]), 0.0) for gp in gps]
    akvs = [_bdot(k_p, jnp.where(bd_cv, 0, stack2(vp))) for k_p, vp in zip(k_ps, vps)]
    tick()
    ns = a_ps
    ts = [eye2 + n for n in ns]
    step = 2
    while step < c:
        nbds = [jnp.where(bd_cc, stack2(n.astype(BF16)), 0) for n in ns]
        ns = [_bdot(n, nbd) for n, nbd in zip(ns, nbds)]
        tick()
        ts = [t + _bdot(t, jnp.where(bd_cc, stack2(n.astype(BF16)), 0)) for t, n in zip(ts, ns)]
        tick()
        step *= 2
    tas = [_bdot(t, jnp.concatenate([bd2(xp[:c]), bd2(akv.astype(BF16))], axis=1))
           for t, xp, akv in zip(ts, xps, akvs)]
    tick()
    tabs = [(ta[:, :pw].astype(BF16), ta[:, pw:].astype(BF16)) for ta in tas]
    qos = [_bdot(jnp.where(incl4, gp[c:, :], 0.0),
                 jnp.concatenate([jnp.where(bd_cv4 & outer4, jnp.concatenate([ta, ta, ta, ta], axis=0), 0),
                                  jnp.where(bd_cv4, jnp.concatenate([tkv, vp, vp, tkv], axis=0), 0)], axis=1))
           for gp, (ta, tkv), vp in zip(gps, tabs, vps)]
    tick()
    qs = [(xp[c:].astype(F32) + qo[:, :pw]).astype(BF16) for xp, qo in zip(xps, qos)]
    mds = [_bdot_tn(jnp.concatenate([jnp.concatenate([ta, tkv], axis=1),
                                     jnp.concatenate([jnp.zeros_like(vp), vp], axis=1)], axis=0), ep)
           for (ta, tkv), vp, ep in zip(tabs, vps, eps)]
    tick()
    bd_vv2 = stack2(bd_vv)
    mds = [jnp.where(bd_vv2, md, 0.0) for md in mds]

    for b in range(bb):
        sbds = [sbd_ref[b, j] for j in range(heads // 2)]
        for ci in range(nsub):
            for j in range(heads // 2):
                i = units.index((b, ci, j))
                rs, sl = idx[i]
                sbd = sbds[j]
                o_ref[b, ci * c:(ci + 1) * c, sl] = _bdot_nt(qs[i], sbd) + qos[i][:, pw:]
                sbds[j] = sbd * jnp.exp(tots[b * nsub + ci][:, sl]) + _bdot(sbd, mds[i][:pw]) + mds[i][pw:]
            tick()
        for j in range(heads // 2):
            sbd_ref[b, j] = sbds[j]

    @pl.when(pl.program_id(1) == pl.num_programs(1) - 1)
    def _():
        for b in range(bb):
            for j in range(heads // 2):
                s_ref[b, 2 * j] = sbd_ref[b, j, 0:hd, 0:hd]
                s_ref[b, 2 * j + 1] = sbd_ref[b, j, hd:pw, hd:pw]

    o = o_ref[...].reshape(rows, wa)
    dev = o - _group_sums(o, hd) * (1.0 / hd)
    var = _group_sums(dev * dev, hd) * (1.0 / hd)
    o = dev * lax.rsqrt(var + LNX_EPS) * lw_ref[...] + lb_ref[...]
    tick()
    bonus = _group_sums(r * k2 * rk_ref[...], hd) * v
    o_ref[...] = ((o + bonus) * g).reshape(bb, cs, wa)


RWKV_POINTS = 24


def _rwkv_call(x, nw, sc, sh, w_in, shift_prev, s0, prm):
    b, t, _ = x.shape
    cols = w_in.shape[1]
    heads, hd = s0.shape[1], s0.shape[2]
    wa = heads * hd
    lora = (prm["w2"].shape[0], prm["a2"].shape[0], prm["g2"].shape[0])
    vec = lambda a: a.reshape(1, -1)
    consts = [vec(prm["mu"]), vec(prm["w0"]), prm["w2"], vec(prm["a0"]), prm["a2"], prm["g2"],
              vec(prm["k_k"]), vec(prm["k_a"]), vec(prm["r_k"]), vec(prm["lnx_w"]), vec(prm["lnx_b"])]
    seq = lambda bb, shape: pl.BlockSpec((bb,) + shape, lambda i, j: (i,) + (0,) * len(shape))
    return _fused_call(
        "rwkv7_mixer", _rwkv_init, functools.partial(_rwkv_core, heads=heads, lora=lora), RWKV_POINTS,
        x, nw, sc, sh, w_in, RWKV_CHUNK * RWKV_GROUP,
        ins=[shift_prev.reshape(b, 1, cols), s0, *consts],
        in_specs=lambda bb, c: [seq(bb, (1, cols)), seq(bb, s0.shape[1:])] + [_const_spec(a.shape) for a in consts],
        out_shapes=[jax.ShapeDtypeStruct((b, t, wa), F32),
                    jax.ShapeDtypeStruct((b, 1, cols), F32),
                    jax.ShapeDtypeStruct(s0.shape, F32)],
        out_specs=lambda bb, c: [pl.BlockSpec((bb, c, wa), lambda i, j: (i, j, 0)),
                                 seq(bb, (1, cols)), seq(bb, s0.shape[1:])],
        scratch=lambda bb, c: [pltpu.VMEM((bb, 1, cols), F32),
                               pltpu.VMEM((bb, heads // 2, 2 * hd, 2 * hd), F32)])


def _mamba_init(ins, outs, scr, *, conv_w):
    conv_ref, s0_ref = ins[:2]
    s_ref = outs[2]
    ubuf_ref = scr[0]
    bb, _, xbc = conv_ref.shape
    pad, hist = 8, conv_w - 1
    s_ref[...] = s0_ref[...]
    for b in range(bb):
        ubuf_ref[b, 0:pad, :] = jnp.zeros((pad, xbc), F32)
        ubuf_ref[b, pad - hist:pad, :] = conv_ref[b]


def _mamba_core(p, ins, outs, scr, tick, *, groups, conv_w):
    conv_ref, s0_ref, cw_ref, cb_ref, dtb_ref, alog_ref, dskip_ref, nw_ref = ins
    o_ref, conv_out_ref, s_ref = outs
    ubuf_ref, inter_ref = scr
    bb, c, wb = o_ref.shape
    rows = bb * c
    heads, hd, ns = s_ref.shape[1], s_ref.shape[2], s_ref.shape[3]
    xbc = conv_ref.shape[-1]
    hpg = heads // groups
    pad = 8
    hist = conv_w - 1

    ys = []
    for b in range(bb):
        u = p[b * c:(b + 1) * c, wb:wb + xbc]
        ubuf_ref[b, pad:pad + c, :] = u
        y = cb_ref[...] + cw_ref[hist:hist + 1, :] * u
        for i in range(hist):
            y = y + cw_ref[i:i + 1, :] * ubuf_ref[b, pad - hist + i:pad - hist + i + c, :]
        conv_out_ref[b] = ubuf_ref[b, pad + c - hist:pad + c, :]
        ubuf_ref[b, 0:pad, :] = ubuf_ref[b, c:c + pad, :]
        ys.append(y)
        tick()
    xc = _silu(ys[0] if bb == 1 else jnp.concatenate(ys, axis=0))
    xs = xc[:, 0:wb]
    bm = xc[:, wb:wb + groups * ns]
    cm = xc[:, wb + groups * ns:wb + 2 * groups * ns]
    z = p[:, 0:wb]
    dt_raw = p[:, wb + xbc:wb + xbc + heads]
    dt = jax.nn.softplus(dt_raw + dtb_ref[...])
    la = dt * (-jnp.exp(alog_ref[...]))

    incl = _tri(c)
    acum = _split_dot(_block_tri(rows, c), la)
    acum_t = _split_dot_tn(la, _block_tri(rows, c, upper=True))
    a_lasts, a_last_rows = _last_rows(acum, bb, c)
    e_tots = [jnp.exp(al) for al in a_lasts]
    stack = jnp.concatenate([dt, jnp.exp(acum), jnp.exp(a_last_rows - acum),
                             jnp.broadcast_to(dskip_ref[...], (8, heads))], axis=0)
    full = _expand_cols(stack, hd)
    ecum_f = full[rows:2 * rows]
    dskip_f = full[3 * rows:3 * rows + 1]
    xd_f = xs * full[0:rows]
    xde_f = xd_f * full[2 * rows:3 * rows]
    colb = _expand_cols(acum, c)
    tick()

    rss = [slice(b * c, (b + 1) * c) for b in range(bb)]
    bgs = [[bm[rs, gi * ns:(gi + 1) * ns].astype(BF16) for gi in range(groups)] for rs in rss]
    cgs = [[cm[rs, gi * ns:(gi + 1) * ns].astype(BF16) for gi in range(groups)] for rs in rss]
    cbs = [[_bdot_nt(cg, bg) for cg, bg in zip(cgb, bgb)] for cgb, bgb in zip(cgs, bgs)]
    tick()
    pairs = [(b, h) for b in range(bb) for h in range(heads)]
    sls = [slice(h * hd, (h + 1) * hd) for _, h in pairs]
    s0s = [s_ref[b, h] for b, h in pairs]
    mats = []
    for b, h in pairs:
        seg = colb[rss[b], h * c:(h + 1) * c] - acum_t[h:h + 1, rss[b]]
        decay = jnp.where(incl, jnp.exp(jnp.where(incl, seg, 0.0)), 0.0)
        mats.append((cbs[b][h // hpg] * decay).astype(BF16))
        if h % 4 == 3:
            tick()
    intra = [_bdot(m, xd_f[rss[b], sl]) for m, (b, _), sl in zip(mats, pairs, sls)]
    tick()
    inter = [_bdot_nt(cgs[b][h // hpg], s0) for (b, h), s0 in zip(pairs, s0s)]
    tick()
    upd = [_bdot_tn(xde_f[rss[b], sl], bgs[b][h // hpg]) for (b, h), sl in zip(pairs, sls)]
    tick()
    for i, ((b, h), sl) in enumerate(zip(pairs, sls)):
        s_ref[b, h] = s0s[i] * e_tots[b][:, h:h + 1] + upd[i]
        o_ref[b, :, sl] = intra[i]
        inter_ref[b, :, sl] = inter[i]

    y_all = o_ref[...].reshape(rows, wb) + inter_ref[...].reshape(rows, wb) * ecum_f + dskip_f * xs
    yv = y_all * _silu(z)
    gw = wb // groups
    for gi in range(groups):
        sl = slice(gi * gw, (gi + 1) * gw)
        yg = yv[:, sl]
        ms = jnp.mean(yg * yg, axis=-1, keepdims=True)
        o_ref[:, :, sl] = (yg * lax.rsqrt(ms + GROUP_RMS_EPS) * nw_ref[:, sl]).reshape(bb, c, gw)


MAMBA_POINTS = 16


def _mamba_call(x, nw, sc, sh, w_in, conv_prev, s0, prm, groups):
    b, t, _ = x.shape
    heads, hd, ns = s0.shape[1:]
    wb = heads * hd
    hist, xbc = conv_prev.shape[1:]
    vec = lambda a: a.reshape(1, -1)
    consts = [prm["conv_w"], vec(prm["conv_b"]), vec(prm["dt_bias"]), vec(prm["a_log"]),
              vec(prm["d_skip"]), vec(prm["norm_b_w"])]
    seq = lambda bb, shape: pl.BlockSpec((bb,) + shape, lambda i, j: (i,) + (0,) * len(shape))
    return _fused_call(
        "mamba2_mixer", functools.partial(_mamba_init, conv_w=hist + 1),
        functools.partial(_mamba_core, groups=groups, conv_w=hist + 1), MAMBA_POINTS,
        x, nw, sc, sh, w_in, MAMBA_CHUNK,
        ins=[conv_prev, s0, *consts],
        in_specs=lambda bb, c: [seq(bb, (hist, xbc)), seq(bb, s0.shape[1:])] + [_const_spec(a.shape) for a in consts],
        out_shapes=[jax.ShapeDtypeStruct((b, t, wb), F32),
                    jax.ShapeDtypeStruct(conv_prev.shape, F32),
                    jax.ShapeDtypeStruct(s0.shape, F32)],
        out_specs=lambda bb, c: [pl.BlockSpec((bb, c, wb), lambda i, j: (i, j, 0)),
                                 seq(bb, (hist, xbc)), seq(bb, s0.shape[1:])],
        scratch=lambda bb, c: [pltpu.VMEM((bb, c + 8, xbc), F32), pltpu.VMEM((bb, c, wb), F32)])


def _hgrn_sum_matrix(c):
    t = np.arange(c)[:, None]
    j = np.arange(c)[None, :]
    blocks = [(j <= t), (j > t)]
    m = c // 2
    while m >= 1:
        mid = (t // (2 * m)) * (2 * m) + m - 1
        right = (t % (2 * m)) >= m
        blocks.append(np.where(right, (j > mid) & (j <= t), (j > t) & (j <= mid)))
        m //= 2
    return np.concatenate(blocks, axis=0).astype(np.float32)


def _hgrn_init(ins, outs, scr):
    s0_ref = ins[0]
    (st_ref,) = scr
    bb, heads = s0_ref.shape[:2]
    for b in range(bb):
        for h in range(heads):
            st_ref[b, h] = s0_ref[b, h].T


def _hgrn_core(p, ins, outs, scr, tick, *, layer):
    s0_ref, sums_ref, lbp_ref, nw_ref = ins
    o_ref, s_ref = outs
    (st_ref,) = scr
    bb, c, wc = o_ref.shape
    rows, cols = p.shape
    heads, dk, dv = s0_ref.shape[1:]
    nchunks = pl.num_programs(1)
    pairs = [(b, h) for b in range(bb) for h in range(heads)]

    lbp = lbp_ref[...]
    e = jnp.exp(lbp - jnp.max(lbp, axis=0, keepdims=True))
    soft = e / jnp.sum(e, axis=0, keepdims=True)
    lb = soft[0:1, :]
    for i in range(1, layer + 1):
        lb = lb + soft[i:i + 1, :]
    lb = lb - soft[0:1, :]

    q = _silu(p[:, 0:wc])
    f = p[:, wc:2 * wc]
    v = p[:, 2 * wc:3 * wc]
    gate = p[:, 3 * wc:4 * wc]
    log_f = jnp.log(lb + (1.0 - lb) * jax.nn.sigmoid(f))
    k = (1.0 - lb) * jax.nn.sigmoid(-f)
    qk = q * k

    row = lax.broadcasted_iota(jnp.int32, (c, 1), 0)
    rr = lax.broadcasted_iota(jnp.int32, (c, c), 0)
    cc = lax.broadcasted_iota(jnp.int32, (c, c), 1)
    eye = rr == cc
    rss = [slice(b * c, (b + 1) * c) for b in range(bb)]
    qes, kes, f_tots, levels = [], [], [], []
    for rs in rss:
        sums = _split_dot(sums_ref[...], log_f[rs])
        bcum = sums[0:c]
        tick()
        qes.append((q[rs] * jnp.exp(bcum)).astype(BF16))
        kes.append((k[rs] * jnp.exp(sums[c:2 * c])).astype(BF16))
        f_tots.append(jnp.exp(bcum[c - 1:c, :]))
        lv = []
        m = c // 2
        i = 2
        while m >= 1:
            wgt = jnp.exp(sums[i * c:(i + 1) * c])
            right = (row & (2 * m - 1)) >= m
            lv.append((jnp.where(right, q[rs] * wgt, 0.0).astype(BF16),
                       jnp.where(right, 0.0, k[rs] * wgt).astype(BF16),
                       (rr ^ cc) < 2 * m))
            m //= 2
            i += 1
            tick()
        levels.append(lv)

    sls = [slice(h * dk, (h + 1) * dk) for _, h in pairs]
    atts = [jnp.where(eye, jnp.sum(qk[rss[b], sl], axis=-1, keepdims=True), 0.0) for (b, _), sl in zip(pairs, sls)]
    for li in range(len(levels[0])):
        atts = [att + jnp.where(levels[b][li][2], _bdot_nt(levels[b][li][0][:, sl], levels[b][li][1][:, sl]), 0.0)
                for att, (b, _), sl in zip(atts, pairs, sls)]
        tick()
    vhs = [v[rss[b], sl].astype(BF16) for (b, _), sl in zip(pairs, sls)]
    sts = [st_ref[b, h] for b, h in pairs]
    intra = [_bdot(att, vh) for att, vh in zip(atts, vhs)]
    tick()
    inter = [_bdot_nt(qes[b][:, sl], st) for (b, _), sl, st in zip(pairs, sls, sts)]
    tick()
    upd = [_bdot_tn(vh, kes[b][:, sl]) for vh, (b, _), sl in zip(vhs, pairs, sls)]
    tick()
    for i, ((b, h), sl) in enumerate(zip(pairs, sls)):
        st_ref[b, h] = sts[i] * f_tots[b][:, sl] + upd[i]
        o = intra[i] + inter[i]
        ms = jnp.mean(o * o, axis=-1, keepdims=True)
        o_ref[b, :, sl] = o * lax.rsqrt(ms + GROUP_RMS_EPS) * nw_ref[:, sl] * _silu(gate[rss[b], sl])

    @pl.when(pl.program_id(1) == nchunks - 1)
    def _():
        for b, h in pairs:
            s_ref[b, h] = st_ref[b, h].T


HGRN_POINTS = 23


def _hgrn_call(x, nw, sc, sh, w_in, s0, lb_param, norm_w, layer):
    b, t, _ = x.shape
    heads, dk, dv = s0.shape[1:]
    assert dk == dv
    wc = heads * dk
    sums = jnp.asarray(_hgrn_sum_matrix(math.gcd(HGRN_CHUNK, t)), dtype=BF16)
    ins = [s0, sums, lb_param, norm_w.reshape(1, wc)]
    seq = lambda bb, shape: pl.BlockSpec((bb,) + shape, lambda i, j: (i,) + (0,) * len(shape))
    return _fused_call(
        "hgrn2_mixer", _hgrn_init, functools.partial(_hgrn_core, layer=layer), HGRN_POINTS,
        x, nw, sc, sh, w_in, HGRN_CHUNK,
        ins=ins,
        in_specs=lambda bb, c: [seq(bb, s0.shape[1:])] + [_const_spec(a.shape) for a in ins[1:]],
        out_shapes=[jax.ShapeDtypeStruct((b, t, wc), F32), jax.ShapeDtypeStruct(s0.shape, F32)],
        out_specs=lambda bb, c: [pl.BlockSpec((bb, c, wc), lambda i, j: (i, j, 0)), seq(bb, s0.shape[1:])],
        scratch=lambda bb, c: [pltpu.VMEM((bb, heads, dv, dk), F32)])


def _round_up(n, m):
    return -(-n // m) * m


def _prepare_weights(w):
    a_cols = w["mu_a"].shape[1]
    b_cols = w["w_in_ab"].shape[2] - a_cols
    b_pad = _round_up(b_cols, LANE) - b_cols
    return dict(
        w_in_a=w["w_in_ab"][:, :, :a_cols].astype(BF16),
        w_in_b=jnp.pad(w["w_in_ab"][:, :, a_cols:], ((0, 0), (0, 0), (0, b_pad))).astype(BF16),
        w_out_ab=w["w_out_ab"].astype(BF16),
        w_in_c=w["w_in_c"].astype(BF16),
        w_out_c=w["w_out_c"].astype(BF16),
        w_gate=w["w_gate"].astype(BF16),
        w_up=w["w_up"].astype(BF16),
        w_down=w["w_down"].astype(BF16),
    )


def _trunk(x, mod, st_rwkv, st_shift, st_ssm, st_conv, st_hgrn, w, wb):
    depth = mod.shape[0]
    b, t, d = x.shape
    groups = (st_conv.shape[-1] - st_ssm.shape[2] * st_ssm.shape[3]) // (2 * st_ssm.shape[4])
    new_rwkv, new_shift, new_ssm, new_conv, new_hgrn = [], [], [], [], []
    for layer in range(depth):
        j = layer // 2
        sh_m, sc_m, g_m, sh_f, sc_f, g_f = (mod[layer, :, None, i * d:(i + 1) * d] for i in range(6))
        nw_mix = w["norm_mix_w"][layer].reshape(1, d)
        nw_ffn = w["norm_ffn_w"][layer].reshape(1, d)
        if layer % 2 == 0:
            prm = dict(mu=w["mu_a"][j], w0=w["w0"][j], w2=w["w2"][j], a0=w["a0"][j], a2=w["a2"][j],
                       g2=w["g2"][j], k_k=w["k_k"][j], k_a=w["k_a"][j], r_k=w["r_k"][j],
                       lnx_w=w["lnx_w"][j], lnx_b=w["lnx_b"][j])
            oa, shift_new, rwkv_new = _rwkv_call(x, nw_mix, sc_m, sh_m, wb["w_in_a"][j],
                                                 st_shift[:, j], st_rwkv[:, j], prm)
            prm_b = dict(conv_w=w["conv_w"][j], conv_b=w["conv_b"][j], dt_bias=w["dt_bias"][j],
                         a_log=w["a_log"][j], d_skip=w["d_skip"][j], norm_b_w=w["norm_b_w"][j])
            ob, conv_new, ssm_new = _mamba_call(x, nw_mix, sc_m, sh_m, wb["w_in_b"][j],
                                                st_conv[:, j], st_ssm[:, j], prm_b, groups)
            mixes, wout = [oa, ob], wb["w_out_ab"][j]
            new_rwkv.append(rwkv_new)
            new_shift.append(shift_new[:, 0])
            new_ssm.append(ssm_new)
            new_conv.append(conv_new)
        else:
            oc, hgrn_new = _hgrn_call(x, nw_mix, sc_m, sh_m, wb["w_in_c"][j],
                                      st_hgrn[:, j], w["lb_param"], w["norm_c_w"][j], j)
            mixes, wout = [oc], wb["w_out_c"][j]
            new_hgrn.append(hgrn_new)
        final_w = w["norm_out_w"].reshape(1, d) if layer == depth - 1 else None
        x = _post_call(x, mixes, wout, g_m, nw_ffn, sc_f, sh_f, g_f,
                       wb["w_gate"][layer], wb["w_up"][layer], wb["w_down"][layer], final_w)
    return (x, jnp.stack(new_rwkv, axis=1), jnp.stack(new_shift, axis=1), jnp.stack(new_ssm, axis=1),
            jnp.stack(new_conv, axis=1), jnp.stack(new_hgrn, axis=1))


def _run(x_prompt, x_sample, state_rwkv, state_rwkv_shift, state_ssm, state_conv, state_hgrn,
         c_prompt, c_sample, w):
    bp, bs = x_prompt.shape[0], x_sample.shape[0]
    rows = _round_up(bp + bs, 8)
    c_all = jnp.pad(jnp.concatenate([c_prompt, c_sample], axis=0), ((0, rows - bp - bs), (0, 0)))
    mod = _ada_call(c_all, w["ada_w"], w["ada_b"])
    wb = _prepare_weights(w)
    zeros = lambda s: jnp.zeros((bp,) + s.shape[1:], F32)
    outs_p = _trunk(x_prompt, mod[:, :bp], zeros(state_rwkv), zeros(state_rwkv_shift), zeros(state_ssm),
                    zeros(state_conv), zeros(state_hgrn), w, wb)
    outs_s = _trunk(x_sample, mod[:, bp:bp + bs], state_rwkv, state_rwkv_shift, state_ssm,
                    state_conv, state_hgrn, w, wb)
    return (outs_p[0], outs_s[0]) + outs_p[1:] + outs_s[1:]


def kernel(x_prompt, x_sample, state_rwkv, state_rwkv_shift, state_ssm, state_conv, state_hgrn, c_prompt, c_sample, norm_mix_w, norm_ffn_w, norm_out_w, ada_w, ada_b, w_in_ab, w_out_ab, mu_a, w0, w2, a0, a2, g2, k_k, k_a, r_k, lnx_w, lnx_b, conv_w, conv_b, dt_bias, a_log, d_skip, norm_b_w, w_in_c, w_out_c, lb_param, norm_c_w, w_gate, w_up, w_down):
    w = dict(norm_mix_w=norm_mix_w, norm_ffn_w=norm_ffn_w, norm_out_w=norm_out_w, ada_w=ada_w, ada_b=ada_b,
             w_in_ab=w_in_ab, w_out_ab=w_out_ab, mu_a=mu_a, w0=w0, w2=w2, a0=a0, a2=a2, g2=g2, k_k=k_k,
             k_a=k_a, r_k=r_k, lnx_w=lnx_w, lnx_b=lnx_b, conv_w=conv_w, conv_b=conv_b, dt_bias=dt_bias,
             a_log=a_log, d_skip=d_skip, norm_b_w=norm_b_w, w_in_c=w_in_c, w_out_c=w_out_c,
             lb_param=lb_param, norm_c_w=norm_c_w, w_gate=w_gate, w_up=w_up, w_down=w_down)
    return _run(x_prompt, x_sample, state_rwkv, state_rwkv_shift, state_ssm, state_conv, state_hgrn,
                c_prompt, c_sample, w)
```

```python
import functools
import math

import numpy as np
import jax
import jax.numpy as jnp
from jax import lax
from jax.experimental import pallas as pl
from jax.experimental.pallas import tpu as pltpu

F32 = jnp.float32
BF16 = jnp.bfloat16

LANE = 128
MXU_TILE = 256
VMEM_LIMIT = 56 * 2**20
ROW_TILE = 512
COL_CHUNK = 512
RWKV_CHUNK = 64
RWKV_GROUP = 4
MAMBA_CHUNK = 128
HGRN_CHUNK = 64
MIX_SEQS = 2
LNX_EPS = 64e-5
RMS_EPS = 1e-6
GROUP_RMS_EPS = 1e-5


def _bdot(a, b):
    return jnp.dot(a.astype(BF16), b.astype(BF16), preferred_element_type=F32)


def _bdot_nt(a, b):
    return lax.dot_general(a.astype(BF16), b.astype(BF16), (((1,), (1,)), ((), ())),
                           preferred_element_type=F32)


def _bdot_tn(a, b):
    return lax.dot_general(a.astype(BF16), b.astype(BF16), (((0,), (0,)), ((), ())),
                           preferred_element_type=F32)


def _silu(x):
    return x * jax.nn.sigmoid(x)


def _const_spec(shape, single_buffer=False):
    nd = len(shape)
    if single_buffer:
        return pl.BlockSpec(shape, lambda *_: (0,) * nd, pipeline_mode=pl.Buffered(1))
    return pl.BlockSpec(shape, lambda *_: (0,) * nd)


def _params(n_axes):
    return pltpu.CompilerParams(dimension_semantics=("arbitrary",) * n_axes,
                                vmem_limit_bytes=VMEM_LIMIT)


def _row_tiling(b, t):
    if t >= ROW_TILE:
        assert t % ROW_TILE == 0
        return 1, ROW_TILE
    bb = max(1, min(b, ROW_TILE // t))
    while b % bb:
        bb -= 1
    return bb, t


def _col_chunks(n):
    return [(n0, min(n0 + COL_CHUNK, n)) for n0 in range(0, n, COL_CHUNK)]


def _ada_body(c_ref, w_ref, b_ref, o_ref):
    o_ref[0] = _bdot(_silu(c_ref[...]), w_ref[0]) + b_ref[0]


def _ada_call(c_all, ada_w, ada_b):
    depth, d, n = ada_w.shape
    r = c_all.shape[0]
    tn = 1024
    assert n % tn == 0
    return pl.pallas_call(
        _ada_body,
        out_shape=jax.ShapeDtypeStruct((depth, r, n), F32),
        grid=(depth, n // tn),
        in_specs=[pl.BlockSpec((r, d), lambda l, j: (0, 0)),
                  pl.BlockSpec((1, d, tn), lambda l, j: (l, 0, j)),
                  pl.BlockSpec((1, 1, tn), lambda l, j: (l, 0, j))],
        out_specs=pl.BlockSpec((1, r, tn), lambda l, j: (l, 0, j)),
        compiler_params=_params(2),
        name="ada_mod",
    )(c_all, ada_w, ada_b.reshape(depth, 1, n))


def _norm_mod(x, nw, sc, sh):
    ms = jnp.mean(x * x, axis=-1, keepdims=True)
    y = x * lax.rsqrt(ms + RMS_EPS) * nw
    return y * (1.0 + sc) + sh


class _Ticker:
    def __init__(self, thunks, points):
        self.thunks, self.total, self.points, self.calls = list(thunks), len(thunks), points, 0

    def __call__(self):
        self.calls += 1
        due = min(self.total, -(-self.calls * self.total // self.points))
        while self.total - len(self.thunks) < due:
            self.thunks.pop(0)()

    def flush(self):
        while self.thunks:
            self.thunks.pop(0)()


def _fused_body(*refs, init, core, n_in, n_out, points):
    x_cur_ref, x_next_ref, nw_ref, sc_ref, sh_ref, w_ref = refs[:6]
    ins = refs[6:6 + n_in]
    outs = refs[6 + n_in:6 + n_in + n_out]
    scr = refs[6 + n_in + n_out:-2]
    bufs = refs[-2:]
    bb, c, d = x_cur_ref.shape
    rows = bb * c
    j = pl.program_id(1)

    def projection(x_ref, dst_ref):
        h = _norm_mod(x_ref[...], nw_ref[...], sc_ref[...], sh_ref[...])
        hb = h.reshape(rows, d).astype(BF16)

        def piece(n0, n1):
            def run():
                dst_ref[:, n0:n1] = jnp.dot(hb, w_ref[:, n0:n1], preferred_element_type=F32)
            return run
        return [piece(n0, min(n0 + MXU_TILE, w_ref.shape[1])) for n0 in range(0, w_ref.shape[1], MXU_TILE)]

    @pl.when(j == 0)
    def _():
        init(ins, outs, scr)
        for run in projection(x_cur_ref, bufs[0]):
            run()

    for parity in range(2):
        @pl.when(lax.rem(j, 2) == parity)
        def _(parity=parity):
            tick = _Ticker(projection(x_next_ref, bufs[1 - parity]), points)
            core(bufs[parity][...], ins, outs, scr, tick)
            tick.flush()


def _fused_call(name, init, core, points, x, nw, sc, sh, w_in, chunk, ins, in_specs, out_shapes, out_specs,
                scratch):
    b, t, d = x.shape
    bb, c = _mixer_tiling(b, t, chunk)
    n = t // c
    cols = w_in.shape[1]
    seq_spec = pl.BlockSpec((bb, 1, d), lambda i, j: (i, 0, 0))
    return pl.pallas_call(
        functools.partial(_fused_body, init=init, core=core, n_in=len(ins), n_out=len(out_shapes), points=points),
        out_shape=out_shapes,
        grid=(b // bb, n),
        in_specs=[pl.BlockSpec((bb, c, d), lambda i, j: (i, j, 0)),
                  pl.BlockSpec((bb, c, d), lambda i, j: (i, jnp.minimum(j + 1, n - 1), 0)),
                  _const_spec((1, d)), seq_spec, seq_spec, _const_spec(w_in.shape, single_buffer=True)]
                 + in_specs(bb, c),
        out_specs=out_specs(bb, c),
        scratch_shapes=scratch(bb, c) + [pltpu.VMEM((bb * c, cols), F32)] * 2,
        compiler_params=_params(2),
        name=name,
    )(x, x, nw, sc, sh, w_in, *ins)


def _post_body(*refs, n_mix, final):
    x_ref = refs[0]
    mix_refs = refs[1:1 + n_mix]
    (wout_ref, gm_ref, nw_ref, sc_ref, sh_ref, gf_ref, wg_ref, wu_ref, wd_ref) = refs[1 + n_mix:10 + n_mix]
    rest = refs[10 + n_mix:]
    if final:
        fw_ref, o_ref, act_ref = rest
    else:
        o_ref, act_ref = rest
    bb, tt, d = x_ref.shape
    rows = bb * tt

    mix = None
    off = 0
    for m_ref in mix_refs:
        wdt = m_ref.shape[-1]
        part = jnp.dot(m_ref[...].reshape(rows, wdt).astype(BF16), wout_ref[off:off + wdt, :],
                       preferred_element_type=F32)
        mix = part if mix is None else mix + part
        off += wdt
    x1 = x_ref[...] + gm_ref[...] * mix.reshape(bb, tt, d)

    h = _norm_mod(x1, nw_ref[...], sc_ref[...], sh_ref[...])
    hb = h.reshape(rows, d).astype(BF16)
    for f0, f1 in _col_chunks(wg_ref.shape[1]):
        gate = jnp.dot(hb, wg_ref[:, f0:f1], preferred_element_type=F32)
        up = jnp.dot(hb, wu_ref[:, f0:f1], preferred_element_type=F32)
        act_ref[:, f0:f1] = (_silu(gate) * up).astype(BF16)
    ffn = jnp.dot(act_ref[...], wd_ref[...], preferred_element_type=F32)
    x2 = x1 + gf_ref[...] * ffn.reshape(bb, tt, d)
    if final:
        ms = jnp.mean(x2 * x2, axis=-1, keepdims=True)
        x2 = x2 * lax.rsqrt(ms + RMS_EPS) * fw_ref[...]
    o_ref[...] = x2


def _post_call(x, mixes, wout, gm, nw, sc, sh, gf, wg, wu, wd, final_w):
    b, t, d = x.shape
    bb, tt = _row_tiling(b, t)
    final = final_w is not None
    row_spec = lambda w: pl.BlockSpec((bb, tt, w), lambda i, j: (i, j, 0))
    seq_spec = pl.BlockSpec((bb, 1, d), lambda i, j: (i, 0, 0))
    in_specs = ([row_spec(d)] + [row_spec(m.shape[-1]) for m in mixes]
                + [_const_spec(wout.shape, True), seq_spec, _const_spec((1, d)), seq_spec, seq_spec, seq_spec,
                   _const_spec(wg.shape, True), _const_spec(wu.shape, True), _const_spec(wd.shape, True)])
    args = [x, *mixes, wout, gm, nw, sc, sh, gf, wg, wu, wd]
    if final:
        in_specs.append(_const_spec((1, d)))
        args.append(final_w)
    return pl.pallas_call(
        functools.partial(_post_body, n_mix=len(mixes), final=final),
        out_shape=jax.ShapeDtypeStruct((b, t, d), F32),
        grid=(b // bb, t // tt),
        in_specs=in_specs,
        out_specs=row_spec(d),
        scratch_shapes=[pltpu.VMEM((bb * tt, wg.shape[1]), BF16)],
        compiler_params=_params(2),
        name="post_ffn",
    )(*args)


def _tri(c, strict=False, reps=1):
    row = lax.broadcasted_iota(jnp.int32, (c, reps * c), 0)
    col = lax.broadcasted_iota(jnp.int32, (c, reps * c), 1) & (c - 1)
    return (row > col) if strict else (row >= col)


def _block_tri(rows, c, upper=False):
    r = lax.broadcasted_iota(jnp.int32, (rows, rows), 0)
    q = lax.broadcasted_iota(jnp.int32, (rows, rows), 1)
    tri = (r <= q) if upper else (r >= q)
    return (tri & ((r ^ q) < c)).astype(BF16)


def _split3(x):
    x1 = x.astype(BF16)
    r1 = x - x1.astype(F32)
    x2 = r1.astype(BF16)
    x3 = (r1 - x2.astype(F32)).astype(BF16)
    return x1, x2, x3


def _split_dot(w01, x):
    x1, x2, x3 = _split3(x)
    dot = lambda piece: jnp.dot(w01, piece, preferred_element_type=F32)
    return dot(x1) + dot(x2) + dot(x3)


def _split_dot_tn(x, w01):
    x1, x2, x3 = _split3(x)
    dot = lambda piece: lax.dot_general(piece, w01, (((0,), (0,)), ((), ())), preferred_element_type=F32)
    return dot(x1) + dot(x2) + dot(x3)


def _expand_cols(x, width):
    k = x.shape[1]
    r = lax.broadcasted_iota(jnp.int32, (k, k * width), 0) * width
    q = lax.broadcasted_iota(jnp.int32, (k, k * width), 1)
    sel = ((q >= r) & (q < r + width)).astype(BF16)
    x1, x2, x3 = _split3(x)
    dot = lambda piece: jnp.dot(piece, sel, preferred_element_type=F32)
    return dot(x1) + dot(x2) + dot(x3)


def _group_sums(x, width):
    w = x.shape[1]
    tile = MXU_TILE if (w % MXU_TILE == 0 and MXU_TILE % width == 0) else w
    r = lax.broadcasted_iota(jnp.int32, (tile, tile), 0)
    q = lax.broadcasted_iota(jnp.int32, (tile, tile), 1)
    ones = ((r ^ q) < width).astype(BF16)
    pieces = _split3(x)
    cols = []
    for t0 in range(0, w, tile):
        acc = None
        for piece in pieces:
            part = jnp.dot(piece[:, t0:t0 + tile], ones, preferred_element_type=F32)
            acc = part if acc is None else acc + part
        cols.append(acc)
    return cols[0] if len(cols) == 1 else jnp.concatenate(cols, axis=1)


def _last_rows(x, bb, c):
    lasts = [x[(b + 1) * c - 1:(b + 1) * c, :] for b in range(bb)]
    tiled = [jnp.broadcast_to(l, (c, x.shape[1])) for l in lasts]
    return lasts, (tiled[0] if bb == 1 else jnp.concatenate(tiled, axis=0))


def _mixer_tiling(b, t, chunk):
    bb = MIX_SEQS if b % MIX_SEQS == 0 else 1
    return bb, math.gcd(chunk, t)


def _rwkv_init(ins, outs, scr):
    shift_ref, s0_ref = ins[:2]
    prev_ref, sbd_ref = scr
    bb, heads, hd = s0_ref.shape[:3]
    prev_ref[...] = shift_ref[...]
    zero = jnp.zeros((hd, hd), F32)
    for b in range(bb):
        for j in range(heads // 2):
            sbd_ref[b, j] = jnp.concatenate(
                [jnp.concatenate([s0_ref[b, 2 * j], zero], axis=1),
                 jnp.concatenate([zero, s0_ref[b, 2 * j + 1]], axis=1)], axis=0)


def _rwkv_core(p, ins, outs, scr, tick, *, heads, lora):
    (shift_ref, s0_ref, mu_ref, w0_ref, w2_ref, a0_ref, a2_ref, g2_ref,
     kk_ref, ka_ref, rk_ref, lw_ref, lb_ref) = ins
    o_ref, shift_out_ref, s_ref = outs
    prev_ref, sbd_ref = scr
    bb, cs, wa = o_ref.shape
    c = min(RWKV_CHUNK, cs)
    nsub = cs // c
    rows, cols = p.shape
    hd = s_ref.shape[-1]
    lw_, la_, lg_ = lora

    row = lax.broadcasted_iota(jnp.int32, (rows, cols), 0)
    p_prev = pltpu.roll(p, 1, axis=0)
    for b in range(bb):
        p_prev = jnp.where(row == b * cs, prev_ref[b], p_prev)
        last = p[(b + 1) * cs - 1:(b + 1) * cs, :]
        prev_ref[b] = last
        shift_out_ref[b] = last
    pm = p + (p_prev - p) * mu_ref[...]

    r = pm[:, 0:wa]
    k = pm[:, wa:2 * wa]
    v = pm[:, 2 * wa:3 * wa]
    o1 = 3 * wa
    xw = pm[:, o1:o1 + lw_]
    xa = pm[:, o1 + lw_:o1 + lw_ + la_]
    xg = pm[:, o1 + lw_ + la_:o1 + lw_ + la_ + lg_]

    w = -jax.nn.softplus(-(w0_ref[...] + _bdot(jnp.tanh(xw), w2_ref[...]))) - 0.5
    logd = -jnp.exp(w)
    tick()
    a = jax.nn.sigmoid(a0_ref[...] + _bdot(xa, a2_ref[...]))
    g = _bdot(jax.nn.sigmoid(xg), g2_ref[...])
    kk_raw = k * kk_ref[...]
    kk = kk_raw / jnp.maximum(jnp.sqrt(_group_sums(kk_raw * kk_raw, hd)), 1e-12)
    kka = kk * a
    k2 = k * (1.0 + (a - 1.0) * ka_ref[...])

    slab = min(rows, 2 * c)
    tri = _block_tri(slab, c)
    cums = [_split_dot(tri, logd[r0:r0 + slab]) for r0 in range(0, rows, slab)]
    cum = cums[0] if len(cums) == 1 else jnp.concatenate(cums, axis=0)
    tick()
    tots, tot_rows = _last_rows(cum, bb * nsub, c)
    p_inv = jnp.exp(-cum)
    p_end = jnp.exp(tot_rows - cum)
    at = (-kk * jnp.exp(cum - logd)).astype(BF16)
    rt = (r * jnp.exp(cum)).astype(BF16)
    bt = (kka * p_inv).astype(BF16)
    kt = (k2 * p_inv).astype(BF16)
    be = (kka * p_end).astype(BF16)
    ke = (k2 * p_end).astype(BF16)
    vb = v.astype(BF16)

    pw = 2 * hd
    iota = lambda shape, dim: lax.broadcasted_iota(jnp.int32, shape, dim)
    first_c = iota((c, 2 * c), 1) < c
    first_v = iota((1, pw), 1) < hd
    bd_cc = (iota((2 * c, 2 * c), 0) < c) == (iota((2 * c, 2 * c), 1) < c)
    bd_cv = (iota((2 * c, pw), 0) < c) == (iota((2 * c, pw), 1) < hd)
    bd_vv = (iota((pw, pw), 0) < hd) == (iota((pw, pw), 1) < hd)
    bd_cv4 = (iota((4 * c, pw), 0) < 2 * c) == (iota((4 * c, pw), 1) < hd)
    outer4 = (iota((4 * c, 1), 0) < c) | (iota((4 * c, 1), 0) >= 3 * c)
    strict2 = _tri(c, strict=True, reps=2)
    incl4 = _tri(c, reps=4)
    eye2 = ((iota((c, 2 * c), 1) & (c - 1)) == iota((c, 2 * c), 0)).astype(F32)
    stack2 = lambda m: jnp.concatenate([m, m], axis=0)
    bd2 = lambda m: jnp.where(bd_cv, stack2(m), 0)

    units = [(b, ci, j) for b in range(bb) for ci in range(nsub) for j in range(heads // 2)]
    idx = [(slice(b * cs + ci * c, b * cs + (ci + 1) * c), slice(j * pw, (j + 1) * pw)) for b, ci, j in units]
    xps = [jnp.concatenate([at[rs, sl], rt[rs, sl]], axis=0) for rs, sl in idx]
    ybds = [jnp.concatenate([jnp.where(first_v, jnp.concatenate([bt[rs, sl], kt[rs, sl]], axis=0), 0),
                             jnp.where(first_v, 0, jnp.concatenate([kt[rs, sl], bt[rs, sl]], axis=0))], axis=0)
            for rs, sl in idx]
    eps = [jnp.concatenate([be[rs, sl], ke[rs, sl]], axis=0) for rs, sl in idx]
    vps = [vb[rs, sl] for rs, sl in idx]
    tick()
    gps = [_bdot_nt(xp, ybd) for xp, ybd in zip(xps, ybds)]
    tick()
    a_ps = [jnp.where(strict2, jnp.where(first_c, gp[:c, 0:2 * c], gp[:c, 2 * c:4 * c]), 0.0) for gp in gps]
    k_ps = [jnp.where(strict2, jnp.where(first_c, gp[:c, 2 * c:4 * c], gp[:c, 0:2 * c]), 0.0) for gp in gps]
    akvs = [_bdot(k_p, jnp.where(bd_cv, 0, stack2(vp))) for k_p, vp in zip(k_ps, vps)]
    tick()
    ns = a_ps
    ts = [eye2 + n for n in ns]
    step = 2
    while step < c:
        nbds = [jnp.where(bd_cc, stack2(n.astype(BF16)), 0) for n in ns]
        ns = [_bdot(n, nbd) for n, nbd in zip(ns, nbds)]
        tick()
        ts = [t + _bdot(t, jnp.where(bd_cc, stack2(n.astype(BF16)), 0)) for t, n in zip(ts, ns)]
        tick()
        step *= 2
    tas = [_bdot(t, jnp.concatenate([bd2(xp[:c]), bd2(akv.astype(BF16))], axis=1))
           for t, xp, akv in zip(ts, xps, akvs)]
    tick()
    tabs = [(ta[:, :pw].astype(BF16), ta[:, pw:].astype(BF16)) for ta in tas]
    qos = [_bdot(jnp.where(incl4, gp[c:, :], 0.0),
                 jnp.concatenate([jnp.where(bd_cv4 & outer4, jnp.concatenate([ta, ta, ta, ta], axis=0), 0),
                                  jnp.where(bd_cv4, jnp.concatenate([tkv, vp, vp, tkv], axis=0), 0)], axis=1))
           for gp, (ta, tkv), vp in zip(gps, tabs, vps)]
    tick()
    qs = [(xp[c:].astype(F32) + qo[:, :pw]).astype(BF16) for xp, qo in zip(xps, qos)]
    mds = [_bdot_tn(jnp.concatenate([jnp.concatenate([ta, tkv], axis=1),
                                     jnp.concatenate([jnp.zeros_like(vp), vp], axis=1)], axis=0), ep)
           for (ta, tkv), vp, ep in zip(tabs, vps, eps)]
    tick()
    bd_vv2 = stack2(bd_vv)
    mds = [jnp.where(bd_vv2, md, 0.0) for md in mds]

    for b in range(bb):
        sbds = [sbd_ref[b, j] for j in range(heads // 2)]
        for ci in range(nsub):
            for j in range(heads // 2):
                i = units.index((b, ci, j))
                rs, sl = idx[i]
                sbd = sbds[j]
                o_ref[b, ci * c:(ci + 1) * c, sl] = _bdot_nt(qs[i], sbd) + qos[i][:, pw:]
                sbds[j] = sbd * jnp.exp(tots[b * nsub + ci][:, sl]) + _bdot(sbd, mds[i][:pw]) + mds[i][pw:]
            tick()
        for j in range(heads // 2):
            sbd_ref[b, j] = sbds[j]

    @pl.when(pl.program_id(1) == pl.num_programs(1) - 1)
    def _():
        for b in range(bb):
            for j in range(heads // 2):
                s_ref[b, 2 * j] = sbd_ref[b, j, 0:hd, 0:hd]
                s_ref[b, 2 * j + 1] = sbd_ref[b, j, hd:pw, hd:pw]

    o = o_ref[...].reshape(rows, wa)
    dev = o - _group_sums(o, hd) * (1.0 / hd)
    var = _group_sums(dev * dev, hd) * (1.0 / hd)
    o = dev * lax.rsqrt(var + LNX_EPS) * lw_ref[...] + lb_ref[...]
    tick()
    bonus = _group_sums(r * k2 * rk_ref[...], hd) * v
    o_ref[...] = ((o + bonus) * g).reshape(bb, cs, wa)


RWKV_POINTS = 24


def _rwkv_call(x, nw, sc, sh, w_in, shift_prev, s0, prm):
    b, t, _ = x.shape
    cols = w_in.shape[1]
    heads, hd = s0.shape[1], s0.shape[2]
    wa = heads * hd
    lora = (prm["w2"].shape[0], prm["a2"].shape[0], prm["g2"].shape[0])
    vec = lambda a: a.reshape(1, -1)
    consts = [vec(prm["mu"]), vec(prm["w0"]), prm["w2"], vec(prm["a0"]), prm["a2"], prm["g2"],
              vec(prm["k_k"]), vec(prm["k_a"]), vec(prm["r_k"]), vec(prm["lnx_w"]), vec(prm["lnx_b"])]
    seq = lambda bb, shape: pl.BlockSpec((bb,) + shape, lambda i, j: (i,) + (0,) * len(shape))
    return _fused_call(
        "rwkv7_mixer", _rwkv_init, functools.partial(_rwkv_core, heads=heads, lora=lora), RWKV_POINTS,
        x, nw, sc, sh, w_in, RWKV_CHUNK * RWKV_GROUP,
        ins=[shift_prev.reshape(b, 1, cols), s0, *consts],
        in_specs=lambda bb, c: [seq(bb, (1, cols)), seq(bb, s0.shape[1:])] + [_const_spec(a.shape) for a in consts],
        out_shapes=[jax.ShapeDtypeStruct((b, t, wa), F32),
                    jax.ShapeDtypeStruct((b, 1, cols), F32),
                    jax.ShapeDtypeStruct(s0.shape, F32)],
        out_specs=lambda bb, c: [pl.BlockSpec((bb, c, wa), lambda i, j: (i, j, 0)),
                                 seq(bb, (1, cols)), seq(bb, s0.shape[1:])],
        scratch=lambda bb, c: [pltpu.VMEM((bb, 1, cols), F32),
                               pltpu.VMEM((bb, heads // 2, 2 * hd, 2 * hd), F32)])


def _mamba_init(ins, outs, scr, *, conv_w):
    conv_ref, s0_ref = ins[:2]
    s_ref = outs[2]
    ubuf_ref = scr[0]
    bb, _, xbc = conv_ref.shape
    pad, hist = 8, conv_w - 1
    s_ref[...] = s0_ref[...]
    for b in range(bb):
        ubuf_ref[b, 0:pad, :] = jnp.zeros((pad, xbc), F32)
        ubuf_ref[b, pad - hist:pad, :] = conv_ref[b]


def _mamba_core(p, ins, outs, scr, tick, *, groups, conv_w):
    conv_ref, s0_ref, cw_ref, cb_ref, dtb_ref, alog_ref, dskip_ref, nw_ref = ins
    o_ref, conv_out_ref, s_ref = outs
    ubuf_ref, inter_ref = scr
    bb, c, wb = o_ref.shape
    rows = bb * c
    heads, hd, ns = s_ref.shape[1], s_ref.shape[2], s_ref.shape[3]
    xbc = conv_ref.shape[-1]
    hpg = heads // groups
    pad = 8
    hist = conv_w - 1

    ys = []
    for b in range(bb):
        u = p[b * c:(b + 1) * c, wb:wb + xbc]
        ext = jnp.concatenate([ubuf_ref[b], u], axis=0)
        y = cb_ref[...] + cw_ref[hist:hist + 1, :] * u
        for i in range(hist):
            y = y + cw_ref[i:i + 1, :] * pltpu.roll(ext, hist - i, axis=0)[pad:pad + c, :]
        conv_out_ref[b] = u[c - hist:c, :]
        ubuf_ref[b] = u[c - pad:c, :]
        ys.append(y)
        tick()
    xc = _silu(ys[0] if bb == 1 else jnp.concatenate(ys, axis=0))
    xs = xc[:, 0:wb]
    bm = xc[:, wb:wb + groups * ns]
    cm = xc[:, wb + groups * ns:wb + 2 * groups * ns]
    z = p[:, 0:wb]
    dt_raw = p[:, wb + xbc:wb + xbc + heads]
    dt = jax.nn.softplus(dt_raw + dtb_ref[...])
    la = dt * (-jnp.exp(alog_ref[...]))

    incl = _tri(c)
    acum = _split_dot(_block_tri(rows, c), la)
    acum_t = _split_dot_tn(la, _block_tri(rows, c, upper=True))
    a_lasts, a_last_rows = _last_rows(acum, bb, c)
    e_tots = [jnp.exp(al) for al in a_lasts]
    stack = jnp.concatenate([dt, jnp.exp(acum), jnp.exp(a_last_rows - acum),
                             jnp.broadcast_to(dskip_ref[...], (8, heads))], axis=0)
    full = _expand_cols(stack, hd)
    ecum_f = full[rows:2 * rows]
    dskip_f = full[3 * rows:3 * rows + 1]
    xd_f = xs * full[0:rows]
    xde_f = xd_f * full[2 * rows:3 * rows]
    colb = _expand_cols(acum, c)
    tick()

    rss = [slice(b * c, (b + 1) * c) for b in range(bb)]
    bgs = [[bm[rs, gi * ns:(gi + 1) * ns].astype(BF16) for gi in range(groups)] for rs in rss]
    cgs = [[cm[rs, gi * ns:(gi + 1) * ns].astype(BF16) for gi in range(groups)] for rs in rss]
    cbs = [[_bdot_nt(cg, bg) for cg, bg in zip(cgb, bgb)] for cgb, bgb in zip(cgs, bgs)]
    tick()
    pairs = [(b, h) for b in range(bb) for h in range(heads)]
    sls = [slice(h * hd, (h + 1) * hd) for _, h in pairs]
    s0s = [s_ref[b, h] for b, h in pairs]
    mats = []
    for b, h in pairs:
        seg = colb[rss[b], h * c:(h + 1) * c] - acum_t[h:h + 1, rss[b]]
        decay = jnp.where(incl, jnp.exp(jnp.where(incl, seg, 0.0)), 0.0)
        mats.append((cbs[b][h // hpg] * decay).astype(BF16))
        if h % 4 == 3:
            tick()
    intra = [_bdot(m, xd_f[rss[b], sl]) for m, (b, _), sl in zip(mats, pairs, sls)]
    tick()
    inter = [_bdot_nt(cgs[b][h // hpg], s0) for (b, h), s0 in zip(pairs, s0s)]
    tick()
    upd = [_bdot_tn(xde_f[rss[b], sl], bgs[b][h // hpg]) for (b, h), sl in zip(pairs, sls)]
    tick()
    for i, ((b, h), sl) in enumerate(zip(pairs, sls)):
        s_ref[b, h] = s0s[i] * e_tots[b][:, h:h + 1] + upd[i]
        o_ref[b, :, sl] = intra[i]
        inter_ref[b, :, sl] = inter[i]

    y_all = o_ref[...].reshape(rows, wb) + inter_ref[...].reshape(rows, wb) * ecum_f + dskip_f * xs
    yv = y_all * _silu(z)
    gw = wb // groups
    for gi in range(groups):
        sl = slice(gi * gw, (gi + 1) * gw)
        yg = yv[:, sl]
        ms = jnp.mean(yg * yg, axis=-1, keepdims=True)
        o_ref[:, :, sl] = (yg * lax.rsqrt(ms + GROUP_RMS_EPS) * nw_ref[:, sl]).reshape(bb, c, gw)


MAMBA_POINTS = 16


def _mamba_call(x, nw, sc, sh, w_in, conv_prev, s0, prm, groups):
    b, t, _ = x.shape
    heads, hd, ns = s0.shape[1:]
    wb = heads * hd
    hist, xbc = conv_prev.shape[1:]
    vec = lambda a: a.reshape(1, -1)
    consts = [prm["conv_w"], vec(prm["conv_b"]), vec(prm["dt_bias"]), vec(prm["a_log"]),
              vec(prm["d_skip"]), vec(prm["norm_b_w"])]
    seq = lambda bb, shape: pl.BlockSpec((bb,) + shape, lambda i, j: (i,) + (0,) * len(shape))
    return _fused_call(
        "mamba2_mixer", functools.partial(_mamba_init, conv_w=hist + 1),
        functools.partial(_mamba_core, groups=groups, conv_w=hist + 1), MAMBA_POINTS,
        x, nw, sc, sh, w_in, MAMBA_CHUNK,
        ins=[conv_prev, s0, *consts],
        in_specs=lambda bb, c: [seq(bb, (hist, xbc)), seq(bb, s0.shape[1:])] + [_const_spec(a.shape) for a in consts],
        out_shapes=[jax.ShapeDtypeStruct((b, t, wb), F32),
                    jax.ShapeDtypeStruct(conv_prev.shape, F32),
                    jax.ShapeDtypeStruct(s0.shape, F32)],
        out_specs=lambda bb, c: [pl.BlockSpec((bb, c, wb), lambda i, j: (i, j, 0)),
                                 seq(bb, (hist, xbc)), seq(bb, s0.shape[1:])],
        scratch=lambda bb, c: [pltpu.VMEM((bb, 8, xbc), F32), pltpu.VMEM((bb, c, wb), F32)])


HGRN_FINE = 4


def _hgrn_sum_matrix(c):
    t = np.arange(c)[:, None]
    j = np.arange(c)[None, :]
    blocks = [(j <= t)]
    m = min(HGRN_FINE, c // 2)
    while m >= 1:
        mid = (t // (2 * m)) * (2 * m) + m - 1
        right = (t % (2 * m)) >= m
        blocks.append(np.where(right, (j > mid) & (j <= t), (j > t) & (j <= mid)))
        m //= 2
    return np.concatenate(blocks, axis=0).astype(np.float32)


def _hgrn_init(ins, outs, scr):
    s0_ref = ins[0]
    (st_ref,) = scr
    bb, heads = s0_ref.shape[:2]
    for b in range(bb):
        for h in range(heads):
            st_ref[b, h] = s0_ref[b, h].T


def _hgrn_core(p, ins, outs, scr, tick, *, layer):
    s0_ref, sums_ref, lbp_ref, nw_ref = ins
    o_ref, s_ref = outs
    (st_ref,) = scr
    bb, c, wc = o_ref.shape
    rows, cols = p.shape
    heads, dk, dv = s0_ref.shape[1:]
    nchunks = pl.num_programs(1)
    pairs = [(b, h) for b in range(bb) for h in range(heads)]

    lbp = lbp_ref[...]
    e = jnp.exp(lbp - jnp.max(lbp, axis=0, keepdims=True))
    soft = e / jnp.sum(e, axis=0, keepdims=True)
    lb = soft[0:1, :]
    for i in range(1, layer + 1):
        lb = lb + soft[i:i + 1, :]
    lb = lb - soft[0:1, :]

    q = _silu(p[:, 0:wc])
    f = p[:, wc:2 * wc]
    v = p[:, 2 * wc:3 * wc]
    gate = p[:, 3 * wc:4 * wc]
    log_f = jnp.log(lb + (1.0 - lb) * jax.nn.sigmoid(f))
    k = (1.0 - lb) * jax.nn.sigmoid(-f)
    qk = q * k

    row = lax.broadcasted_iota(jnp.int32, (c, 1), 0)
    rr = lax.broadcasted_iota(jnp.int32, (c, c), 0)
    cc = lax.broadcasted_iota(jnp.int32, (c, c), 1)
    eye = rr == cc
    rss = [slice(b * c, (b + 1) * c) for b in range(bb)]
    qes, kes, f_tots, levels = [], [], [], []
    for rs in rss:
        sums = _split_dot(sums_ref[...], log_f[rs])
        bcum = sums[0:c]
        tot = bcum[c - 1:c, :]
        tick()
        qes.append((q[rs] * jnp.exp(bcum)).astype(BF16))
        kes.append((k[rs] * jnp.exp(tot - bcum)).astype(BF16))
        f_tots.append(jnp.exp(tot))
        lv = []
        m = c // 2
        i = 1
        while m >= 1:
            if m > HGRN_FINE:
                mids = [jnp.broadcast_to(bcum[r0 + m - 1:r0 + m, :], (2 * m, wc)) for r0 in range(0, c, 2 * m)]
                diff = bcum - (mids[0] if len(mids) == 1 else jnp.concatenate(mids, axis=0))
                z = jnp.where((row & (2 * m - 1)) >= m, diff, -diff)
            else:
                z = sums[i * c:(i + 1) * c]
                i += 1
            wgt = jnp.exp(z)
            lv.append(((q[rs] * wgt).astype(BF16), (k[rs] * wgt).astype(BF16),
                       (rr > cc) & ((rr ^ cc) >= m) & ((rr ^ cc) < 2 * m)))
            m //= 2
            tick()
        levels.append(lv)

    sls = [slice(h * dk, (h + 1) * dk) for _, h in pairs]
    atts = [jnp.where(eye, jnp.sum(qk[rss[b], sl], axis=-1, keepdims=True), 0.0) for (b, _), sl in zip(pairs, sls)]
    for li in range(len(levels[0])):
        atts = [att + jnp.where(levels[b][li][2], _bdot_nt(levels[b][li][0][:, sl], levels[b][li][1][:, sl]), 0.0)
                for att, (b, _), sl in zip(atts, pairs, sls)]
        tick()
    vhs = [v[rss[b], sl].astype(BF16) for (b, _), sl in zip(pairs, sls)]
    sts = [st_ref[b, h] for b, h in pairs]
    intra = [_bdot(att, vh) for att, vh in zip(atts, vhs)]
    tick()
    inter = [_bdot_nt(qes[b][:, sl], st) for (b, _), sl, st in zip(pairs, sls, sts)]
    tick()
    upd = [_bdot_tn(vh, kes[b][:, sl]) for vh, (b, _), sl in zip(vhs, pairs, sls)]
    tick()
    for i, ((b, h), sl) in enumerate(zip(pairs, sls)):
        st_ref[b, h] = sts[i] * f_tots[b][:, sl] + upd[i]
        o = intra[i] + inter[i]
        ms = jnp.mean(o * o, axis=-1, keepdims=True)
        o_ref[b, :, sl] = o * lax.rsqrt(ms + GROUP_RMS_EPS) * nw_ref[:, sl] * _silu(gate[rss[b], sl])

    @pl.when(pl.program_id(1) == nchunks - 1)
    def _():
        for b, h in pairs:
            s_ref[b, h] = st_ref[b, h].T


HGRN_POINTS = 23


def _hgrn_call(x, nw, sc, sh, w_in, s0, lb_param, norm_w, layer):
    b, t, _ = x.shape
    heads, dk, dv = s0.shape[1:]
    assert dk == dv
    wc = heads * dk
    sums = jnp.asarray(_hgrn_sum_matrix(math.gcd(HGRN_CHUNK, t)), dtype=BF16)
    ins = [s0, sums, lb_param, norm_w.reshape(1, wc)]
    seq = lambda bb, shape: pl.BlockSpec((bb,) + shape, lambda i, j: (i,) + (0,) * len(shape))
    return _fused_call(
        "hgrn2_mixer", _hgrn_init, functools.partial(_hgrn_core, layer=layer), HGRN_POINTS,
        x, nw, sc, sh, w_in, HGRN_CHUNK,
        ins=ins,
        in_specs=lambda bb, c: [seq(bb, s0.shape[1:])] + [_const_spec(a.shape) for a in ins[1:]],
        out_shapes=[jax.ShapeDtypeStruct((b, t, wc), F32), jax.ShapeDtypeStruct(s0.shape, F32)],
        out_specs=lambda bb, c: [pl.BlockSpec((bb, c, wc), lambda i, j: (i, j, 0)), seq(bb, s0.shape[1:])],
        scratch=lambda bb, c: [pltpu.VMEM((bb, heads, dv, dk), F32)])


def _round_up(n, m):
    return -(-n // m) * m


def _prepare_weights(w):
    a_cols = w["mu_a"].shape[1]
    b_cols = w["w_in_ab"].shape[2] - a_cols
    b_pad = _round_up(b_cols, LANE) - b_cols
    return dict(
        w_in_a=w["w_in_ab"][:, :, :a_cols].astype(BF16),
        w_in_b=jnp.pad(w["w_in_ab"][:, :, a_cols:], ((0, 0), (0, 0), (0, b_pad))).astype(BF16),
        w_out_ab=w["w_out_ab"].astype(BF16),
        w_in_c=w["w_in_c"].astype(BF16),
        w_out_c=w["w_out_c"].astype(BF16),
        w_gate=w["w_gate"].astype(BF16),
        w_up=w["w_up"].astype(BF16),
        w_down=w["w_down"].astype(BF16),
    )


def _trunk(x, mod, st_rwkv, st_shift, st_ssm, st_conv, st_hgrn, w, wb):
    depth = mod.shape[0]
    b, t, d = x.shape
    groups = (st_conv.shape[-1] - st_ssm.shape[2] * st_ssm.shape[3]) // (2 * st_ssm.shape[4])
    new_rwkv, new_shift, new_ssm, new_conv, new_hgrn = [], [], [], [], []
    for layer in range(depth):
        j = layer // 2
        sh_m, sc_m, g_m, sh_f, sc_f, g_f = (mod[layer, :, None, i * d:(i + 1) * d] for i in range(6))
        nw_mix = w["norm_mix_w"][layer].reshape(1, d)
        nw_ffn = w["norm_ffn_w"][layer].reshape(1, d)
        if layer % 2 == 0:
            prm = dict(mu=w["mu_a"][j], w0=w["w0"][j], w2=w["w2"][j], a0=w["a0"][j], a2=w["a2"][j],
                       g2=w["g2"][j], k_k=w["k_k"][j], k_a=w["k_a"][j], r_k=w["r_k"][j],
                       lnx_w=w["lnx_w"][j], lnx_b=w["lnx_b"][j])
            oa, shift_new, rwkv_new = _rwkv_call(x, nw_mix, sc_m, sh_m, wb["w_in_a"][j],
                                                 st_shift[:, j], st_rwkv[:, j], prm)
            prm_b = dict(conv_w=w["conv_w"][j], conv_b=w["conv_b"][j], dt_bias=w["dt_bias"][j],
                         a_log=w["a_log"][j], d_skip=w["d_skip"][j], norm_b_w=w["norm_b_w"][j])
            ob, conv_new, ssm_new = _mamba_call(x, nw_mix, sc_m, sh_m, wb["w_in_b"][j],
                                                st_conv[:, j], st_ssm[:, j], prm_b, groups)
            mixes, wout = [oa, ob], wb["w_out_ab"][j]
            new_rwkv.append(rwkv_new)
            new_shift.append(shift_new[:, 0])
            new_ssm.append(ssm_new)
            new_conv.append(conv_new)
        else:
            oc, hgrn_new = _hgrn_call(x, nw_mix, sc_m, sh_m, wb["w_in_c"][j],
                                      st_hgrn[:, j], w["lb_param"], w["norm_c_w"][j], j)
            mixes, wout = [oc], wb["w_out_c"][j]
            new_hgrn.append(hgrn_new)
        final_w = w["norm_out_w"].reshape(1, d) if layer == depth - 1 else None
        x = _post_call(x, mixes, wout, g_m, nw_ffn, sc_f, sh_f, g_f,
                       wb["w_gate"][layer], wb["w_up"][layer], wb["w_down"][layer], final_w)
    return (x, jnp.stack(new_rwkv, axis=1), jnp.stack(new_shift, axis=1), jnp.stack(new_ssm, axis=1),
            jnp.stack(new_conv, axis=1), jnp.stack(new_hgrn, axis=1))


def _run(x_prompt, x_sample, state_rwkv, state_rwkv_shift, state_ssm, state_conv, state_hgrn,
         c_prompt, c_sample, w):
    bp, bs = x_prompt.shape[0], x_sample.shape[0]
    rows = _round_up(bp + bs, 8)
    c_all = jnp.pad(jnp.concatenate([c_prompt, c_sample], axis=0), ((0, rows - bp - bs), (0, 0)))
    mod = _ada_call(c_all, w["ada_w"], w["ada_b"])
    wb = _prepare_weights(w)
    zeros = lambda s: jnp.zeros((bp,) + s.shape[1:], F32)
    outs_p = _trunk(x_prompt, mod[:, :bp], zeros(state_rwkv), zeros(state_rwkv_shift), zeros(state_ssm),
                    zeros(state_conv), zeros(state_hgrn), w, wb)
    outs_s = _trunk(x_sample, mod[:, bp:bp + bs], state_rwkv, state_rwkv_shift, state_ssm,
                    state_conv, state_hgrn, w, wb)
    return (outs_p[0], outs_s[0]) + outs_p[1:] + outs_s[1:]


def kernel(x_prompt, x_sample, state_rwkv, state_rwkv_shift, state_ssm, state_conv, state_hgrn, c_prompt, c_sample, norm_mix_w, norm_ffn_w, norm_out_w, ada_w, ada_b, w_in_ab, w_out_ab, mu_a, w0, w2, a0, a2, g2, k_k, k_a, r_k, lnx_w, lnx_b, conv_w, conv_b, dt_bias, a_log, d_skip, norm_b_w, w_in_c, w_out_c, lb_param, norm_c_w, w_gate, w_up, w_down):
    w = dict(norm_mix_w=norm_mix_w, norm_ffn_w=norm_ffn_w, norm_out_w=norm_out_w, ada_w=ada_w, ada_b=ada_b,
             w_in_ab=w_in_ab, w_out_ab=w_out_ab, mu_a=mu_a, w0=w0, w2=w2, a0=a0, a2=a2, g2=g2, k_k=k_k,
             k_a=k_a, r_k=r_k, lnx_w=lnx_w, lnx_b=lnx_b, conv_w=conv_w, conv_b=conv_b, dt_bias=dt_bias,
             a_log=a_log, d_skip=d_skip, norm_b_w=norm_b_w, w_in_c=w_in_c, w_out_c=w_out_c,
             lb_param=lb_param, norm_c_w=norm_c_w, w_gate=w_gate, w_up=w_up, w_down=w_down)
    return _run(x_prompt, x_sample, state_rwkv, state_rwkv_shift, state_ssm, state_conv, state_hgrn,
                c_prompt, c_sample, w)
```

```python
import functools
import math

import numpy as np
import jax
import jax.numpy as jnp
from jax import lax
from jax.experimental import pallas as pl
from jax.experimental.pallas import tpu as pltpu

F32 = jnp.float32
BF16 = jnp.bfloat16

LANE = 128
MXU_TILE = 256
VMEM_LIMIT = 56 * 2**20
ROW_TILE = 512
COL_CHUNK = 512
RWKV_CHUNK = 64
RWKV_GROUP = 4
MAMBA_CHUNK = 128
HGRN_CHUNK = 64
HGRN_GROUP = 2
MIX_SEQS = 4
LNX_EPS = 64e-5
RMS_EPS = 1e-6
GROUP_RMS_EPS = 1e-5


def _bdot(a, b):
    return jnp.dot(a.astype(BF16), b.astype(BF16), preferred_element_type=F32)


def _bdot_nt(a, b):
    return lax.dot_general(a.astype(BF16), b.astype(BF16), (((1,), (1,)), ((), ())),
                           preferred_element_type=F32)


def _bdot_tn(a, b):
    return lax.dot_general(a.astype(BF16), b.astype(BF16), (((0,), (0,)), ((), ())),
                           preferred_element_type=F32)


def _silu(x):
    return x * jax.nn.sigmoid(x)


def _const_spec(shape, single_buffer=False):
    nd = len(shape)
    if single_buffer:
        return pl.BlockSpec(shape, lambda *_: (0,) * nd, pipeline_mode=pl.Buffered(1))
    return pl.BlockSpec(shape, lambda *_: (0,) * nd)


def _params(n_axes):
    return pltpu.CompilerParams(dimension_semantics=("arbitrary",) * n_axes,
                                vmem_limit_bytes=VMEM_LIMIT)


def _row_tiling(b, t):
    if t >= ROW_TILE:
        assert t % ROW_TILE == 0
        return 1, ROW_TILE
    bb = max(1, min(b, ROW_TILE // t))
    while b % bb:
        bb -= 1
    return bb, t


def _col_chunks(n):
    return [(n0, min(n0 + COL_CHUNK, n)) for n0 in range(0, n, COL_CHUNK)]


def _ada_body(c_ref, w_ref, b_ref, o_ref):
    o_ref[0] = _bdot(_silu(c_ref[...]), w_ref[0]) + b_ref[0]


def _ada_call(c_all, ada_w, ada_b):
    depth, d, n = ada_w.shape
    r = c_all.shape[0]
    tn = 1024
    assert n % tn == 0
    return pl.pallas_call(
        _ada_body,
        out_shape=jax.ShapeDtypeStruct((depth, r, n), F32),
        grid=(depth, n // tn),
        in_specs=[pl.BlockSpec((r, d), lambda l, j: (0, 0)),
                  pl.BlockSpec((1, d, tn), lambda l, j: (l, 0, j)),
                  pl.BlockSpec((1, 1, tn), lambda l, j: (l, 0, j))],
        out_specs=pl.BlockSpec((1, r, tn), lambda l, j: (l, 0, j)),
        compiler_params=_params(2),
        name="ada_mod",
    )(c_all, ada_w, ada_b.reshape(depth, 1, n))


def _norm_mod(x, nw, sc, sh):
    ms = jnp.mean(x * x, axis=-1, keepdims=True)
    y = x * lax.rsqrt(ms + RMS_EPS) * nw
    return y * (1.0 + sc) + sh


class _Ticker:
    def __init__(self, thunks, points):
        self.thunks, self.total, self.points, self.calls = list(thunks), len(thunks), points, 0

    def __call__(self):
        self.calls += 1
        due = min(self.total, -(-self.calls * self.total // self.points))
        while self.total - len(self.thunks) < due:
            self.thunks.pop(0)()

    def flush(self):
        while self.thunks:
            self.thunks.pop(0)()


def _fused_body(*refs, init, core, n_in, n_out, points, n_steps):
    x_cur_ref, x_next_ref, nw_ref, sc_ref, sh_ref, w_ref = refs[:6]
    ins = refs[6:6 + n_in]
    outs = refs[6 + n_in:6 + n_in + n_out]
    scr = refs[6 + n_in + n_out:-2]
    bufs = refs[-2:]
    bb, c, d = x_cur_ref.shape
    rows = bb * c
    j = pl.program_id(1)

    def projection(x_ref, dst_ref):
        h = _norm_mod(x_ref[...], nw_ref[...], sc_ref[...], sh_ref[...])
        hb = h.reshape(rows, d).astype(BF16)

        def piece(n0, n1):
            def run():
                dst_ref[:, n0:n1] = jnp.dot(hb, w_ref[:, n0:n1], preferred_element_type=F32)
            return run
        return [piece(n0, min(n0 + MXU_TILE, w_ref.shape[1])) for n0 in range(0, w_ref.shape[1], MXU_TILE)]

    @pl.when(j == 0)
    def _():
        init(ins, outs, scr)
        for run in projection(x_cur_ref, bufs[0]):
            run()

    for parity in range(2):
        @pl.when(lax.rem(j, 2) == parity)
        def _(parity=parity):
            ahead = projection(x_next_ref, bufs[1 - parity]) if n_steps > 1 else []
            tick = _Ticker(ahead, points)
            core(bufs[parity][...], ins, outs, scr, tick)
            tick.flush()


def _fused_call(name, init, core, points, x, nw, sc, sh, w_in, chunk, ins, in_specs, out_shapes, out_specs,
                scratch):
    b, t, d = x.shape
    bb, c = _mixer_tiling(b, t, chunk)
    n = t // c
    cols = w_in.shape[1]
    seq_spec = pl.BlockSpec((bb, 1, d), lambda i, j: (i, 0, 0))
    return pl.pallas_call(
        functools.partial(_fused_body, init=init, core=core, n_in=len(ins), n_out=len(out_shapes), points=points,
                          n_steps=n),
        out_shape=out_shapes,
        grid=(b // bb, n),
        in_specs=[pl.BlockSpec((bb, c, d), lambda i, j: (i, j, 0)),
                  pl.BlockSpec((bb, c, d), lambda i, j: (i, jnp.minimum(j + 1, n - 1), 0)),
                  _const_spec((1, d)), seq_spec, seq_spec, _const_spec(w_in.shape, single_buffer=True)]
                 + in_specs(bb, c),
        out_specs=out_specs(bb, c),
        scratch_shapes=scratch(bb, c) + [pltpu.VMEM((bb * c, cols), F32)] * 2,
        compiler_params=_params(2),
        name=name,
    )(x, x, nw, sc, sh, w_in, *ins)


def _post_body(*refs, n_mix, final):
    x_ref = refs[0]
    mix_refs = refs[1:1 + n_mix]
    (wout_ref, gm_ref, nw_ref, sc_ref, sh_ref, gf_ref, wg_ref, wu_ref, wd_ref) = refs[1 + n_mix:10 + n_mix]
    rest = refs[10 + n_mix:]
    if final:
        fw_ref, o_ref, act_ref = rest
    else:
        o_ref, act_ref = rest
    bb, tt, d = x_ref.shape
    rows = bb * tt

    mix = None
    off = 0
    for m_ref in mix_refs:
        wdt = m_ref.shape[-1]
        part = jnp.dot(m_ref[...].reshape(rows, wdt).astype(BF16), wout_ref[off:off + wdt, :],
                       preferred_element_type=F32)
        mix = part if mix is None else mix + part
        off += wdt
    x1 = x_ref[...] + gm_ref[...] * mix.reshape(bb, tt, d)

    h = _norm_mod(x1, nw_ref[...], sc_ref[...], sh_ref[...])
    hb = h.reshape(rows, d).astype(BF16)
    for f0, f1 in _col_chunks(wg_ref.shape[1]):
        gate = jnp.dot(hb, wg_ref[:, f0:f1], preferred_element_type=F32)
        up = jnp.dot(hb, wu_ref[:, f0:f1], preferred_element_type=F32)
        act_ref[:, f0:f1] = (_silu(gate) * up).astype(BF16)
    ffn = jnp.dot(act_ref[...], wd_ref[...], preferred_element_type=F32)
    x2 = x1 + gf_ref[...] * ffn.reshape(bb, tt, d)
    if final:
        ms = jnp.mean(x2 * x2, axis=-1, keepdims=True)
        x2 = x2 * lax.rsqrt(ms + RMS_EPS) * fw_ref[...]
    o_ref[...] = x2


def _post_call(x, mixes, wout, gm, nw, sc, sh, gf, wg, wu, wd, final_w):
    b, t, d = x.shape
    bb, tt = _row_tiling(b, t)
    final = final_w is not None
    row_spec = lambda w: pl.BlockSpec((bb, tt, w), lambda i, j: (i, j, 0))
    seq_spec = pl.BlockSpec((bb, 1, d), lambda i, j: (i, 0, 0))
    in_specs = ([row_spec(d)] + [row_spec(m.shape[-1]) for m in mixes]
                + [_const_spec(wout.shape, True), seq_spec, _const_spec((1, d)), seq_spec, seq_spec, seq_spec,
                   _const_spec(wg.shape, True), _const_spec(wu.shape, True), _const_spec(wd.shape, True)])
    args = [x, *mixes, wout, gm, nw, sc, sh, gf, wg, wu, wd]
    if final:
        in_specs.append(_const_spec((1, d)))
        args.append(final_w)
    return pl.pallas_call(
        functools.partial(_post_body, n_mix=len(mixes), final=final),
        out_shape=jax.ShapeDtypeStruct((b, t, d), F32),
        grid=(b // bb, t // tt),
        in_specs=in_specs,
        out_specs=row_spec(d),
        scratch_shapes=[pltpu.VMEM((bb * tt, wg.shape[1]), BF16)],
        compiler_params=_params(2),
        name="post_ffn",
    )(*args)


def _tri(c, strict=False, reps=1):
    row = lax.broadcasted_iota(jnp.int32, (c, reps * c), 0)
    col = lax.broadcasted_iota(jnp.int32, (c, reps * c), 1) & (c - 1)
    return (row > col) if strict else (row >= col)


def _block_tri(rows, c, upper=False):
    r = lax.broadcasted_iota(jnp.int32, (rows, rows), 0)
    q = lax.broadcasted_iota(jnp.int32, (rows, rows), 1)
    tri = (r <= q) if upper else (r >= q)
    return (tri & ((r ^ q) < c)).astype(BF16)


def _split3(x):
    x1 = x.astype(BF16)
    r1 = x - x1.astype(F32)
    x2 = r1.astype(BF16)
    x3 = (r1 - x2.astype(F32)).astype(BF16)
    return x1, x2, x3


def _split_dot(w01, x):
    x1, x2, x3 = _split3(x)
    dot = lambda piece: jnp.dot(w01, piece, preferred_element_type=F32)
    return dot(x1) + dot(x2) + dot(x3)


def _split_dot_tn(x, w01):
    x1, x2, x3 = _split3(x)
    dot = lambda piece: lax.dot_general(piece, w01, (((0,), (0,)), ((), ())), preferred_element_type=F32)
    return dot(x1) + dot(x2) + dot(x3)


def _expand_cols(x, width):
    k = x.shape[1]
    r = lax.broadcasted_iota(jnp.int32, (k, k * width), 0) * width
    q = lax.broadcasted_iota(jnp.int32, (k, k * width), 1)
    sel = ((q >= r) & (q < r + width)).astype(BF16)
    x1, x2, x3 = _split3(x)
    dot = lambda piece: jnp.dot(piece, sel, preferred_element_type=F32)
    return dot(x1) + dot(x2) + dot(x3)


def _group_sums(x, width):
    w = x.shape[1]
    tile = MXU_TILE if (w % MXU_TILE == 0 and MXU_TILE % width == 0) else w
    r = lax.broadcasted_iota(jnp.int32, (tile, tile), 0)
    q = lax.broadcasted_iota(jnp.int32, (tile, tile), 1)
    ones = ((r ^ q) < width).astype(BF16)
    pieces = _split3(x)
    cols = []
    for t0 in range(0, w, tile):
        acc = None
        for piece in pieces:
            part = jnp.dot(piece[:, t0:t0 + tile], ones, preferred_element_type=F32)
            acc = part if acc is None else acc + part
        cols.append(acc)
    return cols[0] if len(cols) == 1 else jnp.concatenate(cols, axis=1)


def _last_rows(x, bb, c):
    lasts = [x[(b + 1) * c - 1:(b + 1) * c, :] for b in range(bb)]
    tiled = [jnp.broadcast_to(l, (c, x.shape[1])) for l in lasts]
    return lasts, (tiled[0] if bb == 1 else jnp.concatenate(tiled, axis=0))


def _mixer_tiling(b, t, chunk):
    bb = max(n for n in range(1, MIX_SEQS + 1) if b % n == 0)
    return bb, math.gcd(chunk, t)


def _rwkv_init(ins, outs, scr):
    shift_ref, s0_ref = ins[:2]
    prev_ref, sbd_ref = scr
    bb, heads, hd = s0_ref.shape[:3]
    prev_ref[...] = shift_ref[...]
    zero = jnp.zeros((hd, hd), F32)
    for b in range(bb):
        for j in range(heads // 2):
            sbd_ref[b, j] = jnp.concatenate(
                [jnp.concatenate([s0_ref[b, 2 * j], zero], axis=1),
                 jnp.concatenate([zero, s0_ref[b, 2 * j + 1]], axis=1)], axis=0)


def _rwkv_core(p, ins, outs, scr, tick, *, heads, lora):
    (shift_ref, s0_ref, mu_ref, w0_ref, w2_ref, a0_ref, a2_ref, g2_ref,
     kk_ref, ka_ref, rk_ref, lw_ref, lb_ref) = ins
    o_ref, shift_out_ref, s_ref = outs
    prev_ref, sbd_ref = scr
    bb, cs, wa = o_ref.shape
    c = min(RWKV_CHUNK, cs)
    nsub = cs // c
    rows, cols = p.shape
    hd = s_ref.shape[-1]
    lw_, la_, lg_ = lora

    row = lax.broadcasted_iota(jnp.int32, (rows, cols), 0)
    p_prev = pltpu.roll(p, 1, axis=0)
    for b in range(bb):
        p_prev = jnp.where(row == b * cs, prev_ref[b], p_prev)
        last = p[(b + 1) * cs - 1:(b + 1) * cs, :]
        prev_ref[b] = last
        shift_out_ref[b] = last
    pm = p + (p_prev - p) * mu_ref[...]

    r = pm[:, 0:wa]
    k = pm[:, wa:2 * wa]
    v = pm[:, 2 * wa:3 * wa]
    o1 = 3 * wa
    xw = pm[:, o1:o1 + lw_]
    xa = pm[:, o1 + lw_:o1 + lw_ + la_]
    xg = pm[:, o1 + lw_ + la_:o1 + lw_ + la_ + lg_]

    w = -jax.nn.softplus(-(w0_ref[...] + _bdot(jnp.tanh(xw), w2_ref[...]))) - 0.5
    logd = -jnp.exp(w)
    tick()
    a = jax.nn.sigmoid(a0_ref[...] + _bdot(xa, a2_ref[...]))
    g = _bdot(jax.nn.sigmoid(xg), g2_ref[...])
    kk_raw = k * kk_ref[...]
    kk = kk_raw / jnp.maximum(jnp.sqrt(_group_sums(kk_raw * kk_raw, hd)), 1e-12)
    kka = kk * a
    k2 = k * (1.0 + (a - 1.0) * ka_ref[...])

    slab = min(rows, 2 * c)
    tri = _block_tri(slab, c)
    cums = [_split_dot(tri, logd[r0:r0 + slab]) for r0 in range(0, rows, slab)]
    cum = cums[0] if len(cums) == 1 else jnp.concatenate(cums, axis=0)
    tick()
    tots, tot_rows = _last_rows(cum, bb * nsub, c)
    p_inv = jnp.exp(-cum)
    p_end = jnp.exp(tot_rows - cum)
    at = (-kk * jnp.exp(cum - logd)).astype(BF16)
    rt = (r * jnp.exp(cum)).astype(BF16)
    bt = (kka * p_inv).astype(BF16)
    kt = (k2 * p_inv).astype(BF16)
    be = (kka * p_end).astype(BF16)
    ke = (k2 * p_end).astype(BF16)
    vb = v.astype(BF16)

    pw = 2 * hd
    iota = lambda shape, dim: lax.broadcasted_iota(jnp.int32, shape, dim)
    first_c = iota((c, 2 * c), 1) < c
    first_v = iota((1, pw), 1) < hd
    bd_cc = (iota((2 * c, 2 * c), 0) < c) == (iota((2 * c, 2 * c), 1) < c)
    bd_cv = (iota((2 * c, pw), 0) < c) == (iota((2 * c, pw), 1) < hd)
    bd_vv = (iota((pw, pw), 0) < hd) == (iota((pw, pw), 1) < hd)
    bd_cv4 = (iota((4 * c, pw), 0) < 2 * c) == (iota((4 * c, pw), 1) < hd)
    outer4 = (iota((4 * c, 1), 0) < c) | (iota((4 * c, 1), 0) >= 3 * c)
    strict2 = _tri(c, strict=True, reps=2)
    incl4 = _tri(c, reps=4)
    eye2 = ((iota((c, 2 * c), 1) & (c - 1)) == iota((c, 2 * c), 0)).astype(F32)
    stack2 = lambda m: jnp.concatenate([m, m], axis=0)
    bd2 = lambda m: jnp.where(bd_cv, stack2(m), 0)

    units = [(b, ci, j) for b in range(bb) for ci in range(nsub) for j in range(heads // 2)]
    idx = [(slice(b * cs + ci * c, b * cs + (ci + 1) * c), slice(j * pw, (j + 1) * pw)) for b, ci, j in units]
    xps = [jnp.concatenate([at[rs, sl], rt[rs, sl]], axis=0) for rs, sl in idx]
    ybds = [jnp.concatenate([jnp.where(first_v, jnp.concatenate([bt[rs, sl], kt[rs, sl]], axis=0), 0),
                             jnp.where(first_v, 0, jnp.concatenate([kt[rs, sl], bt[rs, sl]], axis=0))], axis=0)
            for rs, sl in idx]
    eps = [jnp.concatenate([be[rs, sl], ke[rs, sl]], axis=0) for rs, sl in idx]
    vps = [vb[rs, sl] for rs, sl in idx]
    tick()
    gps = [_bdot_nt(xp, ybd) for xp, ybd in zip(xps, ybds)]
    tick()
    a_ps = [jnp.where(strict2, jnp.where(first_c, gp[:c, 0:2 * c], gp[:c, 2 * c:4 * c]), 0.0) for gp in gps]
    k_ps = [jnp.where(strict2, jnp.where(first_c, gp[:c, 2 * c:4 * c], gp[:c, 0:2 * c]), 0.0) for gp in gps]
    akvs = [_bdot(k_p, jnp.where(bd_cv, 0, stack2(vp))) for k_p, vp in zip(k_ps, vps)]
    tick()
    ns = a_ps
    ts = [eye2 + n for n in ns]
    step = 2
    while step < c:
        nbds = [jnp.where(bd_cc, stack2(n.astype(BF16)), 0) for n in ns]
        ns = [_bdot(n, nbd) for n, nbd in zip(ns, nbds)]
        tick()
        ts = [t + _bdot(t, jnp.where(bd_cc, stack2(n.astype(BF16)), 0)) for t, n in zip(ts, ns)]
        tick()
        step *= 2
    tas = [_bdot(t, jnp.concatenate([bd2(xp[:c]), bd2(akv.astype(BF16))], axis=1))
           for t, xp, akv in zip(ts, xps, akvs)]
    tick()
    tabs = [(ta[:, :pw].astype(BF16), ta[:, pw:].astype(BF16)) for ta in tas]
    qos = [_bdot(jnp.where(incl4, gp[c:, :], 0.0),
                 jnp.concatenate([jnp.where(bd_cv4 & outer4, jnp.concatenate([ta, ta, ta, ta], axis=0), 0),
                                  jnp.where(bd_cv4, jnp.concatenate([tkv, vp, vp, tkv], axis=0), 0)], axis=1))
           for gp, (ta, tkv), vp in zip(gps, tabs, vps)]
    tick()
    qs = [(xp[c:].astype(F32) + qo[:, :pw]).astype(BF16) for xp, qo in zip(xps, qos)]
    mds = [_bdot_tn(jnp.concatenate([jnp.concatenate([ta, tkv], axis=1),
                                     jnp.concatenate([jnp.zeros_like(vp), vp], axis=1)], axis=0), ep)
           for (ta, tkv), vp, ep in zip(tabs, vps, eps)]
    tick()
    bd_vv2 = stack2(bd_vv)
    mds = [jnp.where(bd_vv2, md, 0.0) for md in mds]

    for b in range(bb):
        sbds = [sbd_ref[b, j] for j in range(heads // 2)]
        for ci in range(nsub):
            for j in range(heads // 2):
                i = units.index((b, ci, j))
                rs, sl = idx[i]
                sbd = sbds[j]
                o_ref[b, ci * c:(ci + 1) * c, sl] = _bdot_nt(qs[i], sbd) + qos[i][:, pw:]
                sbds[j] = sbd * jnp.exp(tots[b * nsub + ci][:, sl]) + _bdot(sbd, mds[i][:pw]) + mds[i][pw:]
            tick()
        for j in range(heads // 2):
            sbd_ref[b, j] = sbds[j]

    @pl.when(pl.program_id(1) == pl.num_programs(1) - 1)
    def _():
        for b in range(bb):
            for j in range(heads // 2):
                s_ref[b, 2 * j] = sbd_ref[b, j, 0:hd, 0:hd]
                s_ref[b, 2 * j + 1] = sbd_ref[b, j, hd:pw, hd:pw]

    o = o_ref[...].reshape(rows, wa)
    dev = o - _group_sums(o, hd) * (1.0 / hd)
    var = _group_sums(dev * dev, hd) * (1.0 / hd)
    o = dev * lax.rsqrt(var + LNX_EPS) * lw_ref[...] + lb_ref[...]
    tick()
    bonus = _group_sums(r * k2 * rk_ref[...], hd) * v
    o_ref[...] = ((o + bonus) * g).reshape(bb, cs, wa)


RWKV_POINTS = 24


def _rwkv_call(x, nw, sc, sh, w_in, shift_prev, s0, prm):
    b, t, _ = x.shape
    cols = w_in.shape[1]
    heads, hd = s0.shape[1], s0.shape[2]
    wa = heads * hd
    lora = (prm["w2"].shape[0], prm["a2"].shape[0], prm["g2"].shape[0])
    vec = lambda a: a.reshape(1, -1)
    consts = [vec(prm["mu"]), vec(prm["w0"]), prm["w2"], vec(prm["a0"]), prm["a2"], prm["g2"],
              vec(prm["k_k"]), vec(prm["k_a"]), vec(prm["r_k"]), vec(prm["lnx_w"]), vec(prm["lnx_b"])]
    seq = lambda bb, shape: pl.BlockSpec((bb,) + shape, lambda i, j: (i,) + (0,) * len(shape))
    return _fused_call(
        "rwkv7_mixer", _rwkv_init, functools.partial(_rwkv_core, heads=heads, lora=lora), RWKV_POINTS,
        x, nw, sc, sh, w_in, RWKV_CHUNK * RWKV_GROUP,
        ins=[shift_prev.reshape(b, 1, cols), s0, *consts],
        in_specs=lambda bb, c: [seq(bb, (1, cols)), seq(bb, s0.shape[1:])] + [_const_spec(a.shape) for a in consts],
        out_shapes=[jax.ShapeDtypeStruct((b, t, wa), F32),
                    jax.ShapeDtypeStruct((b, 1, cols), F32),
                    jax.ShapeDtypeStruct(s0.shape, F32)],
        out_specs=lambda bb, c: [pl.BlockSpec((bb, c, wa), lambda i, j: (i, j, 0)),
                                 seq(bb, (1, cols)), seq(bb, s0.shape[1:])],
        scratch=lambda bb, c: [pltpu.VMEM((bb, 1, cols), F32),
                               pltpu.VMEM((bb, heads // 2, 2 * hd, 2 * hd), F32)])


def _mamba_init(ins, outs, scr, *, conv_w):
    conv_ref, s0_ref = ins[:2]
    s_ref = outs[2]
    ubuf_ref = scr[0]
    bb, _, xbc = conv_ref.shape
    pad, hist = 8, conv_w - 1
    s_ref[...] = s0_ref[...]
    for b in range(bb):
        ubuf_ref[b, 0:pad, :] = jnp.zeros((pad, xbc), F32)
        ubuf_ref[b, pad - hist:pad, :] = conv_ref[b]


def _mamba_core(p, ins, outs, scr, tick, *, groups, conv_w):
    conv_ref, s0_ref, cw_ref, cb_ref, dtb_ref, alog_ref, dskip_ref, nw_ref = ins
    o_ref, conv_out_ref, s_ref = outs
    ubuf_ref, inter_ref = scr
    bb, c, wb = o_ref.shape
    rows = bb * c
    heads, hd, ns = s_ref.shape[1], s_ref.shape[2], s_ref.shape[3]
    xbc = conv_ref.shape[-1]
    hpg = heads // groups
    pad = 8
    hist = conv_w - 1

    ys = []
    for b in range(bb):
        u = p[b * c:(b + 1) * c, wb:wb + xbc]
        ext = jnp.concatenate([ubuf_ref[b], u], axis=0)
        y = cb_ref[...] + cw_ref[hist:hist + 1, :] * u
        for i in range(hist):
            y = y + cw_ref[i:i + 1, :] * pltpu.roll(ext, hist - i, axis=0)[pad:pad + c, :]
        conv_out_ref[b] = u[c - hist:c, :]
        ubuf_ref[b] = u[c - pad:c, :]
        ys.append(y)
        tick()
    xc = _silu(ys[0] if bb == 1 else jnp.concatenate(ys, axis=0))
    xs = xc[:, 0:wb]
    bm = xc[:, wb:wb + groups * ns]
    cm = xc[:, wb + groups * ns:wb + 2 * groups * ns]
    z = p[:, 0:wb]
    dt_raw = p[:, wb + xbc:wb + xbc + heads]
    dt = jax.nn.softplus(dt_raw + dtb_ref[...])
    la = dt * (-jnp.exp(alog_ref[...]))

    incl = _tri(c)
    acum = _split_dot(_block_tri(rows, c), la)
    acum_t = _split_dot_tn(la, _block_tri(rows, c, upper=True))
    a_lasts, a_last_rows = _last_rows(acum, bb, c)
    e_tots = [jnp.exp(al) for al in a_lasts]
    stack = jnp.concatenate([dt, jnp.exp(acum), jnp.exp(a_last_rows - acum),
                             jnp.broadcast_to(dskip_ref[...], (8, heads))], axis=0)
    full = _expand_cols(stack, hd)
    ecum_f = full[rows:2 * rows]
    dskip_f = full[3 * rows:3 * rows + 1]
    xd_f = xs * full[0:rows]
    xde_f = xd_f * full[2 * rows:3 * rows]
    colb = _expand_cols(acum, c)
    tick()

    rss = [slice(b * c, (b + 1) * c) for b in range(bb)]
    bgs = [[bm[rs, gi * ns:(gi + 1) * ns].astype(BF16) for gi in range(groups)] for rs in rss]
    cgs = [[cm[rs, gi * ns:(gi + 1) * ns].astype(BF16) for gi in range(groups)] for rs in rss]
    cbs = [[_bdot_nt(cg, bg) for cg, bg in zip(cgb, bgb)] for cgb, bgb in zip(cgs, bgs)]
    tick()
    pairs = [(b, h) for b in range(bb) for h in range(heads)]
    sls = [slice(h * hd, (h + 1) * hd) for _, h in pairs]
    s0s = [s_ref[b, h] for b, h in pairs]
    mats = []
    for b, h in pairs:
        seg = colb[rss[b], h * c:(h + 1) * c] - acum_t[h:h + 1, rss[b]]
        decay = jnp.where(incl, jnp.exp(jnp.where(incl, seg, 0.0)), 0.0)
        mats.append((cbs[b][h // hpg] * decay).astype(BF16))
        if h % 4 == 3:
            tick()
    intra = [_bdot(m, xd_f[rss[b], sl]) for m, (b, _), sl in zip(mats, pairs, sls)]
    tick()
    inter = [_bdot_nt(cgs[b][h // hpg], s0) for (b, h), s0 in zip(pairs, s0s)]
    tick()
    upd = [_bdot_tn(xde_f[rss[b], sl], bgs[b][h // hpg]) for (b, h), sl in zip(pairs, sls)]
    tick()
    for i, ((b, h), sl) in enumerate(zip(pairs, sls)):
        s_ref[b, h] = s0s[i] * e_tots[b][:, h:h + 1] + upd[i]
        o_ref[b, :, sl] = intra[i]
        inter_ref[b, :, sl] = inter[i]

    y_all = o_ref[...].reshape(rows, wb) + inter_ref[...].reshape(rows, wb) * ecum_f + dskip_f * xs
    yv = y_all * _silu(z)
    gw = wb // groups
    for gi in range(groups):
        sl = slice(gi * gw, (gi + 1) * gw)
        yg = yv[:, sl]
        ms = jnp.mean(yg * yg, axis=-1, keepdims=True)
        o_ref[:, :, sl] = (yg * lax.rsqrt(ms + GROUP_RMS_EPS) * nw_ref[:, sl]).reshape(bb, c, gw)


MAMBA_POINTS = 16


def _mamba_call(x, nw, sc, sh, w_in, conv_prev, s0, prm, groups):
    b, t, _ = x.shape
    heads, hd, ns = s0.shape[1:]
    wb = heads * hd
    hist, xbc = conv_prev.shape[1:]
    vec = lambda a: a.reshape(1, -1)
    consts = [prm["conv_w"], vec(prm["conv_b"]), vec(prm["dt_bias"]), vec(prm["a_log"]),
              vec(prm["d_skip"]), vec(prm["norm_b_w"])]
    seq = lambda bb, shape: pl.BlockSpec((bb,) + shape, lambda i, j: (i,) + (0,) * len(shape))
    return _fused_call(
        "mamba2_mixer", functools.partial(_mamba_init, conv_w=hist + 1),
        functools.partial(_mamba_core, groups=groups, conv_w=hist + 1), MAMBA_POINTS,
        x, nw, sc, sh, w_in, MAMBA_CHUNK,
        ins=[conv_prev, s0, *consts],
        in_specs=lambda bb, c: [seq(bb, (hist, xbc)), seq(bb, s0.shape[1:])] + [_const_spec(a.shape) for a in consts],
        out_shapes=[jax.ShapeDtypeStruct((b, t, wb), F32),
                    jax.ShapeDtypeStruct(conv_prev.shape, F32),
                    jax.ShapeDtypeStruct(s0.shape, F32)],
        out_specs=lambda bb, c: [pl.BlockSpec((bb, c, wb), lambda i, j: (i, j, 0)),
                                 seq(bb, (hist, xbc)), seq(bb, s0.shape[1:])],
        scratch=lambda bb, c: [pltpu.VMEM((bb, 8, xbc), F32), pltpu.VMEM((bb, c, wb), F32)])


HGRN_FINE = 4


def _hgrn_sum_matrix(c):
    t = np.arange(c)[:, None]
    j = np.arange(c)[None, :]
    blocks = [(j <= t)]
    m = min(HGRN_FINE, c // 2)
    while m >= 1:
        mid = (t // (2 * m)) * (2 * m) + m - 1
        right = (t % (2 * m)) >= m
        blocks.append(np.where(right, (j > mid) & (j <= t), (j > t) & (j <= mid)))
        m //= 2
    return np.concatenate(blocks, axis=0).astype(np.float32)


def _hgrn_init(ins, outs, scr):
    s0_ref = ins[0]
    (st_ref,) = scr
    bb, heads = s0_ref.shape[:2]
    for b in range(bb):
        for h in range(heads):
            st_ref[b, h] = s0_ref[b, h].T


def _hgrn_core(p, ins, outs, scr, tick, *, layer):
    s0_ref, sums_ref, lbp_ref, nw_ref = ins
    o_ref, s_ref = outs
    (st_ref,) = scr
    bb, cs, wc = o_ref.shape
    c = min(HGRN_CHUNK, cs)
    nsub = cs // c
    rows, cols = p.shape
    heads, dk, dv = s0_ref.shape[1:]
    nchunks = pl.num_programs(1)

    lbp = lbp_ref[...]
    e = jnp.exp(lbp - jnp.max(lbp, axis=0, keepdims=True))
    soft = e / jnp.sum(e, axis=0, keepdims=True)
    lb = soft[0:1, :]
    for i in range(1, layer + 1):
        lb = lb + soft[i:i + 1, :]
    lb = lb - soft[0:1, :]

    q = _silu(p[:, 0:wc])
    f = p[:, wc:2 * wc]
    v = p[:, 2 * wc:3 * wc]
    gate = p[:, 3 * wc:4 * wc]
    log_f = jnp.log(lb + (1.0 - lb) * jax.nn.sigmoid(f))
    k = (1.0 - lb) * jax.nn.sigmoid(-f)
    qk = q * k

    row = lax.broadcasted_iota(jnp.int32, (c, 1), 0)
    rr = lax.broadcasted_iota(jnp.int32, (c, c), 0)
    cc = lax.broadcasted_iota(jnp.int32, (c, c), 1)
    eye = rr == cc
    slabs = [(b, ci) for b in range(bb) for ci in range(nsub)]
    rss = [slice(b * cs + ci * c, b * cs + (ci + 1) * c) for b, ci in slabs]
    qes, kes, f_tots, levels = [], [], [], []
    for rs in rss:
        sums = _split_dot(sums_ref[...], log_f[rs])
        bcum = sums[0:c]
        tot = bcum[c - 1:c, :]
        tick()
        qes.append((q[rs] * jnp.exp(bcum)).astype(BF16))
        kes.append((k[rs] * jnp.exp(tot - bcum)).astype(BF16))
        f_tots.append(jnp.exp(tot))
        lv = []
        m = c // 2
        i = 1
        while m >= 1:
            if m > HGRN_FINE:
                mids = [jnp.broadcast_to(bcum[r0 + m - 1:r0 + m, :], (2 * m, wc)) for r0 in range(0, c, 2 * m)]
                diff = bcum - (mids[0] if len(mids) == 1 else jnp.concatenate(mids, axis=0))
                z = jnp.where((row & (2 * m - 1)) >= m, diff, -diff)
            else:
                z = sums[i * c:(i + 1) * c]
                i += 1
            wgt = jnp.exp(z)
            lv.append(((q[rs] * wgt).astype(BF16), (k[rs] * wgt).astype(BF16),
                       (rr > cc) & ((rr ^ cc) >= m) & ((rr ^ cc) < 2 * m)))
            m //= 2
            tick()
        levels.append(lv)

    units = [(si, h) for si in range(len(slabs)) for h in range(heads)]
    sls = [slice(h * dk, (h + 1) * dk) for _, h in units]
    atts = [jnp.where(eye, jnp.sum(qk[rss[si], sl], axis=-1, keepdims=True), 0.0) for (si, _), sl in zip(units, sls)]
    for li in range(len(levels[0])):
        atts = [att + jnp.where(levels[si][li][2], _bdot_nt(levels[si][li][0][:, sl], levels[si][li][1][:, sl]), 0.0)
                for att, (si, _), sl in zip(atts, units, sls)]
        tick()
    vhs = [v[rss[si], sl].astype(BF16) for (si, _), sl in zip(units, sls)]
    intra = [_bdot(att, vh) for att, vh in zip(atts, vhs)]
    tick()
    upd = [_bdot_tn(vh, kes[si][:, sl]) for vh, (si, _), sl in zip(vhs, units, sls)]
    tick()

    for b in range(bb):
        sts = [st_ref[b, h] for h in range(heads)]
        for ci in range(nsub):
            si = b * nsub + ci
            for h in range(heads):
                i = si * heads + h
                sl = sls[i]
                o = intra[i] + _bdot_nt(qes[si][:, sl], sts[h])
                sts[h] = sts[h] * f_tots[si][:, sl] + upd[i]
                ms = jnp.mean(o * o, axis=-1, keepdims=True)
                o_ref[b, ci * c:(ci + 1) * c, sl] = (o * lax.rsqrt(ms + GROUP_RMS_EPS) * nw_ref[:, sl]
                                                     * _silu(gate[rss[si], sl]))
            tick()
        for h in range(heads):
            st_ref[b, h] = sts[h]

    @pl.when(pl.program_id(1) == nchunks - 1)
    def _():
        for b in range(bb):
            for h in range(heads):
                s_ref[b, h] = st_ref[b, h].T


HGRN_POINTS = 40


def _hgrn_call(x, nw, sc, sh, w_in, s0, lb_param, norm_w, layer):
    b, t, _ = x.shape
    heads, dk, dv = s0.shape[1:]
    assert dk == dv
    wc = heads * dk
    sums = jnp.asarray(_hgrn_sum_matrix(math.gcd(HGRN_CHUNK, t)), dtype=BF16)
    ins = [s0, sums, lb_param, norm_w.reshape(1, wc)]
    seq = lambda bb, shape: pl.BlockSpec((bb,) + shape, lambda i, j: (i,) + (0,) * len(shape))
    return _fused_call(
        "hgrn2_mixer", _hgrn_init, functools.partial(_hgrn_core, layer=layer), HGRN_POINTS,
        x, nw, sc, sh, w_in, HGRN_CHUNK * HGRN_GROUP,
        ins=ins,
        in_specs=lambda bb, c: [seq(bb, s0.shape[1:])] + [_const_spec(a.shape) for a in ins[1:]],
        out_shapes=[jax.ShapeDtypeStruct((b, t, wc), F32), jax.ShapeDtypeStruct(s0.shape, F32)],
        out_specs=lambda bb, c: [pl.BlockSpec((bb, c, wc), lambda i, j: (i, j, 0)), seq(bb, s0.shape[1:])],
        scratch=lambda bb, c: [pltpu.VMEM((bb, heads, dv, dk), F32)])


def _round_up(n, m):
    return -(-n // m) * m


def _prepare_weights(w):
    a_cols = w["mu_a"].shape[1]
    b_cols = w["w_in_ab"].shape[2] - a_cols
    b_pad = _round_up(b_cols, LANE) - b_cols
    return dict(
        w_in_a=w["w_in_ab"][:, :, :a_cols].astype(BF16),
        w_in_b=jnp.pad(w["w_in_ab"][:, :, a_cols:], ((0, 0), (0, 0), (0, b_pad))).astype(BF16),
        w_out_ab=w["w_out_ab"].astype(BF16),
        w_in_c=w["w_in_c"].astype(BF16),
        w_out_c=w["w_out_c"].astype(BF16),
        w_gate=w["w_gate"].astype(BF16),
        w_up=w["w_up"].astype(BF16),
        w_down=w["w_down"].astype(BF16),
    )


def _trunk(x, mod, st_rwkv, st_shift, st_ssm, st_conv, st_hgrn, w, wb):
    depth = mod.shape[0]
    b, t, d = x.shape
    groups = (st_conv.shape[-1] - st_ssm.shape[2] * st_ssm.shape[3]) // (2 * st_ssm.shape[4])
    new_rwkv, new_shift, new_ssm, new_conv, new_hgrn = [], [], [], [], []
    for layer in range(depth):
        j = layer // 2
        sh_m, sc_m, g_m, sh_f, sc_f, g_f = (mod[layer, :, None, i * d:(i + 1) * d] for i in range(6))
        nw_mix = w["norm_mix_w"][layer].reshape(1, d)
        nw_ffn = w["norm_ffn_w"][layer].reshape(1, d)
        if layer % 2 == 0:
            prm = dict(mu=w["mu_a"][j], w0=w["w0"][j], w2=w["w2"][j], a0=w["a0"][j], a2=w["a2"][j],
                       g2=w["g2"][j], k_k=w["k_k"][j], k_a=w["k_a"][j], r_k=w["r_k"][j],
                       lnx_w=w["lnx_w"][j], lnx_b=w["lnx_b"][j])
            oa, shift_new, rwkv_new = _rwkv_call(x, nw_mix, sc_m, sh_m, wb["w_in_a"][j],
                                                 st_shift[:, j], st_rwkv[:, j], prm)
            prm_b = dict(conv_w=w["conv_w"][j], conv_b=w["conv_b"][j], dt_bias=w["dt_bias"][j],
                         a_log=w["a_log"][j], d_skip=w["d_skip"][j], norm_b_w=w["norm_b_w"][j])
            ob, conv_new, ssm_new = _mamba_call(x, nw_mix, sc_m, sh_m, wb["w_in_b"][j],
                                                st_conv[:, j], st_ssm[:, j], prm_b, groups)
            mixes, wout = [oa, ob], wb["w_out_ab"][j]
            new_rwkv.append(rwkv_new)
            new_shift.append(shift_new[:, 0])
            new_ssm.append(ssm_new)
            new_conv.append(conv_new)
        else:
            oc, hgrn_new = _hgrn_call(x, nw_mix, sc_m, sh_m, wb["w_in_c"][j],
                                      st_hgrn[:, j], w["lb_param"], w["norm_c_w"][j], j)
            mixes, wout = [oc], wb["w_out_c"][j]
            new_hgrn.append(hgrn_new)
        final_w = w["norm_out_w"].reshape(1, d) if layer == depth - 1 else None
        x = _post_call(x, mixes, wout, g_m, nw_ffn, sc_f, sh_f, g_f,
                       wb["w_gate"][layer], wb["w_up"][layer], wb["w_down"][layer], final_w)
    return (x, jnp.stack(new_rwkv, axis=1), jnp.stack(new_shift, axis=1), jnp.stack(new_ssm, axis=1),
            jnp.stack(new_conv, axis=1), jnp.stack(new_hgrn, axis=1))


def _run(x_prompt, x_sample, state_rwkv, state_rwkv_shift, state_ssm, state_conv, state_hgrn,
         c_prompt, c_sample, w):
    bp, bs = x_prompt.shape[0], x_sample.shape[0]
    rows = _round_up(bp + bs, 8)
    c_all = jnp.pad(jnp.concatenate([c_prompt, c_sample], axis=0), ((0, rows - bp - bs), (0, 0)))
    mod = _ada_call(c_all, w["ada_w"], w["ada_b"])
    wb = _prepare_weights(w)
    zeros = lambda s: jnp.zeros((bp,) + s.shape[1:], F32)
    outs_p = _trunk(x_prompt, mod[:, :bp], zeros(state_rwkv), zeros(state_rwkv_shift), zeros(state_ssm),
                    zeros(state_conv), zeros(state_hgrn), w, wb)
    outs_s = _trunk(x_sample, mod[:, bp:bp + bs], state_rwkv, state_rwkv_shift, state_ssm,
                    state_conv, state_hgrn, w, wb)
    return (outs_p[0], outs_s[0]) + outs_p[1:] + outs_s[1:]


def kernel(x_prompt, x_sample, state_rwkv, state_rwkv_shift, state_ssm, state_conv, state_hgrn, c_prompt, c_sample, norm_mix_w, norm_ffn_w, norm_out_w, ada_w, ada_b, w_in_ab, w_out_ab, mu_a, w0, w2, a0, a2, g2, k_k, k_a, r_k, lnx_w, lnx_b, conv_w, conv_b, dt_bias, a_log, d_skip, norm_b_w, w_in_c, w_out_c, lb_param, norm_c_w, w_gate, w_up, w_down):
    w = dict(norm_mix_w=norm_mix_w, norm_ffn_w=norm_ffn_w, norm_out_w=norm_out_w, ada_w=ada_w, ada_b=ada_b,
             w_in_ab=w_in_ab, w_out_ab=w_out_ab, mu_a=mu_a, w0=w0, w2=w2, a0=a0, a2=a2, g2=g2, k_k=k_k,
             k_a=k_a, r_k=r_k, lnx_w=lnx_w, lnx_b=lnx_b, conv_w=conv_w, conv_b=conv_b, dt_bias=dt_bias,
             a_log=a_log, d_skip=d_skip, norm_b_w=norm_b_w, w_in_c=w_in_c, w_out_c=w_out_c,
             lb_param=lb_param, norm_c_w=norm_c_w, w_gate=w_gate, w_up=w_up, w_down=w_down)
    return _run(x_prompt, x_sample, state_rwkv, state_rwkv_shift, state_ssm, state_conv, state_hgrn,
                c_prompt, c_sample, w)
```

```python
import functools
import math

import numpy as np
import jax
import jax.numpy as jnp
from jax import lax
from jax.experimental import pallas as pl
from jax.experimental.pallas import tpu as pltpu

F32 = jnp.float32
BF16 = jnp.bfloat16

MXU_TILE = 256
VMEM_LIMIT = 56 * 2**20
ROW_TILE = 512
COL_CHUNK = 512
RWKV_CHUNK = 64
RWKV_GROUP = 4
MAMBA_CHUNK = 128
HGRN_CHUNK = 64
HGRN_GROUP = 2
MIX_SEQS = 4
LNX_EPS = 64e-5
RMS_EPS = 1e-6
GROUP_RMS_EPS = 1e-5


def _bdot(a, b):
    return jnp.dot(a.astype(BF16), b.astype(BF16), preferred_element_type=F32)


def _bdot_nt(a, b):
    return lax.dot_general(a.astype(BF16), b.astype(BF16), (((1,), (1,)), ((), ())),
                           preferred_element_type=F32)


def _bdot_tn(a, b):
    return lax.dot_general(a.astype(BF16), b.astype(BF16), (((0,), (0,)), ((), ())),
                           preferred_element_type=F32)


def _silu(x):
    return x * jax.nn.sigmoid(x)


def _const_spec(shape, single_buffer=False):
    nd = len(shape)
    if single_buffer:
        return pl.BlockSpec(shape, lambda *_: (0,) * nd, pipeline_mode=pl.Buffered(1))
    return pl.BlockSpec(shape, lambda *_: (0,) * nd)


def _layer_spec(stacked, layer):
    nd = stacked.ndim
    return pl.BlockSpec((None,) + stacked.shape[1:], lambda *_: (layer,) + (0,) * (nd - 1),
                        pipeline_mode=pl.Buffered(1))


def _params(n_axes):
    return pltpu.CompilerParams(dimension_semantics=("arbitrary",) * n_axes,
                                vmem_limit_bytes=VMEM_LIMIT)


def _row_tiling(b, t):
    if t >= ROW_TILE:
        assert t % ROW_TILE == 0
        return 1, ROW_TILE
    bb = max(1, min(b, ROW_TILE // t))
    while b % bb:
        bb -= 1
    return bb, t


def _col_chunks(n):
    return [(n0, min(n0 + COL_CHUNK, n)) for n0 in range(0, n, COL_CHUNK)]


def _ada_body(c_ref, w_ref, b_ref, o_ref):
    o_ref[0] = _bdot(_silu(c_ref[...]), w_ref[0]) + b_ref[0]


def _ada_call(c_all, ada_w, ada_b):
    depth, d, n = ada_w.shape
    r = c_all.shape[0]
    tn = 1024
    assert n % tn == 0
    return pl.pallas_call(
        _ada_body,
        out_shape=jax.ShapeDtypeStruct((depth, r, n), F32),
        grid=(depth, n // tn),
        in_specs=[pl.BlockSpec((r, d), lambda l, j: (0, 0)),
                  pl.BlockSpec((1, d, tn), lambda l, j: (l, 0, j)),
                  pl.BlockSpec((1, 1, tn), lambda l, j: (l, 0, j))],
        out_specs=pl.BlockSpec((1, r, tn), lambda l, j: (l, 0, j)),
        compiler_params=_params(2),
        name="ada_mod",
    )(c_all, ada_w, ada_b.reshape(depth, 1, n))


def _norm_mod(x, nw, sc, sh):
    ms = jnp.mean(x * x, axis=-1, keepdims=True)
    y = x * lax.rsqrt(ms + RMS_EPS) * nw
    return y * (1.0 + sc) + sh


class _Ticker:
    def __init__(self, thunks, points):
        self.thunks, self.total, self.points, self.calls = list(thunks), len(thunks), points, 0

    def __call__(self):
        self.calls += 1
        due = min(self.total, -(-self.calls * self.total // self.points))
        while self.total - len(self.thunks) < due:
            self.thunks.pop(0)()

    def flush(self):
        while self.thunks:
            self.thunks.pop(0)()


def _fused_body(*refs, init, core, n_in, n_out, points, n_steps, col0, cols):
    x_cur_ref, x_next_ref, nw_ref, sc_ref, sh_ref, w_ref = refs[:6]
    ins = refs[6:6 + n_in]
    outs = refs[6 + n_in:6 + n_in + n_out]
    scr = refs[6 + n_in + n_out:-2]
    bufs = refs[-2:]
    bb, c, d = x_cur_ref.shape
    rows = bb * c
    j = pl.program_id(1)

    def projection(x_ref, dst_ref):
        h = _norm_mod(x_ref[...], nw_ref[...], sc_ref[...], sh_ref[...])
        hb = h.reshape(rows, d).astype(BF16)

        def piece(n0, n1):
            def run():
                dst_ref[:, n0:n1] = jnp.dot(hb, w_ref[:, col0 + n0:col0 + n1], preferred_element_type=F32)
            return run
        return [piece(n0, min(n0 + MXU_TILE, cols)) for n0 in range(0, cols, MXU_TILE)]

    @pl.when(j == 0)
    def _():
        init(ins, outs, scr)
        for run in projection(x_cur_ref, bufs[0]):
            run()

    for parity in range(2):
        @pl.when(lax.rem(j, 2) == parity)
        def _(parity=parity):
            ahead = projection(x_next_ref, bufs[1 - parity]) if n_steps > 1 else []
            tick = _Ticker(ahead, points)
            core(bufs[parity][...], ins, outs, scr, tick)
            tick.flush()


def _fused_call(name, init, core, points, x, nw, sc, sh, w_in, chunk, ins, in_specs, out_shapes, out_specs,
                scratch):
    b, t, d = x.shape
    bb, c = _mixer_tiling(b, t, chunk)
    n = t // c
    w_stack, layer, col0, cols = w_in
    seq_spec = pl.BlockSpec((bb, 1, d), lambda i, j: (i, 0, 0))
    return pl.pallas_call(
        functools.partial(_fused_body, init=init, core=core, n_in=len(ins), n_out=len(out_shapes), points=points,
                          n_steps=n, col0=col0, cols=cols),
        out_shape=out_shapes,
        grid=(b // bb, n),
        in_specs=[pl.BlockSpec((bb, c, d), lambda i, j: (i, j, 0)),
                  pl.BlockSpec((bb, c, d), lambda i, j: (i, jnp.minimum(j + 1, n - 1), 0)),
                  _const_spec((1, d)), seq_spec, seq_spec, _layer_spec(w_stack, layer)]
                 + in_specs(bb, c),
        out_specs=out_specs(bb, c),
        scratch_shapes=scratch(bb, c) + [pltpu.VMEM((bb * c, cols), F32)] * 2,
        compiler_params=_params(2),
        name=name,
    )(x, x, nw, sc, sh, w_stack, *ins)


def _post_body(*refs, n_mix, final):
    x_ref = refs[0]
    mix_refs = refs[1:1 + n_mix]
    (wout_ref, gm_ref, nw_ref, sc_ref, sh_ref, gf_ref, wg_ref, wu_ref, wd_ref) = refs[1 + n_mix:10 + n_mix]
    rest = refs[10 + n_mix:]
    if final:
        fw_ref, o_ref, act_ref = rest
    else:
        o_ref, act_ref = rest
    bb, tt, d = x_ref.shape
    rows = bb * tt

    mix = None
    off = 0
    for m_ref in mix_refs:
        wdt = m_ref.shape[-1]
        part = jnp.dot(m_ref[...].reshape(rows, wdt).astype(BF16), wout_ref[off:off + wdt, :],
                       preferred_element_type=F32)
        mix = part if mix is None else mix + part
        off += wdt
    x1 = x_ref[...] + gm_ref[...] * mix.reshape(bb, tt, d)

    h = _norm_mod(x1, nw_ref[...], sc_ref[...], sh_ref[...])
    hb = h.reshape(rows, d).astype(BF16)
    for f0, f1 in _col_chunks(wg_ref.shape[1]):
        gate = jnp.dot(hb, wg_ref[:, f0:f1], preferred_element_type=F32)
        up = jnp.dot(hb, wu_ref[:, f0:f1], preferred_element_type=F32)
        act_ref[:, f0:f1] = (_silu(gate) * up).astype(BF16)
    ffn = jnp.dot(act_ref[...], wd_ref[...], preferred_element_type=F32)
    x2 = x1 + gf_ref[...] * ffn.reshape(bb, tt, d)
    if final:
        ms = jnp.mean(x2 * x2, axis=-1, keepdims=True)
        x2 = x2 * lax.rsqrt(ms + RMS_EPS) * fw_ref[...]
    o_ref[...] = x2


def _post_call(x, mixes, wout, gm, nw, sc, sh, gf, wg, wu, wd, final_w):
    b, t, d = x.shape
    bb, tt = _row_tiling(b, t)
    final = final_w is not None
    row_spec = lambda w: pl.BlockSpec((bb, tt, w), lambda i, j: (i, j, 0))
    seq_spec = pl.BlockSpec((bb, 1, d), lambda i, j: (i, 0, 0))
    in_specs = ([row_spec(d)] + [row_spec(m.shape[-1]) for m in mixes]
                + [_layer_spec(*wout), seq_spec, _const_spec((1, d)), seq_spec, seq_spec, seq_spec,
                   _layer_spec(*wg), _layer_spec(*wu), _layer_spec(*wd)])
    args = [x, *mixes, wout[0], gm, nw, sc, sh, gf, wg[0], wu[0], wd[0]]
    if final:
        in_specs.append(_const_spec((1, d)))
        args.append(final_w)
    return pl.pallas_call(
        functools.partial(_post_body, n_mix=len(mixes), final=final),
        out_shape=jax.ShapeDtypeStruct((b, t, d), F32),
        grid=(b // bb, t // tt),
        in_specs=in_specs,
        out_specs=row_spec(d),
        scratch_shapes=[pltpu.VMEM((bb * tt, wg[0].shape[2]), BF16)],
        compiler_params=_params(2),
        name="post_ffn",
    )(*args)


def _tri(c, strict=False, reps=1):
    row = lax.broadcasted_iota(jnp.int32, (c, reps * c), 0)
    col = lax.broadcasted_iota(jnp.int32, (c, reps * c), 1) & (c - 1)
    return (row > col) if strict else (row >= col)


def _block_tri(rows, c, upper=False):
    r = lax.broadcasted_iota(jnp.int32, (rows, rows), 0)
    q = lax.broadcasted_iota(jnp.int32, (rows, rows), 1)
    tri = (r <= q) if upper else (r >= q)
    return (tri & ((r ^ q) < c)).astype(BF16)


def _split3(x):
    x1 = x.astype(BF16)
    r1 = x - x1.astype(F32)
    x2 = r1.astype(BF16)
    x3 = (r1 - x2.astype(F32)).astype(BF16)
    return x1, x2, x3


def _split_dot(w01, x):
    x1, x2, x3 = _split3(x)
    dot = lambda piece: jnp.dot(w01, piece, preferred_element_type=F32)
    return dot(x1) + dot(x2) + dot(x3)


def _split_dot_tn(x, w01):
    x1, x2, x3 = _split3(x)
    dot = lambda piece: lax.dot_general(piece, w01, (((0,), (0,)), ((), ())), preferred_element_type=F32)
    return dot(x1) + dot(x2) + dot(x3)


def _expand_cols(x, width):
    k = x.shape[1]
    r = lax.broadcasted_iota(jnp.int32, (k, k * width), 0) * width
    q = lax.broadcasted_iota(jnp.int32, (k, k * width), 1)
    sel = ((q >= r) & (q < r + width)).astype(BF16)
    x1, x2, x3 = _split3(x)
    dot = lambda piece: jnp.dot(piece, sel, preferred_element_type=F32)
    return dot(x1) + dot(x2) + dot(x3)


def _group_sums(x, width):
    w = x.shape[1]
    tile = MXU_TILE if (w % MXU_TILE == 0 and MXU_TILE % width == 0) else w
    r = lax.broadcasted_iota(jnp.int32, (tile, tile), 0)
    q = lax.broadcasted_iota(jnp.int32, (tile, tile), 1)
    ones = ((r ^ q) < width).astype(BF16)
    pieces = _split3(x)
    cols = []
    for t0 in range(0, w, tile):
        acc = None
        for piece in pieces:
            part = jnp.dot(piece[:, t0:t0 + tile], ones, preferred_element_type=F32)
            acc = part if acc is None else acc + part
        cols.append(acc)
    return cols[0] if len(cols) == 1 else jnp.concatenate(cols, axis=1)


def _last_rows(x, bb, c):
    lasts = [x[(b + 1) * c - 1:(b + 1) * c, :] for b in range(bb)]
    tiled = [jnp.broadcast_to(l, (c, x.shape[1])) for l in lasts]
    return lasts, (tiled[0] if bb == 1 else jnp.concatenate(tiled, axis=0))


def _mixer_tiling(b, t, chunk):
    bb = max(n for n in range(1, MIX_SEQS + 1) if b % n == 0)
    return bb, math.gcd(chunk, t)


def _rwkv_init(ins, outs, scr):
    shift_ref, s0_ref = ins[:2]
    prev_ref, sbd_ref = scr
    bb, heads, hd = s0_ref.shape[:3]
    prev_ref[...] = shift_ref[...]
    zero = jnp.zeros((hd, hd), F32)
    for b in range(bb):
        for j in range(heads // 2):
            sbd_ref[b, j] = jnp.concatenate(
                [jnp.concatenate([s0_ref[b, 2 * j], zero], axis=1),
                 jnp.concatenate([zero, s0_ref[b, 2 * j + 1]], axis=1)], axis=0)


def _rwkv_core(p, ins, outs, scr, tick, *, heads, lora):
    (shift_ref, s0_ref, mu_ref, w0_ref, w2_ref, a0_ref, a2_ref, g2_ref,
     kk_ref, ka_ref, rk_ref, lw_ref, lb_ref) = ins
    o_ref, shift_out_ref, s_ref = outs
    prev_ref, sbd_ref = scr
    bb, cs, wa = o_ref.shape
    c = min(RWKV_CHUNK, cs)
    nsub = cs // c
    rows, cols = p.shape
    hd = s_ref.shape[-1]
    lw_, la_, lg_ = lora

    row = lax.broadcasted_iota(jnp.int32, (rows, cols), 0)
    p_prev = pltpu.roll(p, 1, axis=0)
    for b in range(bb):
        p_prev = jnp.where(row == b * cs, prev_ref[b], p_prev)
        last = p[(b + 1) * cs - 1:(b + 1) * cs, :]
        prev_ref[b] = last
        shift_out_ref[b] = last
    pm = p + (p_prev - p) * mu_ref[...]

    r = pm[:, 0:wa]
    k = pm[:, wa:2 * wa]
    v = pm[:, 2 * wa:3 * wa]
    o1 = 3 * wa
    xw = pm[:, o1:o1 + lw_]
    xa = pm[:, o1 + lw_:o1 + lw_ + la_]
    xg = pm[:, o1 + lw_ + la_:o1 + lw_ + la_ + lg_]

    w = -jax.nn.softplus(-(w0_ref[...] + _bdot(jnp.tanh(xw), w2_ref[...]))) - 0.5
    logd = -jnp.exp(w)
    tick()
    a = jax.nn.sigmoid(a0_ref[...] + _bdot(xa, a2_ref[...]))
    g = _bdot(jax.nn.sigmoid(xg), g2_ref[...])
    kk_raw = k * kk_ref[...]
    kk = kk_raw / jnp.maximum(jnp.sqrt(_group_sums(kk_raw * kk_raw, hd)), 1e-12)
    kka = kk * a
    k2 = k * (1.0 + (a - 1.0) * ka_ref[...])

    slab = min(rows, 2 * c)
    tri = _block_tri(slab, c)
    cums = [_split_dot(tri, logd[r0:r0 + slab]) for r0 in range(0, rows, slab)]
    cum = cums[0] if len(cums) == 1 else jnp.concatenate(cums, axis=0)
    tick()
    tots, tot_rows = _last_rows(cum, bb * nsub, c)
    p_inv = jnp.exp(-cum)
    p_end = jnp.exp(tot_rows - cum)
    at = (-kk * jnp.exp(cum - logd)).astype(BF16)
    rt = (r * jnp.exp(cum)).astype(BF16)
    bt = (kka * p_inv).astype(BF16)
    kt = (k2 * p_inv).astype(BF16)
    be = (kka * p_end).astype(BF16)
    ke = (k2 * p_end).astype(BF16)
    vb = v.astype(BF16)

    pw = 2 * hd
    iota = lambda shape, dim: lax.broadcasted_iota(jnp.int32, shape, dim)
    first_c = iota((c, 2 * c), 1) < c
    first_v = iota((1, pw), 1) < hd
    bd_cc = (iota((2 * c, 2 * c), 0) < c) == (iota((2 * c, 2 * c), 1) < c)
    bd_cv = (iota((2 * c, pw), 0) < c) == (iota((2 * c, pw), 1) < hd)
    bd_vv = (iota((pw, pw), 0) < hd) == (iota((pw, pw), 1) < hd)
    bd_cv4 = (iota((4 * c, pw), 0) < 2 * c) == (iota((4 * c, pw), 1) < hd)
    outer4 = (iota((4 * c, 1), 0) < c) | (iota((4 * c, 1), 0) >= 3 * c)
    strict2 = _tri(c, strict=True, reps=2)
    incl4 = _tri(c, reps=4)
    eye2 = ((iota((c, 2 * c), 1) & (c - 1)) == iota((c, 2 * c), 0)).astype(F32)
    stack2 = lambda m: jnp.concatenate([m, m], axis=0)
    bd2 = lambda m: jnp.where(bd_cv, stack2(m), 0)

    units = [(b, ci, j) for b in range(bb) for ci in range(nsub) for j in range(heads // 2)]
    idx = [(slice(b * cs + ci * c, b * cs + (ci + 1) * c), slice(j * pw, (j + 1) * pw)) for b, ci, j in units]
    xps = [jnp.concatenate([at[rs, sl], rt[rs, sl]], axis=0) for rs, sl in idx]
    ybds = [jnp.concatenate([jnp.where(first_v, jnp.concatenate([bt[rs, sl], kt[rs, sl]], axis=0), 0),
                             jnp.where(first_v, 0, jnp.concatenate([kt[rs, sl], bt[rs, sl]], axis=0))], axis=0)
            for rs, sl in idx]
    eps = [jnp.concatenate([be[rs, sl], ke[rs, sl]], axis=0) for rs, sl in idx]
    vps = [vb[rs, sl] for rs, sl in idx]
    tick()
    gps = [_bdot_nt(xp, ybd) for xp, ybd in zip(xps, ybds)]
    tick()
    a_ps = [jnp.where(strict2, jnp.where(first_c, gp[:c, 0:2 * c], gp[:c, 2 * c:4 * c]), 0.0) for gp in gps]
    k_ps = [jnp.where(strict2, jnp.where(first_c, gp[:c, 2 * c:4 * c], gp[:c, 0:2 * c]), 0.0) for gp in gps]
    akvs = [_bdot(k_p, jnp.where(bd_cv, 0, stack2(vp))) for k_p, vp in zip(k_ps, vps)]
    tick()
    ns = a_ps
    ts = [eye2 + n for n in ns]
    step = 2
    while step < c:
        nbds = [jnp.where(bd_cc, stack2(n.astype(BF16)), 0) for n in ns]
        ns = [_bdot(n, nbd) for n, nbd in zip(ns, nbds)]
        tick()
        ts = [t + _bdot(t, jnp.where(bd_cc, stack2(n.astype(BF16)), 0)) for t, n in zip(ts, ns)]
        tick()
        step *= 2
    tas = [_bdot(t, jnp.concatenate([bd2(xp[:c]), bd2(akv.astype(BF16))], axis=1))
           for t, xp, akv in zip(ts, xps, akvs)]
    tick()
    tabs = [(ta[:, :pw].astype(BF16), ta[:, pw:].astype(BF16)) for ta in tas]
    qos = [_bdot(jnp.where(incl4, gp[c:, :], 0.0),
                 jnp.concatenate([jnp.where(bd_cv4 & outer4, jnp.concatenate([ta, ta, ta, ta], axis=0), 0),
                                  jnp.where(bd_cv4, jnp.concatenate([tkv, vp, vp, tkv], axis=0), 0)], axis=1))
           for gp, (ta, tkv), vp in zip(gps, tabs, vps)]
    tick()
    qs = [(xp[c:].astype(F32) + qo[:, :pw]).astype(BF16) for xp, qo in zip(xps, qos)]
    mds = [_bdot_tn(jnp.concatenate([jnp.concatenate([ta, tkv], axis=1),
                                     jnp.concatenate([jnp.zeros_like(vp), vp], axis=1)], axis=0), ep)
           for (ta, tkv), vp, ep in zip(tabs, vps, eps)]
    tick()
    bd_vv2 = stack2(bd_vv)
    mds = [jnp.where(bd_vv2, md, 0.0) for md in mds]

    for b in range(bb):
        sbds = [sbd_ref[b, j] for j in range(heads // 2)]
        for ci in range(nsub):
            for j in range(heads // 2):
                i = units.index((b, ci, j))
                rs, sl = idx[i]
                sbd = sbds[j]
                o_ref[b, ci * c:(ci + 1) * c, sl] = _bdot_nt(qs[i], sbd) + qos[i][:, pw:]
                sbds[j] = sbd * jnp.exp(tots[b * nsub + ci][:, sl]) + _bdot(sbd, mds[i][:pw]) + mds[i][pw:]
            tick()
        for j in range(heads // 2):
            sbd_ref[b, j] = sbds[j]

    @pl.when(pl.program_id(1) == pl.num_programs(1) - 1)
    def _():
        for b in range(bb):
            for j in range(heads // 2):
                s_ref[b, 2 * j] = sbd_ref[b, j, 0:hd, 0:hd]
                s_ref[b, 2 * j + 1] = sbd_ref[b, j, hd:pw, hd:pw]

    o = o_ref[...].reshape(rows, wa)
    dev = o - _group_sums(o, hd) * (1.0 / hd)
    var = _group_sums(dev * dev, hd) * (1.0 / hd)
    o = dev * lax.rsqrt(var + LNX_EPS) * lw_ref[...] + lb_ref[...]
    tick()
    bonus = _group_sums(r * k2 * rk_ref[...], hd) * v
    o_ref[...] = ((o + bonus) * g).reshape(bb, cs, wa)


RWKV_POINTS = 24


def _rwkv_call(x, nw, sc, sh, w_in, shift_prev, s0, prm):
    b, t, _ = x.shape
    cols = w_in[3]
    heads, hd = s0.shape[1], s0.shape[2]
    wa = heads * hd
    lora = (prm["w2"].shape[0], prm["a2"].shape[0], prm["g2"].shape[0])
    vec = lambda a: a.reshape(1, -1)
    consts = [vec(prm["mu"]), vec(prm["w0"]), prm["w2"], vec(prm["a0"]), prm["a2"], prm["g2"],
              vec(prm["k_k"]), vec(prm["k_a"]), vec(prm["r_k"]), vec(prm["lnx_w"]), vec(prm["lnx_b"])]
    seq = lambda bb, shape: pl.BlockSpec((bb,) + shape, lambda i, j: (i,) + (0,) * len(shape))
    return _fused_call(
        "rwkv7_mixer", _rwkv_init, functools.partial(_rwkv_core, heads=heads, lora=lora), RWKV_POINTS,
        x, nw, sc, sh, w_in, RWKV_CHUNK * RWKV_GROUP,
        ins=[shift_prev.reshape(b, 1, cols), s0, *consts],
        in_specs=lambda bb, c: [seq(bb, (1, cols)), seq(bb, s0.shape[1:])] + [_const_spec(a.shape) for a in consts],
        out_shapes=[jax.ShapeDtypeStruct((b, t, wa), F32),
                    jax.ShapeDtypeStruct((b, 1, cols), F32),
                    jax.ShapeDtypeStruct(s0.shape, F32)],
        out_specs=lambda bb, c: [pl.BlockSpec((bb, c, wa), lambda i, j: (i, j, 0)),
                                 seq(bb, (1, cols)), seq(bb, s0.shape[1:])],
        scratch=lambda bb, c: [pltpu.VMEM((bb, 1, cols), F32),
                               pltpu.VMEM((bb, heads // 2, 2 * hd, 2 * hd), F32)])


def _mamba_init(ins, outs, scr, *, conv_w):
    conv_ref, s0_ref = ins[:2]
    s_ref = outs[2]
    ubuf_ref = scr[0]
    bb, _, xbc = conv_ref.shape
    pad, hist = 8, conv_w - 1
    s_ref[...] = s0_ref[...]
    for b in range(bb):
        ubuf_ref[b, 0:pad, :] = jnp.zeros((pad, xbc), F32)
        ubuf_ref[b, pad - hist:pad, :] = conv_ref[b]


def _mamba_core(p, ins, outs, scr, tick, *, groups, conv_w):
    conv_ref, s0_ref, cw_ref, cb_ref, dtb_ref, alog_ref, dskip_ref, nw_ref = ins
    o_ref, conv_out_ref, s_ref = outs
    ubuf_ref, inter_ref = scr
    bb, c, wb = o_ref.shape
    rows = bb * c
    heads, hd, ns = s_ref.shape[1], s_ref.shape[2], s_ref.shape[3]
    xbc = conv_ref.shape[-1]
    hpg = heads // groups
    pad = 8
    hist = conv_w - 1

    ys = []
    for b in range(bb):
        u = p[b * c:(b + 1) * c, wb:wb + xbc]
        ext = jnp.concatenate([ubuf_ref[b], u], axis=0)
        y = cb_ref[...] + cw_ref[hist:hist + 1, :] * u
        for i in range(hist):
            y = y + cw_ref[i:i + 1, :] * pltpu.roll(ext, hist - i, axis=0)[pad:pad + c, :]
        conv_out_ref[b] = u[c - hist:c, :]
        ubuf_ref[b] = u[c - pad:c, :]
        ys.append(y)
        tick()
    xc = _silu(ys[0] if bb == 1 else jnp.concatenate(ys, axis=0))
    xs = xc[:, 0:wb]
    bm = xc[:, wb:wb + groups * ns]
    cm = xc[:, wb + groups * ns:wb + 2 * groups * ns]
    z = p[:, 0:wb]
    dt_raw = p[:, wb + xbc:wb + xbc + heads]
    dt = jax.nn.softplus(dt_raw + dtb_ref[...])
    la = dt * (-jnp.exp(alog_ref[...]))

    incl = _tri(c)
    acum = _split_dot(_block_tri(rows, c), la)
    acum_t = _split_dot_tn(la, _block_tri(rows, c, upper=True))
    a_lasts, a_last_rows = _last_rows(acum, bb, c)
    e_tots = [jnp.exp(al) for al in a_lasts]
    stack = jnp.concatenate([dt, jnp.exp(acum), jnp.exp(a_last_rows - acum),
                             jnp.broadcast_to(dskip_ref[...], (8, heads))], axis=0)
    full = _expand_cols(stack, hd)
    ecum_f = full[rows:2 * rows]
    dskip_f = full[3 * rows:3 * rows + 1]
    xd_f = xs * full[0:rows]
    xde_f = xd_f * full[2 * rows:3 * rows]
    colb = _expand_cols(acum, c)
    tick()

    rss = [slice(b * c, (b + 1) * c) for b in range(bb)]
    bgs = [[bm[rs, gi * ns:(gi + 1) * ns].astype(BF16) for gi in range(groups)] for rs in rss]
    cgs = [[cm[rs, gi * ns:(gi + 1) * ns].astype(BF16) for gi in range(groups)] for rs in rss]
    cbs = [[_bdot_nt(cg, bg) for cg, bg in zip(cgb, bgb)] for cgb, bgb in zip(cgs, bgs)]
    tick()
    pairs = [(b, h) for b in range(bb) for h in range(heads)]
    sls = [slice(h * hd, (h + 1) * hd) for _, h in pairs]
    s0s = [s_ref[b, h] for b, h in pairs]
    mats = []
    for b, h in pairs:
        seg = colb[rss[b], h * c:(h + 1) * c] - acum_t[h:h + 1, rss[b]]
        decay = jnp.where(incl, jnp.exp(jnp.where(incl, seg, 0.0)), 0.0)
        mats.append((cbs[b][h // hpg] * decay).astype(BF16))
        if h % 4 == 3:
            tick()
    intra = [_bdot(m, xd_f[rss[b], sl]) for m, (b, _), sl in zip(mats, pairs, sls)]
    tick()
    inter = [_bdot_nt(cgs[b][h // hpg], s0) for (b, h), s0 in zip(pairs, s0s)]
    tick()
    upd = [_bdot_tn(xde_f[rss[b], sl], bgs[b][h // hpg]) for (b, h), sl in zip(pairs, sls)]
    tick()
    for i, ((b, h), sl) in enumerate(zip(pairs, sls)):
        s_ref[b, h] = s0s[i] * e_tots[b][:, h:h + 1] + upd[i]
        o_ref[b, :, sl] = intra[i]
        inter_ref[b, :, sl] = inter[i]

    y_all = o_ref[...].reshape(rows, wb) + inter_ref[...].reshape(rows, wb) * ecum_f + dskip_f * xs
    yv = y_all * _silu(z)
    gw = wb // groups
    for gi in range(groups):
        sl = slice(gi * gw, (gi + 1) * gw)
        yg = yv[:, sl]
        ms = jnp.mean(yg * yg, axis=-1, keepdims=True)
        o_ref[:, :, sl] = (yg * lax.rsqrt(ms + GROUP_RMS_EPS) * nw_ref[:, sl]).reshape(bb, c, gw)


MAMBA_POINTS = 16


def _mamba_call(x, nw, sc, sh, w_in, conv_prev, s0, prm, groups):
    b, t, _ = x.shape
    heads, hd, ns = s0.shape[1:]
    wb = heads * hd
    hist, xbc = conv_prev.shape[1:]
    vec = lambda a: a.reshape(1, -1)
    consts = [prm["conv_w"], vec(prm["conv_b"]), vec(prm["dt_bias"]), vec(prm["a_log"]),
              vec(prm["d_skip"]), vec(prm["norm_b_w"])]
    seq = lambda bb, shape: pl.BlockSpec((bb,) + shape, lambda i, j: (i,) + (0,) * len(shape))
    return _fused_call(
        "mamba2_mixer", functools.partial(_mamba_init, conv_w=hist + 1),
        functools.partial(_mamba_core, groups=groups, conv_w=hist + 1), MAMBA_POINTS,
        x, nw, sc, sh, w_in, MAMBA_CHUNK,
        ins=[conv_prev, s0, *consts],
        in_specs=lambda bb, c: [seq(bb, (hist, xbc)), seq(bb, s0.shape[1:])] + [_const_spec(a.shape) for a in consts],
        out_shapes=[jax.ShapeDtypeStruct((b, t, wb), F32),
                    jax.ShapeDtypeStruct(conv_prev.shape, F32),
                    jax.ShapeDtypeStruct(s0.shape, F32)],
        out_specs=lambda bb, c: [pl.BlockSpec((bb, c, wb), lambda i, j: (i, j, 0)),
                                 seq(bb, (hist, xbc)), seq(bb, s0.shape[1:])],
        scratch=lambda bb, c: [pltpu.VMEM((bb, 8, xbc), F32), pltpu.VMEM((bb, c, wb), F32)])


HGRN_FINE = 4


def _hgrn_sum_matrix(c):
    t = np.arange(c)[:, None]
    j = np.arange(c)[None, :]
    blocks = [(j <= t)]
    m = min(HGRN_FINE, c // 2)
    while m >= 1:
        mid = (t // (2 * m)) * (2 * m) + m - 1
        right = (t % (2 * m)) >= m
        blocks.append(np.where(right, (j > mid) & (j <= t), (j > t) & (j <= mid)))
        m //= 2
    return np.concatenate(blocks, axis=0).astype(np.float32)


def _hgrn_init(ins, outs, scr):
    s0_ref = ins[0]
    (st_ref,) = scr
    bb, heads = s0_ref.shape[:2]
    for b in range(bb):
        for h in range(heads):
            st_ref[b, h] = s0_ref[b, h].T


def _hgrn_core(p, ins, outs, scr, tick, *, layer):
    s0_ref, sums_ref, lbp_ref, nw_ref = ins
    o_ref, s_ref = outs
    (st_ref,) = scr
    bb, cs, wc = o_ref.shape
    c = min(HGRN_CHUNK, cs)
    nsub = cs // c
    rows, cols = p.shape
    heads, dk, dv = s0_ref.shape[1:]
    nchunks = pl.num_programs(1)

    lbp = lbp_ref[...]
    e = jnp.exp(lbp - jnp.max(lbp, axis=0, keepdims=True))
    soft = e / jnp.sum(e, axis=0, keepdims=True)
    lb = soft[0:1, :]
    for i in range(1, layer + 1):
        lb = lb + soft[i:i + 1, :]
    lb = lb - soft[0:1, :]

    q = _silu(p[:, 0:wc])
    f = p[:, wc:2 * wc]
    v = p[:, 2 * wc:3 * wc]
    gate = p[:, 3 * wc:4 * wc]
    log_f = jnp.log(lb + (1.0 - lb) * jax.nn.sigmoid(f))
    k = (1.0 - lb) * jax.nn.sigmoid(-f)
    qk = q * k

    row = lax.broadcasted_iota(jnp.int32, (c, 1), 0)
    rr = lax.broadcasted_iota(jnp.int32, (c, c), 0)
    cc = lax.broadcasted_iota(jnp.int32, (c, c), 1)
    eye = rr == cc
    slabs = [(b, ci) for b in range(bb) for ci in range(nsub)]
    rss = [slice(b * cs + ci * c, b * cs + (ci + 1) * c) for b, ci in slabs]
    qes, kes, f_tots, levels = [], [], [], []
    for rs in rss:
        sums = _split_dot(sums_ref[...], log_f[rs])
        bcum = sums[0:c]
        tot = bcum[c - 1:c, :]
        tick()
        qes.append((q[rs] * jnp.exp(bcum)).astype(BF16))
        kes.append((k[rs] * jnp.exp(tot - bcum)).astype(BF16))
        f_tots.append(jnp.exp(tot))
        lv = []
        m = c // 2
        i = 1
        while m >= 1:
            if m > HGRN_FINE:
                mids = [jnp.broadcast_to(bcum[r0 + m - 1:r0 + m, :], (2 * m, wc)) for r0 in range(0, c, 2 * m)]
                diff = bcum - (mids[0] if len(mids) == 1 else jnp.concatenate(mids, axis=0))
                z = jnp.where((row & (2 * m - 1)) >= m, diff, -diff)
            else:
                z = sums[i * c:(i + 1) * c]
                i += 1
            wgt = jnp.exp(z)
            lv.append(((q[rs] * wgt).astype(BF16), (k[rs] * wgt).astype(BF16),
                       (rr > cc) & ((rr ^ cc) >= m) & ((rr ^ cc) < 2 * m)))
            m //= 2
            tick()
        levels.append(lv)

    units = [(si, h) for si in range(len(slabs)) for h in range(heads)]
    sls = [slice(h * dk, (h + 1) * dk) for _, h in units]
    atts = [jnp.where(eye, jnp.sum(qk[rss[si], sl], axis=-1, keepdims=True), 0.0) for (si, _), sl in zip(units, sls)]
    for li in range(len(levels[0])):
        atts = [att + jnp.where(levels[si][li][2], _bdot_nt(levels[si][li][0][:, sl], levels[si][li][1][:, sl]), 0.0)
                for att, (si, _), sl in zip(atts, units, sls)]
        tick()
    vhs = [v[rss[si], sl].astype(BF16) for (si, _), sl in zip(units, sls)]
    intra = [_bdot(att, vh) for att, vh in zip(atts, vhs)]
    tick()
    upd = [_bdot_tn(vh, kes[si][:, sl]) for vh, (si, _), sl in zip(vhs, units, sls)]
    tick()

    for b in range(bb):
        sts = [st_ref[b, h] for h in range(heads)]
        for ci in range(nsub):
            si = b * nsub + ci
            for h in range(heads):
                i = si * heads + h
                sl = sls[i]
                o = intra[i] + _bdot_nt(qes[si][:, sl], sts[h])
                sts[h] = sts[h] * f_tots[si][:, sl] + upd[i]
                ms = jnp.mean(o * o, axis=-1, keepdims=True)
                o_ref[b, ci * c:(ci + 1) * c, sl] = (o * lax.rsqrt(ms + GROUP_RMS_EPS) * nw_ref[:, sl]
                                                     * _silu(gate[rss[si], sl]))
            tick()
        for h in range(heads):
            st_ref[b, h] = sts[h]

    @pl.when(pl.program_id(1) == nchunks - 1)
    def _():
        for b in range(bb):
            for h in range(heads):
                s_ref[b, h] = st_ref[b, h].T


HGRN_POINTS = 40


def _hgrn_call(x, nw, sc, sh, w_in, s0, lb_param, norm_w, layer):
    b, t, _ = x.shape
    heads, dk, dv = s0.shape[1:]
    assert dk == dv
    wc = heads * dk
    sums = jnp.asarray(_hgrn_sum_matrix(math.gcd(HGRN_CHUNK, t)), dtype=BF16)
    ins = [s0, sums, lb_param, norm_w.reshape(1, wc)]
    seq = lambda bb, shape: pl.BlockSpec((bb,) + shape, lambda i, j: (i,) + (0,) * len(shape))
    return _fused_call(
        "hgrn2_mixer", _hgrn_init, functools.partial(_hgrn_core, layer=layer), HGRN_POINTS,
        x, nw, sc, sh, w_in, HGRN_CHUNK * HGRN_GROUP,
        ins=ins,
        in_specs=lambda bb, c: [seq(bb, s0.shape[1:])] + [_const_spec(a.shape) for a in ins[1:]],
        out_shapes=[jax.ShapeDtypeStruct((b, t, wc), F32), jax.ShapeDtypeStruct(s0.shape, F32)],
        out_specs=lambda bb, c: [pl.BlockSpec((bb, c, wc), lambda i, j: (i, j, 0)), seq(bb, s0.shape[1:])],
        scratch=lambda bb, c: [pltpu.VMEM((bb, heads, dv, dk), F32)])


def _round_up(n, m):
    return -(-n // m) * m


def _prepare_weights(w):
    names = ("w_in_ab", "w_out_ab", "w_in_c", "w_out_c", "w_gate", "w_up", "w_down")
    return {n: w[n].astype(BF16) for n in names}


def _trunk(x, mod, st_rwkv, st_shift, st_ssm, st_conv, st_hgrn, w, wb):
    depth = mod.shape[0]
    b, t, d = x.shape
    groups = (st_conv.shape[-1] - st_ssm.shape[2] * st_ssm.shape[3]) // (2 * st_ssm.shape[4])
    new_rwkv, new_shift, new_ssm, new_conv, new_hgrn = [], [], [], [], []
    for layer in range(depth):
        j = layer // 2
        sh_m, sc_m, g_m, sh_f, sc_f, g_f = (mod[layer, :, None, i * d:(i + 1) * d] for i in range(6))
        nw_mix = w["norm_mix_w"][layer].reshape(1, d)
        nw_ffn = w["norm_ffn_w"][layer].reshape(1, d)
        if layer % 2 == 0:
            prm = dict(mu=w["mu_a"][j], w0=w["w0"][j], w2=w["w2"][j], a0=w["a0"][j], a2=w["a2"][j],
                       g2=w["g2"][j], k_k=w["k_k"][j], k_a=w["k_a"][j], r_k=w["r_k"][j],
                       lnx_w=w["lnx_w"][j], lnx_b=w["lnx_b"][j])
            a_cols = w["mu_a"].shape[1]
            oa, shift_new, rwkv_new = _rwkv_call(x, nw_mix, sc_m, sh_m, (wb["w_in_ab"], j, 0, a_cols),
                                                 st_shift[:, j], st_rwkv[:, j], prm)
            prm_b = dict(conv_w=w["conv_w"][j], conv_b=w["conv_b"][j], dt_bias=w["dt_bias"][j],
                         a_log=w["a_log"][j], d_skip=w["d_skip"][j], norm_b_w=w["norm_b_w"][j])
            b_in = (wb["w_in_ab"], j, a_cols, w["w_in_ab"].shape[2] - a_cols)
            ob, conv_new, ssm_new = _mamba_call(x, nw_mix, sc_m, sh_m, b_in,
                                                st_conv[:, j], st_ssm[:, j], prm_b, groups)
            mixes, wout = [oa, ob], (wb["w_out_ab"], j)
            new_rwkv.append(rwkv_new)
            new_shift.append(shift_new[:, 0])
            new_ssm.append(ssm_new)
            new_conv.append(conv_new)
        else:
            oc, hgrn_new = _hgrn_call(x, nw_mix, sc_m, sh_m, (wb["w_in_c"], j, 0, w["w_in_c"].shape[2]),
                                      st_hgrn[:, j], w["lb_param"], w["norm_c_w"][j], j)
            mixes, wout = [oc], (wb["w_out_c"], j)
            new_hgrn.append(hgrn_new)
        final_w = w["norm_out_w"].reshape(1, d) if layer == depth - 1 else None
        x = _post_call(x, mixes, wout, g_m, nw_ffn, sc_f, sh_f, g_f,
                       (wb["w_gate"], layer), (wb["w_up"], layer), (wb["w_down"], layer), final_w)
    return (x, jnp.stack(new_rwkv, axis=1), jnp.stack(new_shift, axis=1), jnp.stack(new_ssm, axis=1),
            jnp.stack(new_conv, axis=1), jnp.stack(new_hgrn, axis=1))


def _run(x_prompt, x_sample, state_rwkv, state_rwkv_shift, state_ssm, state_conv, state_hgrn,
         c_prompt, c_sample, w):
    bp, bs = x_prompt.shape[0], x_sample.shape[0]
    rows = _round_up(bp + bs, 8)
    c_all = jnp.pad(jnp.concatenate([c_prompt, c_sample], axis=0), ((0, rows - bp - bs), (0, 0)))
    mod = _ada_call(c_all, w["ada_w"], w["ada_b"])
    wb = _prepare_weights(w)
    zeros = lambda s: jnp.zeros((bp,) + s.shape[1:], F32)
    outs_p = _trunk(x_prompt, mod[:, :bp], zeros(state_rwkv), zeros(state_rwkv_shift), zeros(state_ssm),
                    zeros(state_conv), zeros(state_hgrn), w, wb)
    outs_s = _trunk(x_sample, mod[:, bp:bp + bs], state_rwkv, state_rwkv_shift, state_ssm,
                    state_conv, state_hgrn, w, wb)
    return (outs_p[0], outs_s[0]) + outs_p[1:] + outs_s[1:]


def kernel(x_prompt, x_sample, state_rwkv, state_rwkv_shift, state_ssm, state_conv, state_hgrn, c_prompt, c_sample, norm_mix_w, norm_ffn_w, norm_out_w, ada_w, ada_b, w_in_ab, w_out_ab, mu_a, w0, w2, a0, a2, g2, k_k, k_a, r_k, lnx_w, lnx_b, conv_w, conv_b, dt_bias, a_log, d_skip, norm_b_w, w_in_c, w_out_c, lb_param, norm_c_w, w_gate, w_up, w_down):
    w = dict(norm_mix_w=norm_mix_w, norm_ffn_w=norm_ffn_w, norm_out_w=norm_out_w, ada_w=ada_w, ada_b=ada_b,
             w_in_ab=w_in_ab, w_out_ab=w_out_ab, mu_a=mu_a, w0=w0, w2=w2, a0=a0, a2=a2, g2=g2, k_k=k_k,
             k_a=k_a, r_k=r_k, lnx_w=lnx_w, lnx_b=lnx_b, conv_w=conv_w, conv_b=conv_b, dt_bias=dt_bias,
             a_log=a_log, d_skip=d_skip, norm_b_w=norm_b_w, w_in_c=w_in_c, w_out_c=w_out_c,
             lb_param=lb_param, norm_c_w=norm_c_w, w_gate=w_gate, w_up=w_up, w_down=w_down)
    return _run(x_prompt, x_sample, state_rwkv, state_rwkv_shift, state_ssm, state_conv, state_hgrn,
                c_prompt, c_sample, w)
```

```python
import functools
import math

import numpy as np
import jax
import jax.numpy as jnp
from jax import lax
from jax.experimental import pallas as pl
from jax.experimental.pallas import tpu as pltpu

F32 = jnp.float32
BF16 = jnp.bfloat16

MXU_TILE = 256
VMEM_LIMIT = 56 * 2**20
ROW_TILE = 1024
COL_CHUNK = 512
RWKV_CHUNK = 64
RWKV_GROUP = 4
MAMBA_CHUNK = 128
HGRN_CHUNK = 64
HGRN_GROUP = 2
MIX_SEQS = 4
LNX_EPS = 64e-5
RMS_EPS = 1e-6
GROUP_RMS_EPS = 1e-5


def _bdot(a, b):
    return jnp.dot(a.astype(BF16), b.astype(BF16), preferred_element_type=F32)


def _bdot_nt(a, b):
    return lax.dot_general(a.astype(BF16), b.astype(BF16), (((1,), (1,)), ((), ())),
                           preferred_element_type=F32)


def _bdot_tn(a, b):
    return lax.dot_general(a.astype(BF16), b.astype(BF16), (((0,), (0,)), ((), ())),
                           preferred_element_type=F32)


def _silu(x):
    return x * jax.nn.sigmoid(x)


def _const_spec(shape, single_buffer=False):
    nd = len(shape)
    if single_buffer:
        return pl.BlockSpec(shape, lambda *_: (0,) * nd, pipeline_mode=pl.Buffered(1))
    return pl.BlockSpec(shape, lambda *_: (0,) * nd)


def _layer_spec(stacked, layer):
    nd = stacked.ndim
    return pl.BlockSpec((None,) + stacked.shape[1:], lambda *_: (layer,) + (0,) * (nd - 1),
                        pipeline_mode=pl.Buffered(1))


def _params(n_axes):
    return pltpu.CompilerParams(dimension_semantics=("arbitrary",) * n_axes,
                                vmem_limit_bytes=VMEM_LIMIT)


def _row_tiling(b, t):
    if t >= ROW_TILE:
        assert t % ROW_TILE == 0
        return 1, ROW_TILE
    bb = max(1, min(b, ROW_TILE // t))
    while b % bb:
        bb -= 1
    return bb, t


def _col_chunks(n):
    return [(n0, min(n0 + COL_CHUNK, n)) for n0 in range(0, n, COL_CHUNK)]


def _ada_body(c_ref, w_ref, b_ref, o_ref):
    o_ref[0] = _bdot(_silu(c_ref[...]), w_ref[0]) + b_ref[0]


def _ada_call(c_all, ada_w, ada_b):
    depth, d, n = ada_w.shape
    r = c_all.shape[0]
    tn = 1024
    assert n % tn == 0
    return pl.pallas_call(
        _ada_body,
        out_shape=jax.ShapeDtypeStruct((depth, r, n), F32),
        grid=(depth, n // tn),
        in_specs=[pl.BlockSpec((r, d), lambda l, j: (0, 0)),
                  pl.BlockSpec((1, d, tn), lambda l, j: (l, 0, j)),
                  pl.BlockSpec((1, 1, tn), lambda l, j: (l, 0, j))],
        out_specs=pl.BlockSpec((1, r, tn), lambda l, j: (l, 0, j)),
        compiler_params=_params(2),
        name="ada_mod",
    )(c_all, ada_w, ada_b.reshape(depth, 1, n))


def _norm_mod(x, nw, sc, sh):
    ms = jnp.mean(x * x, axis=-1, keepdims=True)
    y = x * lax.rsqrt(ms + RMS_EPS) * nw
    return y * (1.0 + sc) + sh


class _Ticker:
    def __init__(self, thunks, points):
        self.thunks, self.total, self.points, self.calls = list(thunks), len(thunks), points, 0

    def __call__(self):
        self.calls += 1
        due = min(self.total, -(-self.calls * self.total // self.points))
        while self.total - len(self.thunks) < due:
            self.thunks.pop(0)()

    def flush(self):
        while self.thunks:
            self.thunks.pop(0)()


def _fused_body(*refs, init, core, n_in, n_out, points, n_steps, col0, cols):
    x_cur_ref, x_next_ref, nw_ref, sc_ref, sh_ref, w_ref = refs[:6]
    ins = refs[6:6 + n_in]
    outs = refs[6 + n_in:6 + n_in + n_out]
    scr = refs[6 + n_in + n_out:-2]
    bufs = refs[-2:]
    bb, c, d = x_cur_ref.shape
    rows = bb * c
    j = pl.program_id(1)

    def projection(x_ref, dst_ref):
        h = _norm_mod(x_ref[...], nw_ref[...], sc_ref[...], sh_ref[...])
        hb = h.reshape(rows, d).astype(BF16)

        def piece(n0, n1):
            def run():
                dst_ref[:, n0:n1] = jnp.dot(hb, w_ref[:, col0 + n0:col0 + n1], preferred_element_type=F32)
            return run
        return [piece(n0, min(n0 + MXU_TILE, cols)) for n0 in range(0, cols, MXU_TILE)]

    @pl.when(j == 0)
    def _():
        init(ins, outs, scr)
        for run in projection(x_cur_ref, bufs[0]):
            run()

    for parity in range(2):
        @pl.when(lax.rem(j, 2) == parity)
        def _(parity=parity):
            ahead = projection(x_next_ref, bufs[1 - parity]) if n_steps > 1 else []
            tick = _Ticker(ahead, points)
            core(bufs[parity][...], ins, outs, scr, tick)
            tick.flush()


def _fused_call(name, init, core, points, x, nw, sc, sh, w_in, chunk, ins, in_specs, out_shapes, out_specs,
                scratch):
    b, t, d = x.shape
    bb, c = _mixer_tiling(b, t, chunk)
    n = t // c
    w_stack, layer, col0, cols = w_in
    seq_spec = pl.BlockSpec((bb, 1, d), lambda i, j: (i, 0, 0))
    return pl.pallas_call(
        functools.partial(_fused_body, init=init, core=core, n_in=len(ins), n_out=len(out_shapes), points=points,
                          n_steps=n, col0=col0, cols=cols),
        out_shape=out_shapes,
        grid=(b // bb, n),
        in_specs=[pl.BlockSpec((bb, c, d), lambda i, j: (i, j, 0)),
                  pl.BlockSpec((bb, c, d), lambda i, j: (i, jnp.minimum(j + 1, n - 1), 0)),
                  _const_spec((1, d)), seq_spec, seq_spec, _layer_spec(w_stack, layer)]
                 + in_specs(bb, c),
        out_specs=out_specs(bb, c),
        scratch_shapes=scratch(bb, c) + [pltpu.VMEM((bb * c, cols), F32)] * 2,
        compiler_params=_params(2),
        name=name,
    )(x, x, nw, sc, sh, w_stack, *ins)


def _post_body(*refs, n_mix, final):
    x_ref = refs[0]
    mix_refs = refs[1:1 + n_mix]
    (wout_ref, gm_ref, nw_ref, sc_ref, sh_ref, gf_ref, wg_ref, wu_ref, wd_ref) = refs[1 + n_mix:10 + n_mix]
    rest = refs[10 + n_mix:]
    if final:
        fw_ref, o_ref, act_ref = rest
    else:
        o_ref, act_ref = rest
    bb, tt, d = x_ref.shape
    rows = bb * tt

    mix = None
    off = 0
    for m_ref in mix_refs:
        wdt = m_ref.shape[-1]
        part = jnp.dot(m_ref[...].reshape(rows, wdt), wout_ref[off:off + wdt, :],
                       preferred_element_type=F32)
        mix = part if mix is None else mix + part
        off += wdt
    x1 = x_ref[...] + gm_ref[...] * mix.reshape(bb, tt, d)

    h = _norm_mod(x1, nw_ref[...], sc_ref[...], sh_ref[...])
    hb = h.reshape(rows, d).astype(BF16)
    for f0, f1 in _col_chunks(wg_ref.shape[1]):
        gate = jnp.dot(hb, wg_ref[:, f0:f1], preferred_element_type=F32)
        up = jnp.dot(hb, wu_ref[:, f0:f1], preferred_element_type=F32)
        act_ref[:, f0:f1] = (_silu(gate) * up).astype(BF16)
    ffn = jnp.dot(act_ref[...], wd_ref[...], preferred_element_type=F32)
    x2 = x1 + gf_ref[...] * ffn.reshape(bb, tt, d)
    if final:
        ms = jnp.mean(x2 * x2, axis=-1, keepdims=True)
        x2 = x2 * lax.rsqrt(ms + RMS_EPS) * fw_ref[...]
    o_ref[...] = x2


def _post_call(x, mixes, wout, gm, nw, sc, sh, gf, wg, wu, wd, final_w):
    b, t, d = x.shape
    bb, tt = _row_tiling(b, t)
    final = final_w is not None
    row_spec = lambda w: pl.BlockSpec((bb, tt, w), lambda i, j: (i, j, 0))
    seq_spec = pl.BlockSpec((bb, 1, d), lambda i, j: (i, 0, 0))
    in_specs = ([row_spec(d)] + [row_spec(m.shape[-1]) for m in mixes]
                + [_layer_spec(*wout), seq_spec, _const_spec((1, d)), seq_spec, seq_spec, seq_spec,
                   _layer_spec(*wg), _layer_spec(*wu), _layer_spec(*wd)])
    args = [x, *mixes, wout[0], gm, nw, sc, sh, gf, wg[0], wu[0], wd[0]]
    if final:
        in_specs.append(_const_spec((1, d)))
        args.append(final_w)
    return pl.pallas_call(
        functools.partial(_post_body, n_mix=len(mixes), final=final),
        out_shape=jax.ShapeDtypeStruct((b, t, d), F32),
        grid=(b // bb, t // tt),
        in_specs=in_specs,
        out_specs=row_spec(d),
        scratch_shapes=[pltpu.VMEM((bb * tt, wg[0].shape[2]), BF16)],
        compiler_params=_params(2),
        name="post_ffn",
    )(*args)


def _tri(c, strict=False, reps=1):
    row = lax.broadcasted_iota(jnp.int32, (c, reps * c), 0)
    col = lax.broadcasted_iota(jnp.int32, (c, reps * c), 1) & (c - 1)
    return (row > col) if strict else (row >= col)


def _block_tri(rows, c, upper=False):
    r = lax.broadcasted_iota(jnp.int32, (rows, rows), 0)
    q = lax.broadcasted_iota(jnp.int32, (rows, rows), 1)
    tri = (r <= q) if upper else (r >= q)
    return (tri & ((r ^ q) < c)).astype(BF16)


def _split3(x):
    x1 = x.astype(BF16)
    r1 = x - x1.astype(F32)
    x2 = r1.astype(BF16)
    x3 = (r1 - x2.astype(F32)).astype(BF16)
    return x1, x2, x3


def _split_dot(w01, x):
    x1, x2, x3 = _split3(x)
    dot = lambda piece: jnp.dot(w01, piece, preferred_element_type=F32)
    return dot(x1) + dot(x2) + dot(x3)


def _split_dot_tn(x, w01):
    x1, x2, x3 = _split3(x)
    dot = lambda piece: lax.dot_general(piece, w01, (((0,), (0,)), ((), ())), preferred_element_type=F32)
    return dot(x1) + dot(x2) + dot(x3)


def _expand_cols(x, width):
    k = x.shape[1]
    r = lax.broadcasted_iota(jnp.int32, (k, k * width), 0) * width
    q = lax.broadcasted_iota(jnp.int32, (k, k * width), 1)
    sel = ((q >= r) & (q < r + width)).astype(BF16)
    x1, x2, x3 = _split3(x)
    dot = lambda piece: jnp.dot(piece, sel, preferred_element_type=F32)
    return dot(x1) + dot(x2) + dot(x3)


def _group_sums(x, width):
    w = x.shape[1]
    tile = MXU_TILE if (w % MXU_TILE == 0 and MXU_TILE % width == 0) else w
    r = lax.broadcasted_iota(jnp.int32, (tile, tile), 0)
    q = lax.broadcasted_iota(jnp.int32, (tile, tile), 1)
    ones = ((r ^ q) < width).astype(BF16)
    pieces = _split3(x)
    cols = []
    for t0 in range(0, w, tile):
        acc = None
        for piece in pieces:
            part = jnp.dot(piece[:, t0:t0 + tile], ones, preferred_element_type=F32)
            acc = part if acc is None else acc + part
        cols.append(acc)
    return cols[0] if len(cols) == 1 else jnp.concatenate(cols, axis=1)


def _last_rows(x, bb, c):
    lasts = [x[(b + 1) * c - 1:(b + 1) * c, :] for b in range(bb)]
    tiled = [jnp.broadcast_to(l, (c, x.shape[1])) for l in lasts]
    return lasts, (tiled[0] if bb == 1 else jnp.concatenate(tiled, axis=0))


def _mixer_tiling(b, t, chunk):
    bb = max(n for n in range(1, MIX_SEQS + 1) if b % n == 0)
    return bb, math.gcd(chunk, t)


def _rwkv_init(ins, outs, scr):
    shift_ref, s0_ref = ins[:2]
    prev_ref, sbd_ref = scr[:2]
    bb, heads, hd = s0_ref.shape[:3]
    prev_ref[...] = shift_ref[...]
    zero = jnp.zeros((hd, hd), F32)
    for b in range(bb):
        for j in range(heads // 2):
            sbd_ref[b, j] = jnp.concatenate(
                [jnp.concatenate([s0_ref[b, 2 * j], zero], axis=1),
                 jnp.concatenate([zero, s0_ref[b, 2 * j + 1]], axis=1)], axis=0)


def _rwkv_core(p, ins, outs, scr, tick, *, heads, lora):
    (shift_ref, s0_ref, mu_ref, w0_ref, w2_ref, a0_ref, a2_ref, g2_ref,
     kk_ref, ka_ref, rk_ref, lw_ref, lb_ref) = ins
    o_ref, shift_out_ref, s_ref = outs
    prev_ref, sbd_ref, raw_ref = scr
    bb, cs, wa = o_ref.shape
    c = min(RWKV_CHUNK, cs)
    nsub = cs // c
    rows, cols = p.shape
    hd = s_ref.shape[-1]
    lw_, la_, lg_ = lora

    row = lax.broadcasted_iota(jnp.int32, (rows, cols), 0)
    p_prev = pltpu.roll(p, 1, axis=0)
    for b in range(bb):
        p_prev = jnp.where(row == b * cs, prev_ref[b], p_prev)
        last = p[(b + 1) * cs - 1:(b + 1) * cs, :]
        prev_ref[b] = last
        shift_out_ref[b] = last
    pm = p + (p_prev - p) * mu_ref[...]

    r = pm[:, 0:wa]
    k = pm[:, wa:2 * wa]
    v = pm[:, 2 * wa:3 * wa]
    o1 = 3 * wa
    xw = pm[:, o1:o1 + lw_]
    xa = pm[:, o1 + lw_:o1 + lw_ + la_]
    xg = pm[:, o1 + lw_ + la_:o1 + lw_ + la_ + lg_]

    w = -jax.nn.softplus(-(w0_ref[...] + _bdot(jnp.tanh(xw), w2_ref[...]))) - 0.5
    logd = -jnp.exp(w)
    tick()
    a = jax.nn.sigmoid(a0_ref[...] + _bdot(xa, a2_ref[...]))
    g = _bdot(jax.nn.sigmoid(xg), g2_ref[...])
    kk_raw = k * kk_ref[...]
    kk = kk_raw / jnp.maximum(jnp.sqrt(_group_sums(kk_raw * kk_raw, hd)), 1e-12)
    kka = kk * a
    k2 = k * (1.0 + (a - 1.0) * ka_ref[...])

    slab = min(rows, 2 * c)
    tri = _block_tri(slab, c)
    cums = [_split_dot(tri, logd[r0:r0 + slab]) for r0 in range(0, rows, slab)]
    cum = cums[0] if len(cums) == 1 else jnp.concatenate(cums, axis=0)
    tick()
    tots, tot_rows = _last_rows(cum, bb * nsub, c)
    p_inv = jnp.exp(-cum)
    p_end = jnp.exp(tot_rows - cum)
    at = (-kk * jnp.exp(cum - logd)).astype(BF16)
    rt = (r * jnp.exp(cum)).astype(BF16)
    bt = (kka * p_inv).astype(BF16)
    kt = (k2 * p_inv).astype(BF16)
    be = (kka * p_end).astype(BF16)
    ke = (k2 * p_end).astype(BF16)
    vb = v.astype(BF16)

    pw = 2 * hd
    iota = lambda shape, dim: lax.broadcasted_iota(jnp.int32, shape, dim)
    first_c = iota((c, 2 * c), 1) < c
    first_v = iota((1, pw), 1) < hd
    bd_cc = (iota((2 * c, 2 * c), 0) < c) == (iota((2 * c, 2 * c), 1) < c)
    bd_cv = (iota((2 * c, pw), 0) < c) == (iota((2 * c, pw), 1) < hd)
    bd_vv = (iota((pw, pw), 0) < hd) == (iota((pw, pw), 1) < hd)
    bd_cv4 = (iota((4 * c, pw), 0) < 2 * c) == (iota((4 * c, pw), 1) < hd)
    outer4 = (iota((4 * c, 1), 0) < c) | (iota((4 * c, 1), 0) >= 3 * c)
    strict2 = _tri(c, strict=True, reps=2)
    incl4 = _tri(c, reps=4)
    eye2 = ((iota((c, 2 * c), 1) & (c - 1)) == iota((c, 2 * c), 0)).astype(F32)
    stack2 = lambda m: jnp.concatenate([m, m], axis=0)
    bd2 = lambda m: jnp.where(bd_cv, stack2(m), 0)

    units = [(b, ci, j) for b in range(bb) for ci in range(nsub) for j in range(heads // 2)]
    idx = [(slice(b * cs + ci * c, b * cs + (ci + 1) * c), slice(j * pw, (j + 1) * pw)) for b, ci, j in units]
    xps = [jnp.concatenate([at[rs, sl], rt[rs, sl]], axis=0) for rs, sl in idx]
    ybds = [jnp.concatenate([jnp.where(first_v, jnp.concatenate([bt[rs, sl], kt[rs, sl]], axis=0), 0),
                             jnp.where(first_v, 0, jnp.concatenate([kt[rs, sl], bt[rs, sl]], axis=0))], axis=0)
            for rs, sl in idx]
    eps = [jnp.concatenate([be[rs, sl], ke[rs, sl]], axis=0) for rs, sl in idx]
    vps = [vb[rs, sl] for rs, sl in idx]
    tick()
    gps = [_bdot_nt(xp, ybd) for xp, ybd in zip(xps, ybds)]
    tick()
    a_ps = [jnp.where(strict2, jnp.where(first_c, gp[:c, 0:2 * c], gp[:c, 2 * c:4 * c]), 0.0) for gp in gps]
    k_ps = [jnp.where(strict2, jnp.where(first_c, gp[:c, 2 * c:4 * c], gp[:c, 0:2 * c]), 0.0) for gp in gps]
    akvs = [_bdot(k_p, jnp.where(bd_cv, 0, stack2(vp))) for k_p, vp in zip(k_ps, vps)]
    tick()
    ns = a_ps
    ts = [eye2 + n for n in ns]
    step = 2
    while step < c:
        nbds = [jnp.where(bd_cc, stack2(n.astype(BF16)), 0) for n in ns]
        ns = [_bdot(n, nbd) for n, nbd in zip(ns, nbds)]
        tick()
        ts = [t + _bdot(t, jnp.where(bd_cc, stack2(n.astype(BF16)), 0)) for t, n in zip(ts, ns)]
        tick()
        step *= 2
    tas = [_bdot(t, jnp.concatenate([bd2(xp[:c]), bd2(akv.astype(BF16))], axis=1))
           for t, xp, akv in zip(ts, xps, akvs)]
    tick()
    tabs = [(ta[:, :pw].astype(BF16), ta[:, pw:].astype(BF16)) for ta in tas]
    qos = [_bdot(jnp.where(incl4, gp[c:, :], 0.0),
                 jnp.concatenate([jnp.where(bd_cv4 & outer4, jnp.concatenate([ta, ta, ta, ta], axis=0), 0),
                                  jnp.where(bd_cv4, jnp.concatenate([tkv, vp, vp, tkv], axis=0), 0)], axis=1))
           for gp, (ta, tkv), vp in zip(gps, tabs, vps)]
    tick()
    qs = [(xp[c:].astype(F32) + qo[:, :pw]).astype(BF16) for xp, qo in zip(xps, qos)]
    mds = [_bdot_tn(jnp.concatenate([jnp.concatenate([ta, tkv], axis=1),
                                     jnp.concatenate([jnp.zeros_like(vp), vp], axis=1)], axis=0), ep)
           for (ta, tkv), vp, ep in zip(tabs, vps, eps)]
    tick()
    bd_vv2 = stack2(bd_vv)
    mds = [jnp.where(bd_vv2, md, 0.0) for md in mds]

    for b in range(bb):
        sbds = [sbd_ref[b, j] for j in range(heads // 2)]
        for ci in range(nsub):
            for j in range(heads // 2):
                i = units.index((b, ci, j))
                rs, sl = idx[i]
                sbd = sbds[j]
                raw_ref[b, ci * c:(ci + 1) * c, sl] = _bdot_nt(qs[i], sbd) + qos[i][:, pw:]
                sbds[j] = sbd * jnp.exp(tots[b * nsub + ci][:, sl]) + _bdot(sbd, mds[i][:pw]) + mds[i][pw:]
            tick()
        for j in range(heads // 2):
            sbd_ref[b, j] = sbds[j]

    @pl.when(pl.program_id(1) == pl.num_programs(1) - 1)
    def _():
        for b in range(bb):
            for j in range(heads // 2):
                s_ref[b, 2 * j] = sbd_ref[b, j, 0:hd, 0:hd]
                s_ref[b, 2 * j + 1] = sbd_ref[b, j, hd:pw, hd:pw]

    o = raw_ref[...].reshape(rows, wa)
    dev = o - _group_sums(o, hd) * (1.0 / hd)
    var = _group_sums(dev * dev, hd) * (1.0 / hd)
    o = dev * lax.rsqrt(var + LNX_EPS) * lw_ref[...] + lb_ref[...]
    tick()
    bonus = _group_sums(r * k2 * rk_ref[...], hd) * v
    o_ref[...] = ((o + bonus) * g).reshape(bb, cs, wa).astype(o_ref.dtype)


RWKV_POINTS = 24


def _rwkv_call(x, nw, sc, sh, w_in, shift_prev, s0, prm):
    b, t, _ = x.shape
    cols = w_in[3]
    heads, hd = s0.shape[1], s0.shape[2]
    wa = heads * hd
    lora = (prm["w2"].shape[0], prm["a2"].shape[0], prm["g2"].shape[0])
    vec = lambda a: a.reshape(1, -1)
    consts = [vec(prm["mu"]), vec(prm["w0"]), prm["w2"], vec(prm["a0"]), prm["a2"], prm["g2"],
              vec(prm["k_k"]), vec(prm["k_a"]), vec(prm["r_k"]), vec(prm["lnx_w"]), vec(prm["lnx_b"])]
    seq = lambda bb, shape: pl.BlockSpec((bb,) + shape, lambda i, j: (i,) + (0,) * len(shape))
    return _fused_call(
        "rwkv7_mixer", _rwkv_init, functools.partial(_rwkv_core, heads=heads, lora=lora), RWKV_POINTS,
        x, nw, sc, sh, w_in, RWKV_CHUNK * RWKV_GROUP,
        ins=[shift_prev.reshape(b, 1, cols), s0, *consts],
        in_specs=lambda bb, c: [seq(bb, (1, cols)), seq(bb, s0.shape[1:])] + [_const_spec(a.shape) for a in consts],
        out_shapes=[jax.ShapeDtypeStruct((b, t, wa), BF16),
                    jax.ShapeDtypeStruct((b, 1, cols), F32),
                    jax.ShapeDtypeStruct(s0.shape, F32)],
        out_specs=lambda bb, c: [pl.BlockSpec((bb, c, wa), lambda i, j: (i, j, 0)),
                                 seq(bb, (1, cols)), seq(bb, s0.shape[1:])],
        scratch=lambda bb, c: [pltpu.VMEM((bb, 1, cols), F32),
                               pltpu.VMEM((bb, heads // 2, 2 * hd, 2 * hd), F32),
                               pltpu.VMEM((bb, c, wa), F32)])


def _mamba_init(ins, outs, scr, *, conv_w):
    conv_ref, s0_ref = ins[:2]
    s_ref = outs[2]
    ubuf_ref = scr[0]
    bb, _, xbc = conv_ref.shape
    pad, hist = 8, conv_w - 1
    s_ref[...] = s0_ref[...]
    for b in range(bb):
        ubuf_ref[b, 0:pad, :] = jnp.zeros((pad, xbc), F32)
        ubuf_ref[b, pad - hist:pad, :] = conv_ref[b]


def _mamba_core(p, ins, outs, scr, tick, *, groups, conv_w):
    conv_ref, s0_ref, cw_ref, cb_ref, dtb_ref, alog_ref, dskip_ref, nw_ref = ins
    o_ref, conv_out_ref, s_ref = outs
    ubuf_ref, inter_ref, intra_ref = scr
    bb, c, wb = o_ref.shape
    rows = bb * c
    heads, hd, ns = s_ref.shape[1], s_ref.shape[2], s_ref.shape[3]
    xbc = conv_ref.shape[-1]
    hpg = heads // groups
    pad = 8
    hist = conv_w - 1

    ys = []
    for b in range(bb):
        u = p[b * c:(b + 1) * c, wb:wb + xbc]
        ext = jnp.concatenate([ubuf_ref[b], u], axis=0)
        y = cb_ref[...] + cw_ref[hist:hist + 1, :] * u
        for i in range(hist):
            y = y + cw_ref[i:i + 1, :] * pltpu.roll(ext, hist - i, axis=0)[pad:pad + c, :]
        conv_out_ref[b] = u[c - hist:c, :]
        ubuf_ref[b] = u[c - pad:c, :]
        ys.append(y)
        tick()
    xc = _silu(ys[0] if bb == 1 else jnp.concatenate(ys, axis=0))
    xs = xc[:, 0:wb]
    bm = xc[:, wb:wb + groups * ns]
    cm = xc[:, wb + groups * ns:wb + 2 * groups * ns]
    z = p[:, 0:wb]
    dt_raw = p[:, wb + xbc:wb + xbc + heads]
    dt = jax.nn.softplus(dt_raw + dtb_ref[...])
    la = dt * (-jnp.exp(alog_ref[...]))

    incl = _tri(c)
    acum = _split_dot(_block_tri(rows, c), la)
    acum_t = _split_dot_tn(la, _block_tri(rows, c, upper=True))
    a_lasts, a_last_rows = _last_rows(acum, bb, c)
    e_tots = [jnp.exp(al) for al in a_lasts]
    stack = jnp.concatenate([dt, jnp.exp(acum), jnp.exp(a_last_rows - acum),
                             jnp.broadcast_to(dskip_ref[...], (8, heads))], axis=0)
    full = _expand_cols(stack, hd)
    ecum_f = full[rows:2 * rows]
    dskip_f = full[3 * rows:3 * rows + 1]
    xd_f = xs * full[0:rows]
    xde_f = xd_f * full[2 * rows:3 * rows]
    colb = _expand_cols(acum, c)
    tick()

    rss = [slice(b * c, (b + 1) * c) for b in range(bb)]
    bgs = [[bm[rs, gi * ns:(gi + 1) * ns].astype(BF16) for gi in range(groups)] for rs in rss]
    cgs = [[cm[rs, gi * ns:(gi + 1) * ns].astype(BF16) for gi in range(groups)] for rs in rss]
    cbs = [[_bdot_nt(cg, bg) for cg, bg in zip(cgb, bgb)] for cgb, bgb in zip(cgs, bgs)]
    tick()
    pairs = [(b, h) for b in range(bb) for h in range(heads)]
    sls = [slice(h * hd, (h + 1) * hd) for _, h in pairs]
    s0s = [s_ref[b, h] for b, h in pairs]
    mats = []
    for b, h in pairs:
        seg = colb[rss[b], h * c:(h + 1) * c] - acum_t[h:h + 1, rss[b]]
        decay = jnp.where(incl, jnp.exp(jnp.where(incl, seg, 0.0)), 0.0)
        mats.append((cbs[b][h // hpg] * decay).astype(BF16))
        if h % 4 == 3:
            tick()
    intra = [_bdot(m, xd_f[rss[b], sl]) for m, (b, _), sl in zip(mats, pairs, sls)]
    tick()
    inter = [_bdot_nt(cgs[b][h // hpg], s0) for (b, h), s0 in zip(pairs, s0s)]
    tick()
    upd = [_bdot_tn(xde_f[rss[b], sl], bgs[b][h // hpg]) for (b, h), sl in zip(pairs, sls)]
    tick()
    for i, ((b, h), sl) in enumerate(zip(pairs, sls)):
        s_ref[b, h] = s0s[i] * e_tots[b][:, h:h + 1] + upd[i]
        intra_ref[b, :, sl] = intra[i]
        inter_ref[b, :, sl] = inter[i]

    y_all = intra_ref[...].reshape(rows, wb) + inter_ref[...].reshape(rows, wb) * ecum_f + dskip_f * xs
    yv = y_all * _silu(z)
    gw = wb // groups
    for gi in range(groups):
        sl = slice(gi * gw, (gi + 1) * gw)
        yg = yv[:, sl]
        ms = jnp.mean(yg * yg, axis=-1, keepdims=True)
        o_ref[:, :, sl] = (yg * lax.rsqrt(ms + GROUP_RMS_EPS) * nw_ref[:, sl]).reshape(bb, c, gw).astype(o_ref.dtype)


MAMBA_POINTS = 16


def _mamba_call(x, nw, sc, sh, w_in, conv_prev, s0, prm, groups):
    b, t, _ = x.shape
    heads, hd, ns = s0.shape[1:]
    wb = heads * hd
    hist, xbc = conv_prev.shape[1:]
    vec = lambda a: a.reshape(1, -1)
    consts = [prm["conv_w"], vec(prm["conv_b"]), vec(prm["dt_bias"]), vec(prm["a_log"]),
              vec(prm["d_skip"]), vec(prm["norm_b_w"])]
    seq = lambda bb, shape: pl.BlockSpec((bb,) + shape, lambda i, j: (i,) + (0,) * len(shape))
    return _fused_call(
        "mamba2_mixer", functools.partial(_mamba_init, conv_w=hist + 1),
        functools.partial(_mamba_core, groups=groups, conv_w=hist + 1), MAMBA_POINTS,
        x, nw, sc, sh, w_in, MAMBA_CHUNK,
        ins=[conv_prev, s0, *consts],
        in_specs=lambda bb, c: [seq(bb, (hist, xbc)), seq(bb, s0.shape[1:])] + [_const_spec(a.shape) for a in consts],
        out_shapes=[jax.ShapeDtypeStruct((b, t, wb), BF16),
                    jax.ShapeDtypeStruct(conv_prev.shape, F32),
                    jax.ShapeDtypeStruct(s0.shape, F32)],
        out_specs=lambda bb, c: [pl.BlockSpec((bb, c, wb), lambda i, j: (i, j, 0)),
                                 seq(bb, (hist, xbc)), seq(bb, s0.shape[1:])],
        scratch=lambda bb, c: [pltpu.VMEM((bb, 8, xbc), F32), pltpu.VMEM((bb, c, wb), F32),
                               pltpu.VMEM((bb, c, wb), F32)])


HGRN_FINE = 4


def _hgrn_sum_matrix(c):
    t = np.arange(c)[:, None]
    j = np.arange(c)[None, :]
    blocks = [(j <= t)]
    m = min(HGRN_FINE, c // 2)
    while m >= 1:
        mid = (t // (2 * m)) * (2 * m) + m - 1
        right = (t % (2 * m)) >= m
        blocks.append(np.where(right, (j > mid) & (j <= t), (j > t) & (j <= mid)))
        m //= 2
    return np.concatenate(blocks, axis=0).astype(np.float32)


def _hgrn_init(ins, outs, scr):
    s0_ref = ins[0]
    (st_ref,) = scr
    bb, heads = s0_ref.shape[:2]
    for b in range(bb):
        for h in range(heads):
            st_ref[b, h] = s0_ref[b, h].T


def _hgrn_core(p, ins, outs, scr, tick, *, layer):
    s0_ref, sums_ref, lbp_ref, nw_ref = ins
    o_ref, s_ref = outs
    (st_ref,) = scr
    bb, cs, wc = o_ref.shape
    c = min(HGRN_CHUNK, cs)
    nsub = cs // c
    rows, cols = p.shape
    heads, dk, dv = s0_ref.shape[1:]
    nchunks = pl.num_programs(1)

    lbp = lbp_ref[...]
    e = jnp.exp(lbp - jnp.max(lbp, axis=0, keepdims=True))
    soft = e / jnp.sum(e, axis=0, keepdims=True)
    lb = soft[0:1, :]
    for i in range(1, layer + 1):
        lb = lb + soft[i:i + 1, :]
    lb = lb - soft[0:1, :]

    q = _silu(p[:, 0:wc])
    f = p[:, wc:2 * wc]
    v = p[:, 2 * wc:3 * wc]
    gate = p[:, 3 * wc:4 * wc]
    log_f = jnp.log(lb + (1.0 - lb) * jax.nn.sigmoid(f))
    k = (1.0 - lb) * jax.nn.sigmoid(-f)
    qk = q * k

    row = lax.broadcasted_iota(jnp.int32, (c, 1), 0)
    rr = lax.broadcasted_iota(jnp.int32, (c, c), 0)
    cc = lax.broadcasted_iota(jnp.int32, (c, c), 1)
    eye = rr == cc
    slabs = [(b, ci) for b in range(bb) for ci in range(nsub)]
    rss = [slice(b * cs + ci * c, b * cs + (ci + 1) * c) for b, ci in slabs]
    qes, kes, f_tots, levels = [], [], [], []
    for rs in rss:
        sums = _split_dot(sums_ref[...], log_f[rs])
        bcum = sums[0:c]
        tot = bcum[c - 1:c, :]
        tick()
        qes.append((q[rs] * jnp.exp(bcum)).astype(BF16))
        kes.append((k[rs] * jnp.exp(tot - bcum)).astype(BF16))
        f_tots.append(jnp.exp(tot))
        lv = []
        m = c // 2
        i = 1
        while m >= 1:
            if m > HGRN_FINE:
                mids = [jnp.broadcast_to(bcum[r0 + m - 1:r0 + m, :], (2 * m, wc)) for r0 in range(0, c, 2 * m)]
                diff = bcum - (mids[0] if len(mids) == 1 else jnp.concatenate(mids, axis=0))
                z = jnp.where((row & (2 * m - 1)) >= m, diff, -diff)
            else:
                z = sums[i * c:(i + 1) * c]
                i += 1
            wgt = jnp.exp(z)
            lv.append(((q[rs] * wgt).astype(BF16), (k[rs] * wgt).astype(BF16),
                       (rr > cc) & ((rr ^ cc) >= m) & ((rr ^ cc) < 2 * m)))
            m //= 2
            tick()
        levels.append(lv)

    units = [(si, h) for si in range(len(slabs)) for h in range(heads)]
    sls = [slice(h * dk, (h + 1) * dk) for _, h in units]
    atts = [jnp.where(eye, jnp.sum(qk[rss[si], sl], axis=-1, keepdims=True), 0.0) for (si, _), sl in zip(units, sls)]
    for li in range(len(levels[0])):
        atts = [att + jnp.where(levels[si][li][2], _bdot_nt(levels[si][li][0][:, sl], levels[si][li][1][:, sl]), 0.0)
                for att, (si, _), sl in zip(atts, units, sls)]
        tick()
    vhs = [v[rss[si], sl].astype(BF16) for (si, _), sl in zip(units, sls)]
    intra = [_bdot(att, vh) for att, vh in zip(atts, vhs)]
    tick()
    upd = [_bdot_tn(vh, kes[si][:, sl]) for vh, (si, _), sl in zip(vhs, units, sls)]
    tick()

    for b in range(bb):
        sts = [st_ref[b, h] for h in range(heads)]
        for ci in range(nsub):
            si = b * nsub + ci
            for h in range(heads):
                i = si * heads + h
                sl = sls[i]
                o = intra[i] + _bdot_nt(qes[si][:, sl], sts[h])
                sts[h] = sts[h] * f_tots[si][:, sl] + upd[i]
                ms = jnp.mean(o * o, axis=-1, keepdims=True)
                o_ref[b, ci * c:(ci + 1) * c, sl] = (o * lax.rsqrt(ms + GROUP_RMS_EPS) * nw_ref[:, sl]
                                                     * _silu(gate[rss[si], sl])).astype(o_ref.dtype)
            tick()
        for h in range(heads):
            st_ref[b, h] = sts[h]

    @pl.when(pl.program_id(1) == nchunks - 1)
    def _():
        for b in range(bb):
            for h in range(heads):
                s_ref[b, h] = st_ref[b, h].T


HGRN_POINTS = 40


def _hgrn_call(x, nw, sc, sh, w_in, s0, lb_param, norm_w, layer):
    b, t, _ = x.shape
    heads, dk, dv = s0.shape[1:]
    assert dk == dv
    wc = heads * dk
    sums = jnp.asarray(_hgrn_sum_matrix(math.gcd(HGRN_CHUNK, t)), dtype=BF16)
    ins = [s0, sums, lb_param, norm_w.reshape(1, wc)]
    seq = lambda bb, shape: pl.BlockSpec((bb,) + shape, lambda i, j: (i,) + (0,) * len(shape))
    return _fused_call(
        "hgrn2_mixer", _hgrn_init, functools.partial(_hgrn_core, layer=layer), HGRN_POINTS,
        x, nw, sc, sh, w_in, HGRN_CHUNK * HGRN_GROUP,
        ins=ins,
        in_specs=lambda bb, c: [seq(bb, s0.shape[1:])] + [_const_spec(a.shape) for a in ins[1:]],
        out_shapes=[jax.ShapeDtypeStruct((b, t, wc), BF16), jax.ShapeDtypeStruct(s0.shape, F32)],
        out_specs=lambda bb, c: [pl.BlockSpec((bb, c, wc), lambda i, j: (i, j, 0)), seq(bb, s0.shape[1:])],
        scratch=lambda bb, c: [pltpu.VMEM((bb, heads, dv, dk), F32)])


def _round_up(n, m):
    return -(-n // m) * m


def _prepare_weights(w):
    names = ("w_in_ab", "w_out_ab", "w_in_c", "w_out_c", "w_gate", "w_up", "w_down")
    return {n: w[n].astype(BF16) for n in names}


def _trunk(x, mod, st_rwkv, st_shift, st_ssm, st_conv, st_hgrn, w, wb):
    depth = mod.shape[0]
    b, t, d = x.shape
    groups = (st_conv.shape[-1] - st_ssm.shape[2] * st_ssm.shape[3]) // (2 * st_ssm.shape[4])
    new_rwkv, new_shift, new_ssm, new_conv, new_hgrn = [], [], [], [], []
    for layer in range(depth):
        j = layer // 2
        sh_m, sc_m, g_m, sh_f, sc_f, g_f = (mod[layer, :, None, i * d:(i + 1) * d] for i in range(6))
        nw_mix = w["norm_mix_w"][layer].reshape(1, d)
        nw_ffn = w["norm_ffn_w"][layer].reshape(1, d)
        if layer % 2 == 0:
            prm = dict(mu=w["mu_a"][j], w0=w["w0"][j], w2=w["w2"][j], a0=w["a0"][j], a2=w["a2"][j],
                       g2=w["g2"][j], k_k=w["k_k"][j], k_a=w["k_a"][j], r_k=w["r_k"][j],
                       lnx_w=w["lnx_w"][j], lnx_b=w["lnx_b"][j])
            a_cols = w["mu_a"].shape[1]
            oa, shift_new, rwkv_new = _rwkv_call(x, nw_mix, sc_m, sh_m, (wb["w_in_ab"], j, 0, a_cols),
                                                 st_shift[:, j], st_rwkv[:, j], prm)
            prm_b = dict(conv_w=w["conv_w"][j], conv_b=w["conv_b"][j], dt_bias=w["dt_bias"][j],
                         a_log=w["a_log"][j], d_skip=w["d_skip"][j], norm_b_w=w["norm_b_w"][j])
            b_in = (wb["w_in_ab"], j, a_cols, w["w_in_ab"].shape[2] - a_cols)
            ob, conv_new, ssm_new = _mamba_call(x, nw_mix, sc_m, sh_m, b_in,
                                                st_conv[:, j], st_ssm[:, j], prm_b, groups)
            mixes, wout = [oa, ob], (wb["w_out_ab"], j)
            new_rwkv.append(rwkv_new)
            new_shift.append(shift_new[:, 0])
            new_ssm.append(ssm_new)
            new_conv.append(conv_new)
        else:
            oc, hgrn_new = _hgrn_call(x, nw_mix, sc_m, sh_m, (wb["w_in_c"], j, 0, w["w_in_c"].shape[2]),
                                      st_hgrn[:, j], w["lb_param"], w["norm_c_w"][j], j)
            mixes, wout = [oc], (wb["w_out_c"], j)
            new_hgrn.append(hgrn_new)
        final_w = w["norm_out_w"].reshape(1, d) if layer == depth - 1 else None
        x = _post_call(x, mixes, wout, g_m, nw_ffn, sc_f, sh_f, g_f,
                       (wb["w_gate"], layer), (wb["w_up"], layer), (wb["w_down"], layer), final_w)
    return (x, jnp.stack(new_rwkv, axis=1), jnp.stack(new_shift, axis=1), jnp.stack(new_ssm, axis=1),
            jnp.stack(new_conv, axis=1), jnp.stack(new_hgrn, axis=1))


def _run(x_prompt, x_sample, state_rwkv, state_rwkv_shift, state_ssm, state_conv, state_hgrn,
         c_prompt, c_sample, w):
    bp, bs = x_prompt.shape[0], x_sample.shape[0]
    rows = _round_up(bp + bs, 8)
    c_all = jnp.pad(jnp.concatenate([c_prompt, c_sample], axis=0), ((0, rows - bp - bs), (0, 0)))
    mod = _ada_call(c_all, w["ada_w"], w["ada_b"])
    wb = _prepare_weights(w)
    zeros = lambda s: jnp.zeros((bp,) + s.shape[1:], F32)
    outs_p = _trunk(x_prompt, mod[:, :bp], zeros(state_rwkv), zeros(state_rwkv_shift), zeros(state_ssm),
                    zeros(state_conv), zeros(state_hgrn), w, wb)
    outs_s = _trunk(x_sample, mod[:, bp:bp + bs], state_rwkv, state_rwkv_shift, state_ssm,
                    state_conv, state_hgrn, w, wb)
    return (outs_p[0], outs_s[0]) + outs_p[1:] + outs_s[1:]


def kernel(x_prompt, x_sample, state_rwkv, state_rwkv_shift, state_ssm, state_conv, state_hgrn, c_prompt, c_sample, norm_mix_w, norm_ffn_w, norm_out_w, ada_w, ada_b, w_in_ab, w_out_ab, mu_a, w0, w2, a0, a2, g2, k_k, k_a, r_k, lnx_w, lnx_b, conv_w, conv_b, dt_bias, a_log, d_skip, norm_b_w, w_in_c, w_out_c, lb_param, norm_c_w, w_gate, w_up, w_down):
    w = dict(norm_mix_w=norm_mix_w, norm_ffn_w=norm_ffn_w, norm_out_w=norm_out_w, ada_w=ada_w, ada_b=ada_b,
             w_in_ab=w_in_ab, w_out_ab=w_out_ab, mu_a=mu_a, w0=w0, w2=w2, a0=a0, a2=a2, g2=g2, k_k=k_k,
             k_a=k_a, r_k=r_k, lnx_w=lnx_w, lnx_b=lnx_b, conv_w=conv_w, conv_b=conv_b, dt_bias=dt_bias,
             a_log=a_log, d_skip=d_skip, norm_b_w=norm_b_w, w_in_c=w_in_c, w_out_c=w_out_c,
             lb_param=lb_param, norm_c_w=norm_c_w, w_gate=w_gate, w_up=w_up, w_down=w_down)
    return _run(x_prompt, x_sample, state_rwkv, state_rwkv_shift, state_ssm, state_conv, state_hgrn,
                c_prompt, c_sample, w)
```

```python
import functools
import math

import numpy as np
import jax
import jax.numpy as jnp
from jax import lax
from jax.experimental import pallas as pl
from jax.experimental.pallas import tpu as pltpu

F32 = jnp.float32
BF16 = jnp.bfloat16

MXU_TILE = 256
VMEM_LIMIT = 62 * 2**20
ROW_TILE = 1024
COL_CHUNK = 512
RWKV_CHUNK = 64
RWKV_GROUP = 4
MAMBA_CHUNK = 128
HGRN_CHUNK = 64
HGRN_GROUP = 4
MIX_SEQS = 4
LNX_EPS = 64e-5
RMS_EPS = 1e-6
GROUP_RMS_EPS = 1e-5


def _bdot(a, b):
    return jnp.dot(a.astype(BF16), b.astype(BF16), preferred_element_type=F32)


def _bdot_nt(a, b):
    return lax.dot_general(a.astype(BF16), b.astype(BF16), (((1,), (1,)), ((), ())),
                           preferred_element_type=F32)


def _bdot_tn(a, b):
    return lax.dot_general(a.astype(BF16), b.astype(BF16), (((0,), (0,)), ((), ())),
                           preferred_element_type=F32)


def _silu(x):
    return x * jax.nn.sigmoid(x)


def _const_spec(shape, single_buffer=False):
    nd = len(shape)
    if single_buffer:
        return pl.BlockSpec(shape, lambda *_: (0,) * nd, pipeline_mode=pl.Buffered(1))
    return pl.BlockSpec(shape, lambda *_: (0,) * nd)


def _layer_spec(stacked, layer):
    nd = stacked.ndim
    return pl.BlockSpec((None,) + stacked.shape[1:], lambda *_: (layer,) + (0,) * (nd - 1),
                        pipeline_mode=pl.Buffered(1))


def _params(n_axes):
    return pltpu.CompilerParams(dimension_semantics=("arbitrary",) * n_axes,
                                vmem_limit_bytes=VMEM_LIMIT)


def _row_tiling(b, t):
    if t >= ROW_TILE:
        assert t % ROW_TILE == 0
        return 1, ROW_TILE
    bb = max(1, min(b, ROW_TILE // t))
    while b % bb:
        bb -= 1
    return bb, t


def _col_chunks(n):
    return [(n0, min(n0 + COL_CHUNK, n)) for n0 in range(0, n, COL_CHUNK)]


def _ada_body(c_ref, w_ref, b_ref, o_ref):
    o_ref[0] = _bdot(_silu(c_ref[...]), w_ref[0]) + b_ref[0]


def _ada_call(c_all, ada_w, ada_b):
    depth, d, n = ada_w.shape
    r = c_all.shape[0]
    tn = 1024
    assert n % tn == 0
    return pl.pallas_call(
        _ada_body,
        out_shape=jax.ShapeDtypeStruct((depth, r, n), F32),
        grid=(depth, n // tn),
        in_specs=[pl.BlockSpec((r, d), lambda l, j: (0, 0)),
                  pl.BlockSpec((1, d, tn), lambda l, j: (l, 0, j)),
                  pl.BlockSpec((1, 1, tn), lambda l, j: (l, 0, j))],
        out_specs=pl.BlockSpec((1, r, tn), lambda l, j: (l, 0, j)),
        compiler_params=_params(2),
        name="ada_mod",
    )(c_all, ada_w, ada_b.reshape(depth, 1, n))


def _norm_mod(x, nw, sc, sh):
    ms = jnp.mean(x * x, axis=-1, keepdims=True)
    y = x * lax.rsqrt(ms + RMS_EPS) * nw
    return y * (1.0 + sc) + sh


class _Ticker:
    def __init__(self, thunks, points):
        self.thunks, self.total, self.points, self.calls = list(thunks), len(thunks), points, 0

    def __call__(self):
        self.calls += 1
        due = min(self.total, -(-self.calls * self.total // self.points))
        while self.total - len(self.thunks) < due:
            self.thunks.pop(0)()

    def flush(self):
        while self.thunks:
            self.thunks.pop(0)()


def _fused_body(*refs, init, core, n_in, n_out, points, n_steps, col0, cols):
    x_cur_ref, x_next_ref, nw_ref, sc_ref, sh_ref, w_ref = refs[:6]
    ins = refs[6:6 + n_in]
    outs = refs[6 + n_in:6 + n_in + n_out]
    scr = refs[6 + n_in + n_out:-2]
    bufs = refs[-2:]
    bb, c, d = x_cur_ref.shape
    rows = bb * c
    j = pl.program_id(1)

    def projection(x_ref, dst_ref):
        h = _norm_mod(x_ref[...], nw_ref[...], sc_ref[...], sh_ref[...])
        hb = h.reshape(rows, d).astype(BF16)

        def piece(n0, n1):
            def run():
                dst_ref[:, n0:n1] = jnp.dot(hb, w_ref[:, col0 + n0:col0 + n1], preferred_element_type=F32)
            return run
        return [piece(n0, min(n0 + MXU_TILE, cols)) for n0 in range(0, cols, MXU_TILE)]

    @pl.when(j == 0)
    def _():
        init(ins, outs, scr)
        for run in projection(x_cur_ref, bufs[0]):
            run()

    for parity in range(2):
        @pl.when(lax.rem(j, 2) == parity)
        def _(parity=parity):
            ahead = projection(x_next_ref, bufs[1 - parity]) if n_steps > 1 else []
            tick = _Ticker(ahead, points)
            core(bufs[parity][...], ins, outs, scr, tick)
            tick.flush()


def _fused_call(name, init, core, points, x, nw, sc, sh, w_in, chunk, ins, in_specs, out_shapes, out_specs,
                scratch):
    b, t, d = x.shape
    bb, c = _mixer_tiling(b, t, chunk)
    n = t // c
    w_stack, layer, col0, cols = w_in
    seq_spec = pl.BlockSpec((bb, 1, d), lambda i, j: (i, 0, 0))
    return pl.pallas_call(
        functools.partial(_fused_body, init=init, core=core, n_in=len(ins), n_out=len(out_shapes), points=points,
                          n_steps=n, col0=col0, cols=cols),
        out_shape=out_shapes,
        grid=(b // bb, n),
        in_specs=[pl.BlockSpec((bb, c, d), lambda i, j: (i, j, 0)),
                  pl.BlockSpec((bb, c, d), lambda i, j: (i, jnp.minimum(j + 1, n - 1), 0)),
                  _const_spec((1, d)), seq_spec, seq_spec, _layer_spec(w_stack, layer)]
                 + in_specs(bb, c),
        out_specs=out_specs(bb, c),
        scratch_shapes=scratch(bb, c) + [pltpu.VMEM((bb * c, cols), F32)] * 2,
        compiler_params=_params(2),
        name=name,
    )(x, x, nw, sc, sh, w_stack, *ins)


def _post_body(*refs, n_mix, final):
    x_ref = refs[0]
    mix_refs = refs[1:1 + n_mix]
    (wout_ref, gm_ref, nw_ref, sc_ref, sh_ref, gf_ref, wg_ref, wu_ref, wd_ref) = refs[1 + n_mix:10 + n_mix]
    rest = refs[10 + n_mix:]
    if final:
        fw_ref, o_ref, act_ref = rest
    else:
        o_ref, act_ref = rest
    bb, tt, d = x_ref.shape
    rows = bb * tt

    mix = None
    off = 0
    for m_ref in mix_refs:
        wdt = m_ref.shape[-1]
        part = jnp.dot(m_ref[...].reshape(rows, wdt), wout_ref[off:off + wdt, :],
                       preferred_element_type=F32)
        mix = part if mix is None else mix + part
        off += wdt
    x1 = x_ref[...] + gm_ref[...] * mix.reshape(bb, tt, d)

    h = _norm_mod(x1, nw_ref[...], sc_ref[...], sh_ref[...])
    hb = h.reshape(rows, d).astype(BF16)
    for f0, f1 in _col_chunks(wg_ref.shape[1]):
        gate = jnp.dot(hb, wg_ref[:, f0:f1], preferred_element_type=F32)
        up = jnp.dot(hb, wu_ref[:, f0:f1], preferred_element_type=F32)
        act_ref[:, f0:f1] = (_silu(gate) * up).astype(BF16)
    ffn = jnp.dot(act_ref[...], wd_ref[...], preferred_element_type=F32)
    x2 = x1 + gf_ref[...] * ffn.reshape(bb, tt, d)
    if final:
        ms = jnp.mean(x2 * x2, axis=-1, keepdims=True)
        x2 = x2 * lax.rsqrt(ms + RMS_EPS) * fw_ref[...]
    o_ref[...] = x2


def _post_call(x, mixes, wout, gm, nw, sc, sh, gf, wg, wu, wd, final_w):
    b, t, d = x.shape
    bb, tt = _row_tiling(b, t)
    final = final_w is not None
    row_spec = lambda w: pl.BlockSpec((bb, tt, w), lambda i, j: (i, j, 0))
    seq_spec = pl.BlockSpec((bb, 1, d), lambda i, j: (i, 0, 0))
    in_specs = ([row_spec(d)] + [row_spec(m.shape[-1]) for m in mixes]
                + [_layer_spec(*wout), seq_spec, _const_spec((1, d)), seq_spec, seq_spec, seq_spec,
                   _layer_spec(*wg), _layer_spec(*wu), _layer_spec(*wd)])
    args = [x, *mixes, wout[0], gm, nw, sc, sh, gf, wg[0], wu[0], wd[0]]
    if final:
        in_specs.append(_const_spec((1, d)))
        args.append(final_w)
    return pl.pallas_call(
        functools.partial(_post_body, n_mix=len(mixes), final=final),
        out_shape=jax.ShapeDtypeStruct((b, t, d), F32),
        grid=(b // bb, t // tt),
        in_specs=in_specs,
        out_specs=row_spec(d),
        scratch_shapes=[pltpu.VMEM((bb * tt, wg[0].shape[2]), BF16)],
        compiler_params=_params(2),
        name="post_ffn",
    )(*args)


def _tri(c, strict=False, reps=1):
    row = lax.broadcasted_iota(jnp.int32, (c, reps * c), 0)
    col = lax.broadcasted_iota(jnp.int32, (c, reps * c), 1) & (c - 1)
    return (row > col) if strict else (row >= col)


def _block_tri(rows, c, upper=False):
    r = lax.broadcasted_iota(jnp.int32, (rows, rows), 0)
    q = lax.broadcasted_iota(jnp.int32, (rows, rows), 1)
    tri = (r <= q) if upper else (r >= q)
    return (tri & ((r ^ q) < c)).astype(BF16)


def _split3(x):
    x1 = x.astype(BF16)
    r1 = x - x1.astype(F32)
    x2 = r1.astype(BF16)
    x3 = (r1 - x2.astype(F32)).astype(BF16)
    return x1, x2, x3


def _split_dot(w01, x):
    x1, x2, x3 = _split3(x)
    dot = lambda piece: jnp.dot(w01, piece, preferred_element_type=F32)
    return dot(x1) + dot(x2) + dot(x3)


def _split_dot_tn(x, w01):
    x1, x2, x3 = _split3(x)
    dot = lambda piece: lax.dot_general(piece, w01, (((0,), (0,)), ((), ())), preferred_element_type=F32)
    return dot(x1) + dot(x2) + dot(x3)


def _expand_cols(x, width):
    k = x.shape[1]
    r = lax.broadcasted_iota(jnp.int32, (k, k * width), 0) * width
    q = lax.broadcasted_iota(jnp.int32, (k, k * width), 1)
    sel = ((q >= r) & (q < r + width)).astype(BF16)
    x1, x2, x3 = _split3(x)
    dot = lambda piece: jnp.dot(piece, sel, preferred_element_type=F32)
    return dot(x1) + dot(x2) + dot(x3)


def _group_sums(x, width):
    w = x.shape[1]
    tile = MXU_TILE if (w % MXU_TILE == 0 and MXU_TILE % width == 0) else w
    r = lax.broadcasted_iota(jnp.int32, (tile, tile), 0)
    q = lax.broadcasted_iota(jnp.int32, (tile, tile), 1)
    ones = ((r ^ q) < width).astype(BF16)
    pieces = _split3(x)
    cols = []
    for t0 in range(0, w, tile):
        acc = None
        for piece in pieces:
            part = jnp.dot(piece[:, t0:t0 + tile], ones, preferred_element_type=F32)
            acc = part if acc is None else acc + part
        cols.append(acc)
    return cols[0] if len(cols) == 1 else jnp.concatenate(cols, axis=1)


def _last_rows(x, bb, c):
    lasts = [x[(b + 1) * c - 1:(b + 1) * c, :] for b in range(bb)]
    tiled = [jnp.broadcast_to(l, (c, x.shape[1])) for l in lasts]
    return lasts, (tiled[0] if bb == 1 else jnp.concatenate(tiled, axis=0))


def _mixer_tiling(b, t, chunk):
    bb = max(n for n in range(1, MIX_SEQS + 1) if b % n == 0)
    return bb, math.gcd(chunk, t)


def _rwkv_init(ins, outs, scr):
    shift_ref, s0_ref = ins[:2]
    prev_ref, sbd_ref = scr[:2]
    bb, heads, hd = s0_ref.shape[:3]
    prev_ref[...] = shift_ref[...]
    zero = jnp.zeros((hd, hd), F32)
    for b in range(bb):
        for j in range(heads // 2):
            sbd_ref[b, j] = jnp.concatenate(
                [jnp.concatenate([s0_ref[b, 2 * j], zero], axis=1),
                 jnp.concatenate([zero, s0_ref[b, 2 * j + 1]], axis=1)], axis=0)


def _rwkv_core(p, ins, outs, scr, tick, *, heads, lora):
    (shift_ref, s0_ref, mu_ref, w0_ref, w2_ref, a0_ref, a2_ref, g2_ref,
     kk_ref, ka_ref, rk_ref, lw_ref, lb_ref) = ins
    o_ref, shift_out_ref, s_ref = outs
    prev_ref, sbd_ref, raw_ref = scr
    bb, cs, wa = o_ref.shape
    c = min(RWKV_CHUNK, cs)
    nsub = cs // c
    rows, cols = p.shape
    hd = s_ref.shape[-1]
    lw_, la_, lg_ = lora

    row = lax.broadcasted_iota(jnp.int32, (rows, cols), 0)
    p_prev = pltpu.roll(p, 1, axis=0)
    for b in range(bb):
        p_prev = jnp.where(row == b * cs, prev_ref[b], p_prev)
        last = p[(b + 1) * cs - 1:(b + 1) * cs, :]
        prev_ref[b] = last
        shift_out_ref[b] = last
    pm = p + (p_prev - p) * mu_ref[...]

    r = pm[:, 0:wa]
    k = pm[:, wa:2 * wa]
    v = pm[:, 2 * wa:3 * wa]
    o1 = 3 * wa
    xw = pm[:, o1:o1 + lw_]
    xa = pm[:, o1 + lw_:o1 + lw_ + la_]
    xg = pm[:, o1 + lw_ + la_:o1 + lw_ + la_ + lg_]

    w = -jax.nn.softplus(-(w0_ref[...] + _bdot(jnp.tanh(xw), w2_ref[...]))) - 0.5
    logd = -jnp.exp(w)
    tick()
    a = jax.nn.sigmoid(a0_ref[...] + _bdot(xa, a2_ref[...]))
    g = _bdot(jax.nn.sigmoid(xg), g2_ref[...])
    kk_raw = k * kk_ref[...]
    kk = kk_raw / jnp.maximum(jnp.sqrt(_group_sums(kk_raw * kk_raw, hd)), 1e-12)
    kka = kk * a
    k2 = k * (1.0 + (a - 1.0) * ka_ref[...])

    slab = min(rows, 2 * c)
    tri = _block_tri(slab, c)
    cums = [_split_dot(tri, logd[r0:r0 + slab]) for r0 in range(0, rows, slab)]
    cum = cums[0] if len(cums) == 1 else jnp.concatenate(cums, axis=0)
    tick()
    tots, tot_rows = _last_rows(cum, bb * nsub, c)
    p_inv = jnp.exp(-cum)
    p_end = jnp.exp(tot_rows - cum)
    at = (-kk * jnp.exp(cum - logd)).astype(BF16)
    rt = (r * jnp.exp(cum)).astype(BF16)
    bt = (kka * p_inv).astype(BF16)
    kt = (k2 * p_inv).astype(BF16)
    be = (kka * p_end).astype(BF16)
    ke = (k2 * p_end).astype(BF16)
    vb = v.astype(BF16)

    pw = 2 * hd
    iota = lambda shape, dim: lax.broadcasted_iota(jnp.int32, shape, dim)
    first_c = iota((c, 2 * c), 1) < c
    first_v = iota((1, pw), 1) < hd
    bd_cc = (iota((2 * c, 2 * c), 0) < c) == (iota((2 * c, 2 * c), 1) < c)
    bd_cv = (iota((2 * c, pw), 0) < c) == (iota((2 * c, pw), 1) < hd)
    bd_vv = (iota((pw, pw), 0) < hd) == (iota((pw, pw), 1) < hd)
    bd_cv4 = (iota((4 * c, pw), 0) < 2 * c) == (iota((4 * c, pw), 1) < hd)
    outer4 = (iota((4 * c, 1), 0) < c) | (iota((4 * c, 1), 0) >= 3 * c)
    strict2 = _tri(c, strict=True, reps=2)
    incl4 = _tri(c, reps=4)
    eye2 = ((iota((c, 2 * c), 1) & (c - 1)) == iota((c, 2 * c), 0)).astype(F32)
    stack2 = lambda m: jnp.concatenate([m, m], axis=0)
    bd2 = lambda m: jnp.where(bd_cv, stack2(m), 0)

    units = [(b, ci, j) for b in range(bb) for ci in range(nsub) for j in range(heads // 2)]
    idx = [(slice(b * cs + ci * c, b * cs + (ci + 1) * c), slice(j * pw, (j + 1) * pw)) for b, ci, j in units]
    xps = [jnp.concatenate([at[rs, sl], rt[rs, sl]], axis=0) for rs, sl in idx]
    ybds = [jnp.concatenate([jnp.where(first_v, jnp.concatenate([bt[rs, sl], kt[rs, sl]], axis=0), 0),
                             jnp.where(first_v, 0, jnp.concatenate([kt[rs, sl], bt[rs, sl]], axis=0))], axis=0)
            for rs, sl in idx]
    eps = [jnp.concatenate([be[rs, sl], ke[rs, sl]], axis=0) for rs, sl in idx]
    vps = [vb[rs, sl] for rs, sl in idx]
    tick()
    gps = [_bdot_nt(xp, ybd) for xp, ybd in zip(xps, ybds)]
    tick()
    a_ps = [jnp.where(strict2, jnp.where(first_c, gp[:c, 0:2 * c], gp[:c, 2 * c:4 * c]), 0.0) for gp in gps]
    k_ps = [jnp.where(strict2, jnp.where(first_c, gp[:c, 2 * c:4 * c], gp[:c, 0:2 * c]), 0.0) for gp in gps]
    akvs = [_bdot(k_p, jnp.where(bd_cv, 0, stack2(vp))) for k_p, vp in zip(k_ps, vps)]
    tick()
    ns = a_ps
    ts = [eye2 + n for n in ns]
    step = 2
    while step < c:
        nbds = [jnp.where(bd_cc, stack2(n.astype(BF16)), 0) for n in ns]
        ns = [_bdot(n, nbd) for n, nbd in zip(ns, nbds)]
        tick()
        ts = [t + _bdot(t, jnp.where(bd_cc, stack2(n.astype(BF16)), 0)) for t, n in zip(ts, ns)]
        tick()
        step *= 2
    tas = [_bdot(t, jnp.concatenate([bd2(xp[:c]), bd2(akv.astype(BF16))], axis=1))
           for t, xp, akv in zip(ts, xps, akvs)]
    tick()
    tabs = [(ta[:, :pw].astype(BF16), ta[:, pw:].astype(BF16)) for ta in tas]
    qos = [_bdot(jnp.where(incl4, gp[c:, :], 0.0),
                 jnp.concatenate([jnp.where(bd_cv4 & outer4, jnp.concatenate([ta, ta, ta, ta], axis=0), 0),
                                  jnp.where(bd_cv4, jnp.concatenate([tkv, vp, vp, tkv], axis=0), 0)], axis=1))
           for gp, (ta, tkv), vp in zip(gps, tabs, vps)]
    tick()
    qs = [(xp[c:].astype(F32) + qo[:, :pw]).astype(BF16) for xp, qo in zip(xps, qos)]
    mds = [_bdot_tn(jnp.concatenate([jnp.concatenate([ta, tkv], axis=1),
                                     jnp.concatenate([jnp.zeros_like(vp), vp], axis=1)], axis=0), ep)
           for (ta, tkv), vp, ep in zip(tabs, vps, eps)]
    tick()
    bd_vv2 = stack2(bd_vv)
    mds = [jnp.where(bd_vv2, md, 0.0) for md in mds]

    for b in range(bb):
        sbds = [sbd_ref[b, j] for j in range(heads // 2)]
        for ci in range(nsub):
            for j in range(heads // 2):
                i = units.index((b, ci, j))
                rs, sl = idx[i]
                sbd = sbds[j]
                raw_ref[b, ci * c:(ci + 1) * c, sl] = _bdot_nt(qs[i], sbd) + qos[i][:, pw:]
                sbds[j] = sbd * jnp.exp(tots[b * nsub + ci][:, sl]) + _bdot(sbd, mds[i][:pw]) + mds[i][pw:]
            tick()
        for j in range(heads // 2):
            sbd_ref[b, j] = sbds[j]

    @pl.when(pl.program_id(1) == pl.num_programs(1) - 1)
    def _():
        for b in range(bb):
            for j in range(heads // 2):
                s_ref[b, 2 * j] = sbd_ref[b, j, 0:hd, 0:hd]
                s_ref[b, 2 * j + 1] = sbd_ref[b, j, hd:pw, hd:pw]

    o = raw_ref[...].reshape(rows, wa)
    dev = o - _group_sums(o, hd) * (1.0 / hd)
    var = _group_sums(dev * dev, hd) * (1.0 / hd)
    o = dev * lax.rsqrt(var + LNX_EPS) * lw_ref[...] + lb_ref[...]
    tick()
    bonus = _group_sums(r * k2 * rk_ref[...], hd) * v
    o_ref[...] = ((o + bonus) * g).reshape(bb, cs, wa).astype(o_ref.dtype)


RWKV_POINTS = 24


def _rwkv_call(x, nw, sc, sh, w_in, shift_prev, s0, prm):
    b, t, _ = x.shape
    cols = w_in[3]
    heads, hd = s0.shape[1], s0.shape[2]
    wa = heads * hd
    lora = (prm["w2"].shape[0], prm["a2"].shape[0], prm["g2"].shape[0])
    vec = lambda a: a.reshape(1, -1)
    consts = [vec(prm["mu"]), vec(prm["w0"]), prm["w2"], vec(prm["a0"]), prm["a2"], prm["g2"],
              vec(prm["k_k"]), vec(prm["k_a"]), vec(prm["r_k"]), vec(prm["lnx_w"]), vec(prm["lnx_b"])]
    seq = lambda bb, shape: pl.BlockSpec((bb,) + shape, lambda i, j: (i,) + (0,) * len(shape))
    return _fused_call(
        "rwkv7_mixer", _rwkv_init, functools.partial(_rwkv_core, heads=heads, lora=lora), RWKV_POINTS,
        x, nw, sc, sh, w_in, RWKV_CHUNK * RWKV_GROUP,
        ins=[shift_prev.reshape(b, 1, cols), s0, *consts],
        in_specs=lambda bb, c: [seq(bb, (1, cols)), seq(bb, s0.shape[1:])] + [_const_spec(a.shape) for a in consts],
        out_shapes=[jax.ShapeDtypeStruct((b, t, wa), BF16),
                    jax.ShapeDtypeStruct((b, 1, cols), F32),
                    jax.ShapeDtypeStruct(s0.shape, F32)],
        out_specs=lambda bb, c: [pl.BlockSpec((bb, c, wa), lambda i, j: (i, j, 0)),
                                 seq(bb, (1, cols)), seq(bb, s0.shape[1:])],
        scratch=lambda bb, c: [pltpu.VMEM((bb, 1, cols), F32),
                               pltpu.VMEM((bb, heads // 2, 2 * hd, 2 * hd), F32),
                               pltpu.VMEM((bb, c, wa), F32)])


def _mamba_init(ins, outs, scr, *, conv_w):
    conv_ref, s0_ref = ins[:2]
    s_ref = outs[2]
    ubuf_ref = scr[0]
    bb, _, xbc = conv_ref.shape
    pad, hist = 8, conv_w - 1
    s_ref[...] = s0_ref[...]
    for b in range(bb):
        ubuf_ref[b, 0:pad, :] = jnp.zeros((pad, xbc), F32)
        ubuf_ref[b, pad - hist:pad, :] = conv_ref[b]


def _mamba_core(p, ins, outs, scr, tick, *, groups, conv_w):
    conv_ref, s0_ref, cw_ref, cb_ref, dtb_ref, alog_ref, dskip_ref, nw_ref = ins
    o_ref, conv_out_ref, s_ref = outs
    ubuf_ref, inter_ref, intra_ref = scr
    bb, c, wb = o_ref.shape
    rows = bb * c
    heads, hd, ns = s_ref.shape[1], s_ref.shape[2], s_ref.shape[3]
    xbc = conv_ref.shape[-1]
    hpg = heads // groups
    pad = 8
    hist = conv_w - 1

    ys = []
    for b in range(bb):
        u = p[b * c:(b + 1) * c, wb:wb + xbc]
        ext = jnp.concatenate([ubuf_ref[b], u], axis=0)
        y = cb_ref[...] + cw_ref[hist:hist + 1, :] * u
        for i in range(hist):
            y = y + cw_ref[i:i + 1, :] * pltpu.roll(ext, hist - i, axis=0)[pad:pad + c, :]
        conv_out_ref[b] = u[c - hist:c, :]
        ubuf_ref[b] = u[c - pad:c, :]
        ys.append(y)
        tick()
    xc = _silu(ys[0] if bb == 1 else jnp.concatenate(ys, axis=0))
    xs = xc[:, 0:wb]
    bm = xc[:, wb:wb + groups * ns]
    cm = xc[:, wb + groups * ns:wb + 2 * groups * ns]
    z = p[:, 0:wb]
    dt_raw = p[:, wb + xbc:wb + xbc + heads]
    dt = jax.nn.softplus(dt_raw + dtb_ref[...])
    la = dt * (-jnp.exp(alog_ref[...]))

    incl = _tri(c)
    acum = _split_dot(_block_tri(rows, c), la)
    acum_t = _split_dot_tn(la, _block_tri(rows, c, upper=True))
    a_lasts, a_last_rows = _last_rows(acum, bb, c)
    e_tots = [jnp.exp(al) for al in a_lasts]
    stack = jnp.concatenate([dt, jnp.exp(acum), jnp.exp(a_last_rows - acum),
                             jnp.broadcast_to(dskip_ref[...], (8, heads))], axis=0)
    full = _expand_cols(stack, hd)
    ecum_f = full[rows:2 * rows]
    dskip_f = full[3 * rows:3 * rows + 1]
    xd_f = xs * full[0:rows]
    xde_f = xd_f * full[2 * rows:3 * rows]
    colb = _expand_cols(acum, c)
    tick()

    rss = [slice(b * c, (b + 1) * c) for b in range(bb)]
    bgs = [[bm[rs, gi * ns:(gi + 1) * ns].astype(BF16) for gi in range(groups)] for rs in rss]
    cgs = [[cm[rs, gi * ns:(gi + 1) * ns].astype(BF16) for gi in range(groups)] for rs in rss]
    cbs = [[_bdot_nt(cg, bg) for cg, bg in zip(cgb, bgb)] for cgb, bgb in zip(cgs, bgs)]
    tick()
    pairs = [(b, h) for b in range(bb) for h in range(heads)]
    sls = [slice(h * hd, (h + 1) * hd) for _, h in pairs]
    s0s = [s_ref[b, h] for b, h in pairs]
    mats = []
    for b, h in pairs:
        seg = colb[rss[b], h * c:(h + 1) * c] - acum_t[h:h + 1, rss[b]]
        decay = jnp.where(incl, jnp.exp(jnp.where(incl, seg, 0.0)), 0.0)
        mats.append((cbs[b][h // hpg] * decay).astype(BF16))
        if h % 4 == 3:
            tick()
    intra = [_bdot(m, xd_f[rss[b], sl]) for m, (b, _), sl in zip(mats, pairs, sls)]
    tick()
    inter = [_bdot_nt(cgs[b][h // hpg], s0) for (b, h), s0 in zip(pairs, s0s)]
    tick()
    upd = [_bdot_tn(xde_f[rss[b], sl], bgs[b][h // hpg]) for (b, h), sl in zip(pairs, sls)]
    tick()
    for i, ((b, h), sl) in enumerate(zip(pairs, sls)):
        s_ref[b, h] = s0s[i] * e_tots[b][:, h:h + 1] + upd[i]
        intra_ref[b, :, sl] = intra[i]
        inter_ref[b, :, sl] = inter[i]

    y_all = intra_ref[...].reshape(rows, wb) + inter_ref[...].reshape(rows, wb) * ecum_f + dskip_f * xs
    yv = y_all * _silu(z)
    gw = wb // groups
    for gi in range(groups):
        sl = slice(gi * gw, (gi + 1) * gw)
        yg = yv[:, sl]
        ms = jnp.mean(yg * yg, axis=-1, keepdims=True)
        o_ref[:, :, sl] = (yg * lax.rsqrt(ms + GROUP_RMS_EPS) * nw_ref[:, sl]).reshape(bb, c, gw).astype(o_ref.dtype)


MAMBA_POINTS = 16


def _mamba_call(x, nw, sc, sh, w_in, conv_prev, s0, prm, groups):
    b, t, _ = x.shape
    heads, hd, ns = s0.shape[1:]
    wb = heads * hd
    hist, xbc = conv_prev.shape[1:]
    vec = lambda a: a.reshape(1, -1)
    consts = [prm["conv_w"], vec(prm["conv_b"]), vec(prm["dt_bias"]), vec(prm["a_log"]),
              vec(prm["d_skip"]), vec(prm["norm_b_w"])]
    seq = lambda bb, shape: pl.BlockSpec((bb,) + shape, lambda i, j: (i,) + (0,) * len(shape))
    return _fused_call(
        "mamba2_mixer", functools.partial(_mamba_init, conv_w=hist + 1),
        functools.partial(_mamba_core, groups=groups, conv_w=hist + 1), MAMBA_POINTS,
        x, nw, sc, sh, w_in, MAMBA_CHUNK,
        ins=[conv_prev, s0, *consts],
        in_specs=lambda bb, c: [seq(bb, (hist, xbc)), seq(bb, s0.shape[1:])] + [_const_spec(a.shape) for a in consts],
        out_shapes=[jax.ShapeDtypeStruct((b, t, wb), BF16),
                    jax.ShapeDtypeStruct(conv_prev.shape, F32),
                    jax.ShapeDtypeStruct(s0.shape, F32)],
        out_specs=lambda bb, c: [pl.BlockSpec((bb, c, wb), lambda i, j: (i, j, 0)),
                                 seq(bb, (hist, xbc)), seq(bb, s0.shape[1:])],
        scratch=lambda bb, c: [pltpu.VMEM((bb, 8, xbc), F32), pltpu.VMEM((bb, c, wb), F32),
                               pltpu.VMEM((bb, c, wb), F32)])


HGRN_FINE = 4


def _hgrn_sum_matrix(c):
    t = np.arange(c)[:, None]
    j = np.arange(c)[None, :]
    blocks = [(j <= t)]
    m = min(HGRN_FINE, c // 2)
    while m >= 1:
        mid = (t // (2 * m)) * (2 * m) + m - 1
        right = (t % (2 * m)) >= m
        blocks.append(np.where(right, (j > mid) & (j <= t), (j > t) & (j <= mid)))
        m //= 2
    return np.concatenate(blocks, axis=0).astype(np.float32)


def _hgrn_init(ins, outs, scr):
    s0_ref = ins[0]
    (st_ref,) = scr
    bb, heads = s0_ref.shape[:2]
    for b in range(bb):
        for h in range(heads):
            st_ref[b, h] = s0_ref[b, h].T


def _hgrn_core(p, ins, outs, scr, tick, *, layer):
    s0_ref, sums_ref, lbp_ref, nw_ref = ins
    o_ref, s_ref = outs
    (st_ref,) = scr
    bb, cs, wc = o_ref.shape
    c = min(HGRN_CHUNK, cs)
    nsub = cs // c
    rows, cols = p.shape
    heads, dk, dv = s0_ref.shape[1:]
    nchunks = pl.num_programs(1)

    lbp = lbp_ref[...]
    e = jnp.exp(lbp - jnp.max(lbp, axis=0, keepdims=True))
    soft = e / jnp.sum(e, axis=0, keepdims=True)
    lb = soft[0:1, :]
    for i in range(1, layer + 1):
        lb = lb + soft[i:i + 1, :]
    lb = lb - soft[0:1, :]

    q = _silu(p[:, 0:wc])
    f = p[:, wc:2 * wc]
    v = p[:, 2 * wc:3 * wc]
    gate = p[:, 3 * wc:4 * wc]
    log_f = jnp.log(lb + (1.0 - lb) * jax.nn.sigmoid(f))
    k = (1.0 - lb) * jax.nn.sigmoid(-f)
    qk = q * k

    row = lax.broadcasted_iota(jnp.int32, (c, 1), 0)
    rr = lax.broadcasted_iota(jnp.int32, (c, c), 0)
    cc = lax.broadcasted_iota(jnp.int32, (c, c), 1)
    eye = rr == cc
    slabs = [(b, ci) for b in range(bb) for ci in range(nsub)]
    rss = [slice(b * cs + ci * c, b * cs + (ci + 1) * c) for b, ci in slabs]
    qes, kes, f_tots, levels = [], [], [], []
    for rs in rss:
        sums = _split_dot(sums_ref[...], log_f[rs])
        bcum = sums[0:c]
        tot = bcum[c - 1:c, :]
        tick()
        qes.append((q[rs] * jnp.exp(bcum)).astype(BF16))
        kes.append((k[rs] * jnp.exp(tot - bcum)).astype(BF16))
        f_tots.append(jnp.exp(tot))
        lv = []
        m = c // 2
        i = 1
        while m >= 1:
            if m > HGRN_FINE:
                mids = [jnp.broadcast_to(bcum[r0 + m - 1:r0 + m, :], (2 * m, wc)) for r0 in range(0, c, 2 * m)]
                diff = bcum - (mids[0] if len(mids) == 1 else jnp.concatenate(mids, axis=0))
                z = jnp.where((row & (2 * m - 1)) >= m, diff, -diff)
            else:
                z = sums[i * c:(i + 1) * c]
                i += 1
            wgt = jnp.exp(z)
            lv.append(((q[rs] * wgt).astype(BF16), (k[rs] * wgt).astype(BF16),
                       (rr > cc) & ((rr ^ cc) >= m) & ((rr ^ cc) < 2 * m)))
            m //= 2
            tick()
        levels.append(lv)

    units = [(si, h) for si in range(len(slabs)) for h in range(heads)]
    sls = [slice(h * dk, (h + 1) * dk) for _, h in units]
    atts = [jnp.where(eye, jnp.sum(qk[rss[si], sl], axis=-1, keepdims=True), 0.0) for (si, _), sl in zip(units, sls)]
    for li in range(len(levels[0])):
        atts = [att + jnp.where(levels[si][li][2], _bdot_nt(levels[si][li][0][:, sl], levels[si][li][1][:, sl]), 0.0)
                for att, (si, _), sl in zip(atts, units, sls)]
        tick()
    vhs = [v[rss[si], sl].astype(BF16) for (si, _), sl in zip(units, sls)]
    intra = [_bdot(att, vh) for att, vh in zip(atts, vhs)]
    tick()
    upd = [_bdot_tn(vh, kes[si][:, sl]) for vh, (si, _), sl in zip(vhs, units, sls)]
    tick()

    for b in range(bb):
        sts = [st_ref[b, h] for h in range(heads)]
        for ci in range(nsub):
            si = b * nsub + ci
            for h in range(heads):
                i = si * heads + h
                sl = sls[i]
                o = intra[i] + _bdot_nt(qes[si][:, sl], sts[h])
                sts[h] = sts[h] * f_tots[si][:, sl] + upd[i]
                ms = jnp.mean(o * o, axis=-1, keepdims=True)
                o_ref[b, ci * c:(ci + 1) * c, sl] = (o * lax.rsqrt(ms + GROUP_RMS_EPS) * nw_ref[:, sl]
                                                     * _silu(gate[rss[si], sl])).astype(o_ref.dtype)
            tick()
        for h in range(heads):
            st_ref[b, h] = sts[h]

    @pl.when(pl.program_id(1) == nchunks - 1)
    def _():
        for b in range(bb):
            for h in range(heads):
                s_ref[b, h] = st_ref[b, h].T


HGRN_POINTS = 40


def _hgrn_call(x, nw, sc, sh, w_in, s0, lb_param, norm_w, layer):
    b, t, _ = x.shape
    heads, dk, dv = s0.shape[1:]
    assert dk == dv
    wc = heads * dk
    sums = jnp.asarray(_hgrn_sum_matrix(math.gcd(HGRN_CHUNK, t)), dtype=BF16)
    ins = [s0, sums, lb_param, norm_w.reshape(1, wc)]
    seq = lambda bb, shape: pl.BlockSpec((bb,) + shape, lambda i, j: (i,) + (0,) * len(shape))
    return _fused_call(
        "hgrn2_mixer", _hgrn_init, functools.partial(_hgrn_core, layer=layer), HGRN_POINTS,
        x, nw, sc, sh, w_in, HGRN_CHUNK * HGRN_GROUP,
        ins=ins,
        in_specs=lambda bb, c: [seq(bb, s0.shape[1:])] + [_const_spec(a.shape) for a in ins[1:]],
        out_shapes=[jax.ShapeDtypeStruct((b, t, wc), BF16), jax.ShapeDtypeStruct(s0.shape, F32)],
        out_specs=lambda bb, c: [pl.BlockSpec((bb, c, wc), lambda i, j: (i, j, 0)), seq(bb, s0.shape[1:])],
        scratch=lambda bb, c: [pltpu.VMEM((bb, heads, dv, dk), F32)])


def _round_up(n, m):
    return -(-n // m) * m


def _prepare_weights(w):
    names = ("w_in_ab", "w_out_ab", "w_in_c", "w_out_c", "w_gate", "w_up", "w_down")
    return {n: w[n].astype(BF16) for n in names}


def _trunk(x, mod, st_rwkv, st_shift, st_ssm, st_conv, st_hgrn, w, wb):
    depth = mod.shape[0]
    b, t, d = x.shape
    groups = (st_conv.shape[-1] - st_ssm.shape[2] * st_ssm.shape[3]) // (2 * st_ssm.shape[4])
    new_rwkv, new_shift, new_ssm, new_conv, new_hgrn = [], [], [], [], []
    for layer in range(depth):
        j = layer // 2
        sh_m, sc_m, g_m, sh_f, sc_f, g_f = (mod[layer, :, None, i * d:(i + 1) * d] for i in range(6))
        nw_mix = w["norm_mix_w"][layer].reshape(1, d)
        nw_ffn = w["norm_ffn_w"][layer].reshape(1, d)
        if layer % 2 == 0:
            prm = dict(mu=w["mu_a"][j], w0=w["w0"][j], w2=w["w2"][j], a0=w["a0"][j], a2=w["a2"][j],
                       g2=w["g2"][j], k_k=w["k_k"][j], k_a=w["k_a"][j], r_k=w["r_k"][j],
                       lnx_w=w["lnx_w"][j], lnx_b=w["lnx_b"][j])
            a_cols = w["mu_a"].shape[1]
            oa, shift_new, rwkv_new = _rwkv_call(x, nw_mix, sc_m, sh_m, (wb["w_in_ab"], j, 0, a_cols),
                                                 st_shift[:, j], st_rwkv[:, j], prm)
            prm_b = dict(conv_w=w["conv_w"][j], conv_b=w["conv_b"][j], dt_bias=w["dt_bias"][j],
                         a_log=w["a_log"][j], d_skip=w["d_skip"][j], norm_b_w=w["norm_b_w"][j])
            b_in = (wb["w_in_ab"], j, a_cols, w["w_in_ab"].shape[2] - a_cols)
            ob, conv_new, ssm_new = _mamba_call(x, nw_mix, sc_m, sh_m, b_in,
                                                st_conv[:, j], st_ssm[:, j], prm_b, groups)
            mixes, wout = [oa, ob], (wb["w_out_ab"], j)
            new_rwkv.append(rwkv_new)
            new_shift.append(shift_new[:, 0])
            new_ssm.append(ssm_new)
            new_conv.append(conv_new)
        else:
            oc, hgrn_new = _hgrn_call(x, nw_mix, sc_m, sh_m, (wb["w_in_c"], j, 0, w["w_in_c"].shape[2]),
                                      st_hgrn[:, j], w["lb_param"], w["norm_c_w"][j], j)
            mixes, wout = [oc], (wb["w_out_c"], j)
            new_hgrn.append(hgrn_new)
        final_w = w["norm_out_w"].reshape(1, d) if layer == depth - 1 else None
        x = _post_call(x, mixes, wout, g_m, nw_ffn, sc_f, sh_f, g_f,
                       (wb["w_gate"], layer), (wb["w_up"], layer), (wb["w_down"], layer), final_w)
    return (x, jnp.stack(new_rwkv, axis=1), jnp.stack(new_shift, axis=1), jnp.stack(new_ssm, axis=1),
            jnp.stack(new_conv, axis=1), jnp.stack(new_hgrn, axis=1))


def _run(x_prompt, x_sample, state_rwkv, state_rwkv_shift, state_ssm, state_conv, state_hgrn,
         c_prompt, c_sample, w):
    bp, bs = x_prompt.shape[0], x_sample.shape[0]
    rows = _round_up(bp + bs, 8)
    c_all = jnp.pad(jnp.concatenate([c_prompt, c_sample], axis=0), ((0, rows - bp - bs), (0, 0)))
    mod = _ada_call(c_all, w["ada_w"], w["ada_b"])
    wb = _prepare_weights(w)
    zeros = lambda s: jnp.zeros((bp,) + s.shape[1:], F32)
    outs_p = _trunk(x_prompt, mod[:, :bp], zeros(state_rwkv), zeros(state_rwkv_shift), zeros(state_ssm),
                    zeros(state_conv), zeros(state_hgrn), w, wb)
    outs_s = _trunk(x_sample, mod[:, bp:bp + bs], state_rwkv, state_rwkv_shift, state_ssm,
                    state_conv, state_hgrn, w, wb)
    return (outs_p[0], outs_s[0]) + outs_p[1:] + outs_s[1:]


def kernel(x_prompt, x_sample, state_rwkv, state_rwkv_shift, state_ssm, state_conv, state_hgrn, c_prompt, c_sample, norm_mix_w, norm_ffn_w, norm_out_w, ada_w, ada_b, w_in_ab, w_out_ab, mu_a, w0, w2, a0, a2, g2, k_k, k_a, r_k, lnx_w, lnx_b, conv_w, conv_b, dt_bias, a_log, d_skip, norm_b_w, w_in_c, w_out_c, lb_param, norm_c_w, w_gate, w_up, w_down):
    w = dict(norm_mix_w=norm_mix_w, norm_ffn_w=norm_ffn_w, norm_out_w=norm_out_w, ada_w=ada_w, ada_b=ada_b,
             w_in_ab=w_in_ab, w_out_ab=w_out_ab, mu_a=mu_a, w0=w0, w2=w2, a0=a0, a2=a2, g2=g2, k_k=k_k,
             k_a=k_a, r_k=r_k, lnx_w=lnx_w, lnx_b=lnx_b, conv_w=conv_w, conv_b=conv_b, dt_bias=dt_bias,
             a_log=a_log, d_skip=d_skip, norm_b_w=norm_b_w, w_in_c=w_in_c, w_out_c=w_out_c,
             lb_param=lb_param, norm_c_w=norm_c_w, w_gate=w_gate, w_up=w_up, w_down=w_down)
    return _run(x_prompt, x_sample, state_rwkv, state_rwkv_shift, state_ssm, state_conv, state_hgrn,
                c_prompt, c_sample, w)
```

```python
import functools
import math

import numpy as np
import jax
import jax.numpy as jnp
from jax import lax
from jax.experimental import pallas as pl
from jax.experimental.pallas import tpu as pltpu

F32 = jnp.float32
BF16 = jnp.bfloat16

MXU_TILE = 256
VMEM_LIMIT = 56 * 2**20
ROW_TILE = 1024
COL_CHUNK = 512
RWKV_CHUNK = 64
RWKV_GROUP = 4
MAMBA_CHUNK = 128
HGRN_CHUNK = 64
HGRN_GROUP = 2
MIX_SEQS = 4
LNX_EPS = 64e-5
RMS_EPS = 1e-6
GROUP_RMS_EPS = 1e-5


def _bdot(a, b):
    return jnp.dot(a.astype(BF16), b.astype(BF16), preferred_element_type=F32)


def _bdot_nt(a, b):
    return lax.dot_general(a.astype(BF16), b.astype(BF16), (((1,), (1,)), ((), ())),
                           preferred_element_type=F32)


def _bdot_tn(a, b):
    return lax.dot_general(a.astype(BF16), b.astype(BF16), (((0,), (0,)), ((), ())),
                           preferred_element_type=F32)


def _silu(x):
    return x * jax.nn.sigmoid(x)


def _const_spec(shape, single_buffer=False):
    nd = len(shape)
    if single_buffer:
        return pl.BlockSpec(shape, lambda *_: (0,) * nd, pipeline_mode=pl.Buffered(1))
    return pl.BlockSpec(shape, lambda *_: (0,) * nd)


def _layer_spec(stacked, layer):
    nd = stacked.ndim
    return pl.BlockSpec((None,) + stacked.shape[1:], lambda *_: (layer,) + (0,) * (nd - 1),
                        pipeline_mode=pl.Buffered(1))


def _params(n_axes):
    return pltpu.CompilerParams(dimension_semantics=("arbitrary",) * n_axes,
                                vmem_limit_bytes=VMEM_LIMIT)


def _row_tiling(b, t):
    if t >= ROW_TILE:
        assert t % ROW_TILE == 0
        return 1, ROW_TILE
    bb = max(1, min(b, ROW_TILE // t))
    while b % bb:
        bb -= 1
    return bb, t


def _col_chunks(n):
    return [(n0, min(n0 + COL_CHUNK, n)) for n0 in range(0, n, COL_CHUNK)]


def _ada_body(c_ref, w_ref, b_ref, o_ref):
    o_ref[0] = _bdot(_silu(c_ref[...]), w_ref[0]) + b_ref[0]


def _ada_call(c_all, ada_w, ada_b):
    depth, d, n = ada_w.shape
    r = c_all.shape[0]
    tn = 1024
    assert n % tn == 0
    return pl.pallas_call(
        _ada_body,
        out_shape=jax.ShapeDtypeStruct((depth, r, n), F32),
        grid=(depth, n // tn),
        in_specs=[pl.BlockSpec((r, d), lambda l, j: (0, 0)),
                  pl.BlockSpec((1, d, tn), lambda l, j: (l, 0, j)),
                  pl.BlockSpec((1, 1, tn), lambda l, j: (l, 0, j))],
        out_specs=pl.BlockSpec((1, r, tn), lambda l, j: (l, 0, j)),
        compiler_params=_params(2),
        name="ada_mod",
    )(c_all, ada_w, ada_b.reshape(depth, 1, n))


def _norm_mod(x, nw, sc, sh):
    ms = jnp.mean(x * x, axis=-1, keepdims=True)
    y = x * lax.rsqrt(ms + RMS_EPS) * nw
    return y * (1.0 + sc) + sh


class _Ticker:
    def __init__(self, thunks, points):
        self.thunks, self.total, self.points, self.calls = list(thunks), len(thunks), points, 0

    def __call__(self):
        self.calls += 1
        due = min(self.total, -(-self.calls * self.total // self.points))
        while self.total - len(self.thunks) < due:
            self.thunks.pop(0)()

    def flush(self):
        while self.thunks:
            self.thunks.pop(0)()


def _fused_body(*refs, init, core, n_in, n_out, points, n_steps, col0, cols):
    x_cur_ref, x_next_ref, nw_ref, sc_ref, sh_ref, w_ref = refs[:6]
    ins = refs[6:6 + n_in]
    outs = refs[6 + n_in:6 + n_in + n_out]
    scr = refs[6 + n_in + n_out:-2]
    bufs = refs[-2:]
    bb, c, d = x_cur_ref.shape
    rows = bb * c
    j = pl.program_id(1)

    def projection(x_ref, dst_ref):
        h = _norm_mod(x_ref[...], nw_ref[...], sc_ref[...], sh_ref[...])
        hb = h.reshape(rows, d).astype(BF16)

        def piece(n0, n1):
            def run():
                dst_ref[:, n0:n1] = jnp.dot(hb, w_ref[:, col0 + n0:col0 + n1], preferred_element_type=F32)
            return run
        return [piece(n0, min(n0 + MXU_TILE, cols)) for n0 in range(0, cols, MXU_TILE)]

    @pl.when(j == 0)
    def _():
        init(ins, outs, scr)
        for run in projection(x_cur_ref, bufs[0]):
            run()

    for parity in range(2):
        @pl.when(lax.rem(j, 2) == parity)
        def _(parity=parity):
            ahead = projection(x_next_ref, bufs[1 - parity]) if n_steps > 1 else []
            tick = _Ticker(ahead, points)
            core(bufs[parity][...], ins, outs, scr, tick)
            tick.flush()


def _fused_call(name, init, core, points, x, nw, sc, sh, w_in, chunk, ins, in_specs, out_shapes, out_specs,
                scratch):
    b, t, d = x.shape
    bb, c = _mixer_tiling(b, t, chunk)
    n = t // c
    w_stack, layer, col0, cols = w_in
    seq_spec = pl.BlockSpec((bb, 1, d), lambda i, j: (i, 0, 0))
    return pl.pallas_call(
        functools.partial(_fused_body, init=init, core=core, n_in=len(ins), n_out=len(out_shapes), points=points,
                          n_steps=n, col0=col0, cols=cols),
        out_shape=out_shapes,
        grid=(b // bb, n),
        in_specs=[pl.BlockSpec((bb, c, d), lambda i, j: (i, j, 0)),
                  pl.BlockSpec((bb, c, d), lambda i, j: (i, jnp.minimum(j + 1, n - 1), 0)),
                  _const_spec((1, d)), seq_spec, seq_spec, _layer_spec(w_stack, layer)]
                 + in_specs(bb, c),
        out_specs=out_specs(bb, c),
        scratch_shapes=scratch(bb, c) + [pltpu.VMEM((bb * c, cols), F32)] * 2,
        compiler_params=_params(2),
        name=name,
    )(x, x, nw, sc, sh, w_stack, *ins)


def _post_body(*refs, n_mix, final):
    x_ref = refs[0]
    mix_refs = refs[1:1 + n_mix]
    (wout_ref, gm_ref, nw_ref, sc_ref, sh_ref, gf_ref, wg_ref, wu_ref, wd_ref) = refs[1 + n_mix:10 + n_mix]
    rest = refs[10 + n_mix:]
    if final:
        fw_ref, o_ref, act_ref = rest
    else:
        o_ref, act_ref = rest
    bb, tt, d = x_ref.shape
    rows = bb * tt

    mix = None
    off = 0
    for m_ref in mix_refs:
        wdt = m_ref.shape[-1]
        part = jnp.dot(m_ref[...].reshape(rows, wdt), wout_ref[off:off + wdt, :],
                       preferred_element_type=F32)
        mix = part if mix is None else mix + part
        off += wdt
    x1 = x_ref[...] + gm_ref[...] * mix.reshape(bb, tt, d)

    h = _norm_mod(x1, nw_ref[...], sc_ref[...], sh_ref[...])
    hb = h.reshape(rows, d).astype(BF16)
    for f0, f1 in _col_chunks(wg_ref.shape[1]):
        gate = jnp.dot(hb, wg_ref[:, f0:f1], preferred_element_type=F32)
        up = jnp.dot(hb, wu_ref[:, f0:f1], preferred_element_type=F32)
        act_ref[:, f0:f1] = (_silu(gate) * up).astype(BF16)
    ffn = jnp.dot(act_ref[...], wd_ref[...], preferred_element_type=F32)
    x2 = x1 + gf_ref[...] * ffn.reshape(bb, tt, d)
    if final:
        ms = jnp.mean(x2 * x2, axis=-1, keepdims=True)
        x2 = x2 * lax.rsqrt(ms + RMS_EPS) * fw_ref[...]
    o_ref[...] = x2


def _post_call(x, mixes, wout, gm, nw, sc, sh, gf, wg, wu, wd, final_w):
    b, t, d = x.shape
    bb, tt = _row_tiling(b, t)
    final = final_w is not None
    row_spec = lambda w: pl.BlockSpec((bb, tt, w), lambda i, j: (i, j, 0))
    seq_spec = pl.BlockSpec((bb, 1, d), lambda i, j: (i, 0, 0))
    in_specs = ([row_spec(d)] + [row_spec(m.shape[-1]) for m in mixes]
                + [_layer_spec(*wout), seq_spec, _const_spec((1, d)), seq_spec, seq_spec, seq_spec,
                   _layer_spec(*wg), _layer_spec(*wu), _layer_spec(*wd)])
    args = [x, *mixes, wout[0], gm, nw, sc, sh, gf, wg[0], wu[0], wd[0]]
    if final:
        in_specs.append(_const_spec((1, d)))
        args.append(final_w)
    return pl.pallas_call(
        functools.partial(_post_body, n_mix=len(mixes), final=final),
        out_shape=jax.ShapeDtypeStruct((b, t, d), F32),
        grid=(b // bb, t // tt),
        in_specs=in_specs,
        out_specs=row_spec(d),
        scratch_shapes=[pltpu.VMEM((bb * tt, wg[0].shape[2]), BF16)],
        compiler_params=_params(2),
        name="post_ffn",
    )(*args)


def _tri(c, strict=False, reps=1):
    row = lax.broadcasted_iota(jnp.int32, (c, reps * c), 0)
    col = lax.broadcasted_iota(jnp.int32, (c, reps * c), 1) & (c - 1)
    return (row > col) if strict else (row >= col)


def _block_tri(rows, c, upper=False):
    r = lax.broadcasted_iota(jnp.int32, (rows, rows), 0)
    q = lax.broadcasted_iota(jnp.int32, (rows, rows), 1)
    tri = (r <= q) if upper else (r >= q)
    return (tri & ((r ^ q) < c)).astype(BF16)


def _split3(x):
    x1 = x.astype(BF16)
    r1 = x - x1.astype(F32)
    x2 = r1.astype(BF16)
    x3 = (r1 - x2.astype(F32)).astype(BF16)
    return x1, x2, x3


def _split_dot(w01, x):
    x1, x2, x3 = _split3(x)
    dot = lambda piece: jnp.dot(w01, piece, preferred_element_type=F32)
    return dot(x1) + dot(x2) + dot(x3)


def _split_dot_tn(x, w01):
    x1, x2, x3 = _split3(x)
    dot = lambda piece: lax.dot_general(piece, w01, (((0,), (0,)), ((), ())), preferred_element_type=F32)
    return dot(x1) + dot(x2) + dot(x3)


def _expand_cols(x, width):
    k = x.shape[1]
    r = lax.broadcasted_iota(jnp.int32, (k, k * width), 0) * width
    q = lax.broadcasted_iota(jnp.int32, (k, k * width), 1)
    sel = ((q >= r) & (q < r + width)).astype(BF16)
    x1, x2, x3 = _split3(x)
    dot = lambda piece: jnp.dot(piece, sel, preferred_element_type=F32)
    return dot(x1) + dot(x2) + dot(x3)


def _group_sums(x, width):
    w = x.shape[1]
    tile = MXU_TILE if (w % MXU_TILE == 0 and MXU_TILE % width == 0) else w
    r = lax.broadcasted_iota(jnp.int32, (tile, tile), 0)
    q = lax.broadcasted_iota(jnp.int32, (tile, tile), 1)
    ones = ((r ^ q) < width).astype(BF16)
    pieces = _split3(x)
    cols = []
    for t0 in range(0, w, tile):
        acc = None
        for piece in pieces:
            part = jnp.dot(piece[:, t0:t0 + tile], ones, preferred_element_type=F32)
            acc = part if acc is None else acc + part
        cols.append(acc)
    return cols[0] if len(cols) == 1 else jnp.concatenate(cols, axis=1)


def _last_rows(x, bb, c):
    lasts = [x[(b + 1) * c - 1:(b + 1) * c, :] for b in range(bb)]
    tiled = [jnp.broadcast_to(l, (c, x.shape[1])) for l in lasts]
    return lasts, (tiled[0] if bb == 1 else jnp.concatenate(tiled, axis=0))


def _mixer_tiling(b, t, chunk):
    bb = max(n for n in range(1, MIX_SEQS + 1) if b % n == 0)
    return bb, math.gcd(chunk, t)


def _rwkv_init(ins, outs, scr):
    shift_ref, s0_ref = ins[:2]
    prev_ref, sbd_ref = scr[:2]
    bb, heads, hd = s0_ref.shape[:3]
    prev_ref[...] = shift_ref[...]
    zero = jnp.zeros((hd, hd), F32)
    for b in range(bb):
        for j in range(heads // 2):
            sbd_ref[b, j] = jnp.concatenate(
                [jnp.concatenate([s0_ref[b, 2 * j], zero], axis=1),
                 jnp.concatenate([zero, s0_ref[b, 2 * j + 1]], axis=1)], axis=0)


def _rwkv_core(p, ins, outs, scr, tick, *, heads, lora):
    (shift_ref, s0_ref, mu_ref, w0_ref, w2_ref, a0_ref, a2_ref, g2_ref,
     kk_ref, ka_ref, rk_ref, lw_ref, lb_ref) = ins
    o_ref, shift_out_ref, s_ref = outs
    prev_ref, sbd_ref, raw_ref = scr
    bb, cs, wa = o_ref.shape
    c = min(RWKV_CHUNK, cs)
    nsub = cs // c
    rows, cols = p.shape
    hd = s_ref.shape[-1]
    lw_, la_, lg_ = lora

    row = lax.broadcasted_iota(jnp.int32, (rows, cols), 0)
    p_prev = pltpu.roll(p, 1, axis=0)
    for b in range(bb):
        p_prev = jnp.where(row == b * cs, prev_ref[b], p_prev)
        last = p[(b + 1) * cs - 1:(b + 1) * cs, :]
        prev_ref[b] = last
        shift_out_ref[b] = last
    pm = p + (p_prev - p) * mu_ref[...]

    r = pm[:, 0:wa]
    k = pm[:, wa:2 * wa]
    v = pm[:, 2 * wa:3 * wa]
    o1 = 3 * wa
    xw = pm[:, o1:o1 + lw_]
    xa = pm[:, o1 + lw_:o1 + lw_ + la_]
    xg = pm[:, o1 + lw_ + la_:o1 + lw_ + la_ + lg_]

    w = -jax.nn.softplus(-(w0_ref[...] + _bdot(jnp.tanh(xw), w2_ref[...]))) - 0.5
    logd = -jnp.exp(w)
    tick()
    a = jax.nn.sigmoid(a0_ref[...] + _bdot(xa, a2_ref[...]))
    g = _bdot(jax.nn.sigmoid(xg), g2_ref[...])
    kk_raw = k * kk_ref[...]
    kk = kk_raw / jnp.maximum(jnp.sqrt(_group_sums(kk_raw * kk_raw, hd)), 1e-12)
    kka = kk * a
    k2 = k * (1.0 + (a - 1.0) * ka_ref[...])

    slab = min(rows, 2 * c)
    tri = _block_tri(slab, c)
    cums = [_split_dot(tri, logd[r0:r0 + slab]) for r0 in range(0, rows, slab)]
    cum = cums[0] if len(cums) == 1 else jnp.concatenate(cums, axis=0)
    tick()
    tots, tot_rows = _last_rows(cum, bb * nsub, c)
    p_inv = jnp.exp(-cum)
    p_end = jnp.exp(tot_rows - cum)
    at = (-kk * jnp.exp(cum - logd)).astype(BF16)
    rt = (r * jnp.exp(cum)).astype(BF16)
    bt = (kka * p_inv).astype(BF16)
    kt = (k2 * p_inv).astype(BF16)
    be = (kka * p_end).astype(BF16)
    ke = (k2 * p_end).astype(BF16)
    vb = v.astype(BF16)

    pw = 2 * hd
    iota = lambda shape, dim: lax.broadcasted_iota(jnp.int32, shape, dim)
    first_c = iota((c, 2 * c), 1) < c
    first_v = iota((1, pw), 1) < hd
    bd_cc = (iota((2 * c, 2 * c), 0) < c) == (iota((2 * c, 2 * c), 1) < c)
    bd_cv = (iota((2 * c, pw), 0) < c) == (iota((2 * c, pw), 1) < hd)
    bd_vv = (iota((pw, pw), 0) < hd) == (iota((pw, pw), 1) < hd)
    bd_cv4 = (iota((4 * c, pw), 0) < 2 * c) == (iota((4 * c, pw), 1) < hd)
    outer4 = (iota((4 * c, 1), 0) < c) | (iota((4 * c, 1), 0) >= 3 * c)
    strict2 = _tri(c, strict=True, reps=2)
    incl4 = _tri(c, reps=4)
    eye2 = ((iota((c, 2 * c), 1) & (c - 1)) == iota((c, 2 * c), 0)).astype(F32)
    stack2 = lambda m: jnp.concatenate([m, m], axis=0)
    bd2 = lambda m: jnp.where(bd_cv, stack2(m), 0)

    units = [(b, ci, j) for b in range(bb) for ci in range(nsub) for j in range(heads // 2)]
    idx = [(slice(b * cs + ci * c, b * cs + (ci + 1) * c), slice(j * pw, (j + 1) * pw)) for b, ci, j in units]
    xps = [jnp.concatenate([at[rs, sl], rt[rs, sl]], axis=0) for rs, sl in idx]
    ybds = [jnp.concatenate([jnp.where(first_v, jnp.concatenate([bt[rs, sl], kt[rs, sl]], axis=0), 0),
                             jnp.where(first_v, 0, jnp.concatenate([kt[rs, sl], bt[rs, sl]], axis=0))], axis=0)
            for rs, sl in idx]
    eps = [jnp.concatenate([be[rs, sl], ke[rs, sl]], axis=0) for rs, sl in idx]
    vps = [vb[rs, sl] for rs, sl in idx]
    tick()
    gps = [_bdot_nt(xp, ybd) for xp, ybd in zip(xps, ybds)]
    tick()
    a_ps = [jnp.where(strict2, jnp.where(first_c, gp[:c, 0:2 * c], gp[:c, 2 * c:4 * c]), 0.0) for gp in gps]
    k_ps = [jnp.where(strict2, jnp.where(first_c, gp[:c, 2 * c:4 * c], gp[:c, 0:2 * c]), 0.0) for gp in gps]
    akvs = [_bdot(k_p, jnp.where(bd_cv, 0, stack2(vp))) for k_p, vp in zip(k_ps, vps)]
    tick()
    ns = a_ps
    ts = [eye2 + n for n in ns]
    step = 2
    while step < c:
        nbds = [jnp.where(bd_cc, stack2(n.astype(BF16)), 0) for n in ns]
        ns = [_bdot(n, nbd) for n, nbd in zip(ns, nbds)]
        tick()
        ts = [t + _bdot(t, jnp.where(bd_cc, stack2(n.astype(BF16)), 0)) for t, n in zip(ts, ns)]
        tick()
        step *= 2
    tas = [_bdot(t, jnp.concatenate([bd2(xp[:c]), bd2(akv.astype(BF16))], axis=1))
           for t, xp, akv in zip(ts, xps, akvs)]
    tick()
    tabs = [(ta[:, :pw].astype(BF16), ta[:, pw:].astype(BF16)) for ta in tas]
    qos = [_bdot(jnp.where(incl4, gp[c:, :], 0.0),
                 jnp.concatenate([jnp.where(bd_cv4 & outer4, jnp.concatenate([ta, ta, ta, ta], axis=0), 0),
                                  jnp.where(bd_cv4, jnp.concatenate([tkv, vp, vp, tkv], axis=0), 0)], axis=1))
           for gp, (ta, tkv), vp in zip(gps, tabs, vps)]
    tick()
    qs = [(xp[c:].astype(F32) + qo[:, :pw]).astype(BF16) for xp, qo in zip(xps, qos)]
    mds = [_bdot_tn(jnp.concatenate([jnp.concatenate([ta, tkv], axis=1),
                                     jnp.concatenate([jnp.zeros_like(vp), vp], axis=1)], axis=0), ep)
           for (ta, tkv), vp, ep in zip(tabs, vps, eps)]
    tick()
    bd_vv2 = stack2(bd_vv)
    mds = [jnp.where(bd_vv2, md, 0.0) for md in mds]

    for b in range(bb):
        sbds = [sbd_ref[b, j] for j in range(heads // 2)]
        for ci in range(nsub):
            for j in range(heads // 2):
                i = units.index((b, ci, j))
                rs, sl = idx[i]
                sbd = sbds[j]
                raw_ref[b, ci * c:(ci + 1) * c, sl] = _bdot_nt(qs[i], sbd) + qos[i][:, pw:]
                sbds[j] = sbd * jnp.exp(tots[b * nsub + ci][:, sl]) + _bdot(sbd, mds[i][:pw]) + mds[i][pw:]
            tick()
        for j in range(heads // 2):
            sbd_ref[b, j] = sbds[j]

    @pl.when(pl.program_id(1) == pl.num_programs(1) - 1)
    def _():
        for b in range(bb):
            for j in range(heads // 2):
                s_ref[b, 2 * j] = sbd_ref[b, j, 0:hd, 0:hd]
                s_ref[b, 2 * j + 1] = sbd_ref[b, j, hd:pw, hd:pw]

    o = raw_ref[...].reshape(rows, wa)
    dev = o - _group_sums(o, hd) * (1.0 / hd)
    var = _group_sums(dev * dev, hd) * (1.0 / hd)
    o = dev * lax.rsqrt(var + LNX_EPS) * lw_ref[...] + lb_ref[...]
    tick()
    bonus = _group_sums(r * k2 * rk_ref[...], hd) * v
    o_ref[...] = ((o + bonus) * g).reshape(bb, cs, wa).astype(o_ref.dtype)


RWKV_POINTS = 1


def _rwkv_call(x, nw, sc, sh, w_in, shift_prev, s0, prm):
    b, t, _ = x.shape
    cols = w_in[3]
    heads, hd = s0.shape[1], s0.shape[2]
    wa = heads * hd
    lora = (prm["w2"].shape[0], prm["a2"].shape[0], prm["g2"].shape[0])
    vec = lambda a: a.reshape(1, -1)
    consts = [vec(prm["mu"]), vec(prm["w0"]), prm["w2"], vec(prm["a0"]), prm["a2"], prm["g2"],
              vec(prm["k_k"]), vec(prm["k_a"]), vec(prm["r_k"]), vec(prm["lnx_w"]), vec(prm["lnx_b"])]
    seq = lambda bb, shape: pl.BlockSpec((bb,) + shape, lambda i, j: (i,) + (0,) * len(shape))
    return _fused_call(
        "rwkv7_mixer", _rwkv_init, functools.partial(_rwkv_core, heads=heads, lora=lora), RWKV_POINTS,
        x, nw, sc, sh, w_in, RWKV_CHUNK * RWKV_GROUP,
        ins=[shift_prev.reshape(b, 1, cols), s0, *consts],
        in_specs=lambda bb, c: [seq(bb, (1, cols)), seq(bb, s0.shape[1:])] + [_const_spec(a.shape) for a in consts],
        out_shapes=[jax.ShapeDtypeStruct((b, t, wa), BF16),
                    jax.ShapeDtypeStruct((b, 1, cols), F32),
                    jax.ShapeDtypeStruct(s0.shape, F32)],
        out_specs=lambda bb, c: [pl.BlockSpec((bb, c, wa), lambda i, j: (i, j, 0)),
                                 seq(bb, (1, cols)), seq(bb, s0.shape[1:])],
        scratch=lambda bb, c: [pltpu.VMEM((bb, 1, cols), F32),
                               pltpu.VMEM((bb, heads // 2, 2 * hd, 2 * hd), F32),
                               pltpu.VMEM((bb, c, wa), F32)])


def _mamba_init(ins, outs, scr, *, conv_w):
    conv_ref, s0_ref = ins[:2]
    s_ref = outs[2]
    ubuf_ref = scr[0]
    bb, _, xbc = conv_ref.shape
    pad, hist = 8, conv_w - 1
    s_ref[...] = s0_ref[...]
    for b in range(bb):
        ubuf_ref[b, 0:pad, :] = jnp.zeros((pad, xbc), F32)
        ubuf_ref[b, pad - hist:pad, :] = conv_ref[b]


def _mamba_core(p, ins, outs, scr, tick, *, groups, conv_w):
    conv_ref, s0_ref, cw_ref, cb_ref, dtb_ref, alog_ref, dskip_ref, nw_ref = ins
    o_ref, conv_out_ref, s_ref = outs
    ubuf_ref, inter_ref, intra_ref = scr
    bb, c, wb = o_ref.shape
    rows = bb * c
    heads, hd, ns = s_ref.shape[1], s_ref.shape[2], s_ref.shape[3]
    xbc = conv_ref.shape[-1]
    hpg = heads // groups
    pad = 8
    hist = conv_w - 1

    ys = []
    for b in range(bb):
        u = p[b * c:(b + 1) * c, wb:wb + xbc]
        ext = jnp.concatenate([ubuf_ref[b], u], axis=0)
        y = cb_ref[...] + cw_ref[hist:hist + 1, :] * u
        for i in range(hist):
            y = y + cw_ref[i:i + 1, :] * pltpu.roll(ext, hist - i, axis=0)[pad:pad + c, :]
        conv_out_ref[b] = u[c - hist:c, :]
        ubuf_ref[b] = u[c - pad:c, :]
        ys.append(y)
        tick()
    xc = _silu(ys[0] if bb == 1 else jnp.concatenate(ys, axis=0))
    xs = xc[:, 0:wb]
    bm = xc[:, wb:wb + groups * ns]
    cm = xc[:, wb + groups * ns:wb + 2 * groups * ns]
    z = p[:, 0:wb]
    dt_raw = p[:, wb + xbc:wb + xbc + heads]
    dt = jax.nn.softplus(dt_raw + dtb_ref[...])
    la = dt * (-jnp.exp(alog_ref[...]))

    incl = _tri(c)
    acum = _split_dot(_block_tri(rows, c), la)
    acum_t = _split_dot_tn(la, _block_tri(rows, c, upper=True))
    a_lasts, a_last_rows = _last_rows(acum, bb, c)
    e_tots = [jnp.exp(al) for al in a_lasts]
    stack = jnp.concatenate([dt, jnp.exp(acum), jnp.exp(a_last_rows - acum),
                             jnp.broadcast_to(dskip_ref[...], (8, heads))], axis=0)
    full = _expand_cols(stack, hd)
    ecum_f = full[rows:2 * rows]
    dskip_f = full[3 * rows:3 * rows + 1]
    xd_f = xs * full[0:rows]
    xde_f = xd_f * full[2 * rows:3 * rows]
    colb = _expand_cols(acum, c)
    tick()

    rss = [slice(b * c, (b + 1) * c) for b in range(bb)]
    bgs = [[bm[rs, gi * ns:(gi + 1) * ns].astype(BF16) for gi in range(groups)] for rs in rss]
    cgs = [[cm[rs, gi * ns:(gi + 1) * ns].astype(BF16) for gi in range(groups)] for rs in rss]
    cbs = [[_bdot_nt(cg, bg) for cg, bg in zip(cgb, bgb)] for cgb, bgb in zip(cgs, bgs)]
    tick()
    pairs = [(b, h) for b in range(bb) for h in range(heads)]
    sls = [slice(h * hd, (h + 1) * hd) for _, h in pairs]
    s0s = [s_ref[b, h] for b, h in pairs]
    mats = []
    for b, h in pairs:
        seg = colb[rss[b], h * c:(h + 1) * c] - acum_t[h:h + 1, rss[b]]
        decay = jnp.where(incl, jnp.exp(jnp.where(incl, seg, 0.0)), 0.0)
        mats.append((cbs[b][h // hpg] * decay).astype(BF16))
        if h % 4 == 3:
            tick()
    intra = [_bdot(m, xd_f[rss[b], sl]) for m, (b, _), sl in zip(mats, pairs, sls)]
    tick()
    inter = [_bdot_nt(cgs[b][h // hpg], s0) for (b, h), s0 in zip(pairs, s0s)]
    tick()
    upd = [_bdot_tn(xde_f[rss[b], sl], bgs[b][h // hpg]) for (b, h), sl in zip(pairs, sls)]
    tick()
    for i, ((b, h), sl) in enumerate(zip(pairs, sls)):
        s_ref[b, h] = s0s[i] * e_tots[b][:, h:h + 1] + upd[i]
        intra_ref[b, :, sl] = intra[i]
        inter_ref[b, :, sl] = inter[i]

    y_all = intra_ref[...].reshape(rows, wb) + inter_ref[...].reshape(rows, wb) * ecum_f + dskip_f * xs
    yv = y_all * _silu(z)
    gw = wb // groups
    for gi in range(groups):
        sl = slice(gi * gw, (gi + 1) * gw)
        yg = yv[:, sl]
        ms = jnp.mean(yg * yg, axis=-1, keepdims=True)
        o_ref[:, :, sl] = (yg * lax.rsqrt(ms + GROUP_RMS_EPS) * nw_ref[:, sl]).reshape(bb, c, gw).astype(o_ref.dtype)


MAMBA_POINTS = 16


def _mamba_call(x, nw, sc, sh, w_in, conv_prev, s0, prm, groups):
    b, t, _ = x.shape
    heads, hd, ns = s0.shape[1:]
    wb = heads * hd
    hist, xbc = conv_prev.shape[1:]
    vec = lambda a: a.reshape(1, -1)
    consts = [prm["conv_w"], vec(prm["conv_b"]), vec(prm["dt_bias"]), vec(prm["a_log"]),
              vec(prm["d_skip"]), vec(prm["norm_b_w"])]
    seq = lambda bb, shape: pl.BlockSpec((bb,) + shape, lambda i, j: (i,) + (0,) * len(shape))
    return _fused_call(
        "mamba2_mixer", functools.partial(_mamba_init, conv_w=hist + 1),
        functools.partial(_mamba_core, groups=groups, conv_w=hist + 1), MAMBA_POINTS,
        x, nw, sc, sh, w_in, MAMBA_CHUNK,
        ins=[conv_prev, s0, *consts],
        in_specs=lambda bb, c: [seq(bb, (hist, xbc)), seq(bb, s0.shape[1:])] + [_const_spec(a.shape) for a in consts],
        out_shapes=[jax.ShapeDtypeStruct((b, t, wb), BF16),
                    jax.ShapeDtypeStruct(conv_prev.shape, F32),
                    jax.ShapeDtypeStruct(s0.shape, F32)],
        out_specs=lambda bb, c: [pl.BlockSpec((bb, c, wb), lambda i, j: (i, j, 0)),
                                 seq(bb, (hist, xbc)), seq(bb, s0.shape[1:])],
        scratch=lambda bb, c: [pltpu.VMEM((bb, 8, xbc), F32), pltpu.VMEM((bb, c, wb), F32),
                               pltpu.VMEM((bb, c, wb), F32)])


HGRN_FINE = 4


def _hgrn_sum_matrix(c):
    t = np.arange(c)[:, None]
    j = np.arange(c)[None, :]
    blocks = [(j <= t)]
    m = min(HGRN_FINE, c // 2)
    while m >= 1:
        mid = (t // (2 * m)) * (2 * m) + m - 1
        right = (t % (2 * m)) >= m
        blocks.append(np.where(right, (j > mid) & (j <= t), (j > t) & (j <= mid)))
        m //= 2
    return np.concatenate(blocks, axis=0).astype(np.float32)


def _hgrn_init(ins, outs, scr):
    s0_ref = ins[0]
    (st_ref,) = scr
    bb, heads = s0_ref.shape[:2]
    for b in range(bb):
        for h in range(heads):
            st_ref[b, h] = s0_ref[b, h].T


def _hgrn_core(p, ins, outs, scr, tick, *, layer):
    s0_ref, sums_ref, lbp_ref, nw_ref = ins
    o_ref, s_ref = outs
    (st_ref,) = scr
    bb, cs, wc = o_ref.shape
    c = min(HGRN_CHUNK, cs)
    nsub = cs // c
    rows, cols = p.shape
    heads, dk, dv = s0_ref.shape[1:]
    nchunks = pl.num_programs(1)

    lbp = lbp_ref[...]
    e = jnp.exp(lbp - jnp.max(lbp, axis=0, keepdims=True))
    soft = e / jnp.sum(e, axis=0, keepdims=True)
    lb = soft[0:1, :]
    for i in range(1, layer + 1):
        lb = lb + soft[i:i + 1, :]
    lb = lb - soft[0:1, :]

    q = _silu(p[:, 0:wc])
    f = p[:, wc:2 * wc]
    v = p[:, 2 * wc:3 * wc]
    gate = p[:, 3 * wc:4 * wc]
    log_f = jnp.log(lb + (1.0 - lb) * jax.nn.sigmoid(f))
    k = (1.0 - lb) * jax.nn.sigmoid(-f)
    qk = q * k

    row = lax.broadcasted_iota(jnp.int32, (c, 1), 0)
    rr = lax.broadcasted_iota(jnp.int32, (c, c), 0)
    cc = lax.broadcasted_iota(jnp.int32, (c, c), 1)
    eye = rr == cc
    slabs = [(b, ci) for b in range(bb) for ci in range(nsub)]
    rss = [slice(b * cs + ci * c, b * cs + (ci + 1) * c) for b, ci in slabs]
    qes, kes, f_tots, levels = [], [], [], []
    for rs in rss:
        sums = _split_dot(sums_ref[...], log_f[rs])
        bcum = sums[0:c]
        tot = bcum[c - 1:c, :]
        tick()
        qes.append((q[rs] * jnp.exp(bcum)).astype(BF16))
        kes.append((k[rs] * jnp.exp(tot - bcum)).astype(BF16))
        f_tots.append(jnp.exp(tot))
        lv = []
        m = c // 2
        i = 1
        while m >= 1:
            if m > HGRN_FINE:
                mids = [jnp.broadcast_to(bcum[r0 + m - 1:r0 + m, :], (2 * m, wc)) for r0 in range(0, c, 2 * m)]
                diff = bcum - (mids[0] if len(mids) == 1 else jnp.concatenate(mids, axis=0))
                z = jnp.where((row & (2 * m - 1)) >= m, diff, -diff)
            else:
                z = sums[i * c:(i + 1) * c]
                i += 1
            wgt = jnp.exp(z)
            lv.append(((q[rs] * wgt).astype(BF16), (k[rs] * wgt).astype(BF16),
                       (rr > cc) & ((rr ^ cc) >= m) & ((rr ^ cc) < 2 * m)))
            m //= 2
            tick()
        levels.append(lv)

    units = [(si, h) for si in range(len(slabs)) for h in range(heads)]
    sls = [slice(h * dk, (h + 1) * dk) for _, h in units]
    atts = [jnp.where(eye, jnp.sum(qk[rss[si], sl], axis=-1, keepdims=True), 0.0) for (si, _), sl in zip(units, sls)]
    for li in range(len(levels[0])):
        atts = [att + jnp.where(levels[si][li][2], _bdot_nt(levels[si][li][0][:, sl], levels[si][li][1][:, sl]), 0.0)
                for att, (si, _), sl in zip(atts, units, sls)]
        tick()
    vhs = [v[rss[si], sl].astype(BF16) for (si, _), sl in zip(units, sls)]
    intra = [_bdot(att, vh) for att, vh in zip(atts, vhs)]
    tick()
    upd = [_bdot_tn(vh, kes[si][:, sl]) for vh, (si, _), sl in zip(vhs, units, sls)]
    tick()

    for b in range(bb):
        sts = [st_ref[b, h] for h in range(heads)]
        for ci in range(nsub):
            si = b * nsub + ci
            for h in range(heads):
                i = si * heads + h
                sl = sls[i]
                o = intra[i] + _bdot_nt(qes[si][:, sl], sts[h])
                sts[h] = sts[h] * f_tots[si][:, sl] + upd[i]
                ms = jnp.mean(o * o, axis=-1, keepdims=True)
                o_ref[b, ci * c:(ci + 1) * c, sl] = (o * lax.rsqrt(ms + GROUP_RMS_EPS) * nw_ref[:, sl]
                                                     * _silu(gate[rss[si], sl])).astype(o_ref.dtype)
            tick()
        for h in range(heads):
            st_ref[b, h] = sts[h]

    @pl.when(pl.program_id(1) == nchunks - 1)
    def _():
        for b in range(bb):
            for h in range(heads):
                s_ref[b, h] = st_ref[b, h].T


HGRN_POINTS = 40


def _hgrn_call(x, nw, sc, sh, w_in, s0, lb_param, norm_w, layer):
    b, t, _ = x.shape
    heads, dk, dv = s0.shape[1:]
    assert dk == dv
    wc = heads * dk
    sums = jnp.asarray(_hgrn_sum_matrix(math.gcd(HGRN_CHUNK, t)), dtype=BF16)
    ins = [s0, sums, lb_param, norm_w.reshape(1, wc)]
    seq = lambda bb, shape: pl.BlockSpec((bb,) + shape, lambda i, j: (i,) + (0,) * len(shape))
    return _fused_call(
        "hgrn2_mixer", _hgrn_init, functools.partial(_hgrn_core, layer=layer), HGRN_POINTS,
        x, nw, sc, sh, w_in, HGRN_CHUNK * HGRN_GROUP,
        ins=ins,
        in_specs=lambda bb, c: [seq(bb, s0.shape[1:])] + [_const_spec(a.shape) for a in ins[1:]],
        out_shapes=[jax.ShapeDtypeStruct((b, t, wc), BF16), jax.ShapeDtypeStruct(s0.shape, F32)],
        out_specs=lambda bb, c: [pl.BlockSpec((bb, c, wc), lambda i, j: (i, j, 0)), seq(bb, s0.shape[1:])],
        scratch=lambda bb, c: [pltpu.VMEM((bb, heads, dv, dk), F32)])


def _round_up(n, m):
    return -(-n // m) * m


def _prepare_weights(w):
    names = ("w_in_ab", "w_out_ab", "w_in_c", "w_out_c", "w_gate", "w_up", "w_down")
    return {n: w[n].astype(BF16) for n in names}


def _trunk(x, mod, st_rwkv, st_shift, st_ssm, st_conv, st_hgrn, w, wb):
    depth = mod.shape[0]
    b, t, d = x.shape
    groups = (st_conv.shape[-1] - st_ssm.shape[2] * st_ssm.shape[3]) // (2 * st_ssm.shape[4])
    new_rwkv, new_shift, new_ssm, new_conv, new_hgrn = [], [], [], [], []
    for layer in range(depth):
        j = layer // 2
        sh_m, sc_m, g_m, sh_f, sc_f, g_f = (mod[layer, :, None, i * d:(i + 1) * d] for i in range(6))
        nw_mix = w["norm_mix_w"][layer].reshape(1, d)
        nw_ffn = w["norm_ffn_w"][layer].reshape(1, d)
        if layer % 2 == 0:
            prm = dict(mu=w["mu_a"][j], w0=w["w0"][j], w2=w["w2"][j], a0=w["a0"][j], a2=w["a2"][j],
                       g2=w["g2"][j], k_k=w["k_k"][j], k_a=w["k_a"][j], r_k=w["r_k"][j],
                       lnx_w=w["lnx_w"][j], lnx_b=w["lnx_b"][j])
            a_cols = w["mu_a"].shape[1]
            oa, shift_new, rwkv_new = _rwkv_call(x, nw_mix, sc_m, sh_m, (wb["w_in_ab"], j, 0, a_cols),
                                                 st_shift[:, j], st_rwkv[:, j], prm)
            prm_b = dict(conv_w=w["conv_w"][j], conv_b=w["conv_b"][j], dt_bias=w["dt_bias"][j],
                         a_log=w["a_log"][j], d_skip=w["d_skip"][j], norm_b_w=w["norm_b_w"][j])
            b_in = (wb["w_in_ab"], j, a_cols, w["w_in_ab"].shape[2] - a_cols)
            ob, conv_new, ssm_new = _mamba_call(x, nw_mix, sc_m, sh_m, b_in,
                                                st_conv[:, j], st_ssm[:, j], prm_b, groups)
            mixes, wout = [oa, ob], (wb["w_out_ab"], j)
            new_rwkv.append(rwkv_new)
            new_shift.append(shift_new[:, 0])
            new_ssm.append(ssm_new)
            new_conv.append(conv_new)
        else:
            oc, hgrn_new = _hgrn_call(x, nw_mix, sc_m, sh_m, (wb["w_in_c"], j, 0, w["w_in_c"].shape[2]),
                                      st_hgrn[:, j], w["lb_param"], w["norm_c_w"][j], j)
            mixes, wout = [oc], (wb["w_out_c"], j)
            new_hgrn.append(hgrn_new)
        final_w = w["norm_out_w"].reshape(1, d) if layer == depth - 1 else None
        x = _post_call(x, mixes, wout, g_m, nw_ffn, sc_f, sh_f, g_f,
                       (wb["w_gate"], layer), (wb["w_up"], layer), (wb["w_down"], layer), final_w)
    return (x, jnp.stack(new_rwkv, axis=1), jnp.stack(new_shift, axis=1), jnp.stack(new_ssm, axis=1),
            jnp.stack(new_conv, axis=1), jnp.stack(new_hgrn, axis=1))


def _run(x_prompt, x_sample, state_rwkv, state_rwkv_shift, state_ssm, state_conv, state_hgrn,
         c_prompt, c_sample, w):
    bp, bs = x_prompt.shape[0], x_sample.shape[0]
    rows = _round_up(bp + bs, 8)
    c_all = jnp.pad(jnp.concatenate([c_prompt, c_sample], axis=0), ((0, rows - bp - bs), (0, 0)))
    mod = _ada_call(c_all, w["ada_w"], w["ada_b"])
    wb = _prepare_weights(w)
    zeros = lambda s: jnp.zeros((bp,) + s.shape[1:], F32)
    outs_p = _trunk(x_prompt, mod[:, :bp], zeros(state_rwkv), zeros(state_rwkv_shift), zeros(state_ssm),
                    zeros(state_conv), zeros(state_hgrn), w, wb)
    outs_s = _trunk(x_sample, mod[:, bp:bp + bs], state_rwkv, state_rwkv_shift, state_ssm,
                    state_conv, state_hgrn, w, wb)
    return (outs_p[0], outs_s[0]) + outs_p[1:] + outs_s[1:]


def kernel(x_prompt, x_sample, state_rwkv, state_rwkv_shift, state_ssm, state_conv, state_hgrn, c_prompt, c_sample, norm_mix_w, norm_ffn_w, norm_out_w, ada_w, ada_b, w_in_ab, w_out_ab, mu_a, w0, w2, a0, a2, g2, k_k, k_a, r_k, lnx_w, lnx_b, conv_w, conv_b, dt_bias, a_log, d_skip, norm_b_w, w_in_c, w_out_c, lb_param, norm_c_w, w_gate, w_up, w_down):
    w = dict(norm_mix_w=norm_mix_w, norm_ffn_w=norm_ffn_w, norm_out_w=norm_out_w, ada_w=ada_w, ada_b=ada_b,
             w_in_ab=w_in_ab, w_out_ab=w_out_ab, mu_a=mu_a, w0=w0, w2=w2, a0=a0, a2=a2, g2=g2, k_k=k_k,
             k_a=k_a, r_k=r_k, lnx_w=lnx_w, lnx_b=lnx_b, conv_w=conv_w, conv_b=conv_b, dt_bias=dt_bias,
             a_log=a_log, d_skip=d_skip, norm_b_w=norm_b_w, w_in_c=w_in_c, w_out_c=w_out_c,
             lb_param=lb_param, norm_c_w=norm_c_w, w_gate=w_gate, w_up=w_up, w_down=w_down)
    return _run(x_prompt, x_sample, state_rwkv, state_rwkv_shift, state_ssm, state_conv, state_hgrn,
                c_prompt, c_sample, w)
```

```python
import functools
import math

import numpy as np
import jax
import jax.numpy as jnp
from jax import lax
from jax.experimental import pallas as pl
from jax.experimental.pallas import tpu as pltpu

F32 = jnp.float32
BF16 = jnp.bfloat16

MXU_TILE = 256
VMEM_LIMIT = 56 * 2**20
ROW_TILE = 1024
COL_CHUNK = 512
RWKV_CHUNK = 64
RWKV_GROUP = 4
MAMBA_CHUNK = 128
HGRN_CHUNK = 64
HGRN_GROUP = 2
MIX_SEQS = 4
LNX_EPS = 64e-5
RMS_EPS = 1e-6
GROUP_RMS_EPS = 1e-5


def _bdot(a, b):
    return jnp.dot(a.astype(BF16), b.astype(BF16), preferred_element_type=F32)


def _bdot_nt(a, b):
    return lax.dot_general(a.astype(BF16), b.astype(BF16), (((1,), (1,)), ((), ())),
                           preferred_element_type=F32)


def _bdot_tn(a, b):
    return lax.dot_general(a.astype(BF16), b.astype(BF16), (((0,), (0,)), ((), ())),
                           preferred_element_type=F32)


def _silu(x):
    return x * jax.nn.sigmoid(x)


def _const_spec(shape, single_buffer=False):
    nd = len(shape)
    if single_buffer:
        return pl.BlockSpec(shape, lambda *_: (0,) * nd, pipeline_mode=pl.Buffered(1))
    return pl.BlockSpec(shape, lambda *_: (0,) * nd)


def _layer_spec(stacked, layer):
    nd = stacked.ndim
    return pl.BlockSpec((None,) + stacked.shape[1:], lambda *_: (layer,) + (0,) * (nd - 1),
                        pipeline_mode=pl.Buffered(1))


def _params(n_axes):
    return pltpu.CompilerParams(dimension_semantics=("arbitrary",) * n_axes,
                                vmem_limit_bytes=VMEM_LIMIT)


def _row_tiling(b, t):
    if t >= ROW_TILE:
        assert t % ROW_TILE == 0
        return 1, ROW_TILE
    bb = max(1, min(b, ROW_TILE // t))
    while b % bb:
        bb -= 1
    return bb, t


def _col_chunks(n):
    return [(n0, min(n0 + COL_CHUNK, n)) for n0 in range(0, n, COL_CHUNK)]


def _ada_body(c_ref, w_ref, b_ref, o_ref):
    o_ref[0] = _bdot(_silu(c_ref[...]), w_ref[0]) + b_ref[0]


def _ada_call(c_all, ada_w, ada_b):
    depth, d, n = ada_w.shape
    r = c_all.shape[0]
    tn = 1024
    assert n % tn == 0
    return pl.pallas_call(
        _ada_body,
        out_shape=jax.ShapeDtypeStruct((depth, r, n), F32),
        grid=(depth, n // tn),
        in_specs=[pl.BlockSpec((r, d), lambda l, j: (0, 0)),
                  pl.BlockSpec((1, d, tn), lambda l, j: (l, 0, j)),
                  pl.BlockSpec((1, 1, tn), lambda l, j: (l, 0, j))],
        out_specs=pl.BlockSpec((1, r, tn), lambda l, j: (l, 0, j)),
        compiler_params=_params(2),
        name="ada_mod",
    )(c_all, ada_w, ada_b.reshape(depth, 1, n))


def _norm_mod(x, nw, sc, sh):
    ms = jnp.mean(x * x, axis=-1, keepdims=True)
    y = x * lax.rsqrt(ms + RMS_EPS) * nw
    return y * (1.0 + sc) + sh


class _Ticker:
    def __init__(self, thunks, points):
        self.thunks, self.total, self.points, self.calls = list(thunks), len(thunks), points, 0

    def __call__(self):
        self.calls += 1
        due = min(self.total, -(-self.calls * self.total // self.points))
        while self.total - len(self.thunks) < due:
            self.thunks.pop(0)()

    def flush(self):
        while self.thunks:
            self.thunks.pop(0)()


def _fused_body(*refs, init, core, n_in, n_out, points, n_steps, col0, cols):
    x_cur_ref, x_next_ref, nw_ref, sc_ref, sh_ref, w_ref = refs[:6]
    ins = refs[6:6 + n_in]
    outs = refs[6 + n_in:6 + n_in + n_out]
    scr = refs[6 + n_in + n_out:-2]
    bufs = refs[-2:]
    bb, c, d = x_cur_ref.shape
    rows = bb * c
    j = pl.program_id(1)

    def projection(x_ref, dst_ref):
        h = _norm_mod(x_ref[...], nw_ref[...], sc_ref[...], sh_ref[...])
        hb = h.reshape(rows, d).astype(BF16)

        def piece(n0, n1):
            def run():
                dst_ref[:, n0:n1] = jnp.dot(hb, w_ref[:, col0 + n0:col0 + n1], preferred_element_type=F32)
            return run
        return [piece(n0, min(n0 + MXU_TILE, cols)) for n0 in range(0, cols, MXU_TILE)]

    @pl.when(j == 0)
    def _():
        init(ins, outs, scr)
        for run in projection(x_cur_ref, bufs[0]):
            run()

    for parity in range(2):
        @pl.when(lax.rem(j, 2) == parity)
        def _(parity=parity):
            ahead = projection(x_next_ref, bufs[1 - parity]) if n_steps > 1 else []
            tick = _Ticker(ahead, points)
            core(bufs[parity][...], ins, outs, scr, tick)
            tick.flush()


def _fused_call(name, init, core, points, x, nw, sc, sh, w_in, chunk, ins, in_specs, out_shapes, out_specs,
                scratch):
    b, t, d = x.shape
    bb, c = _mixer_tiling(b, t, chunk)
    n = t // c
    w_stack, layer, col0, cols = w_in
    seq_spec = pl.BlockSpec((bb, 1, d), lambda i, j: (i, 0, 0))
    return pl.pallas_call(
        functools.partial(_fused_body, init=init, core=core, n_in=len(ins), n_out=len(out_shapes), points=points,
                          n_steps=n, col0=col0, cols=cols),
        out_shape=out_shapes,
        grid=(b // bb, n),
        in_specs=[pl.BlockSpec((bb, c, d), lambda i, j: (i, j, 0)),
                  pl.BlockSpec((bb, c, d), lambda i, j: (i, jnp.minimum(j + 1, n - 1), 0)),
                  _const_spec((1, d)), seq_spec, seq_spec, _layer_spec(w_stack, layer)]
                 + in_specs(bb, c),
        out_specs=out_specs(bb, c),
        scratch_shapes=scratch(bb, c) + [pltpu.VMEM((bb * c, cols), F32)] * 2,
        compiler_params=_params(2),
        name=name,
    )(x, x, nw, sc, sh, w_stack, *ins)


def _post_body(*refs, n_mix, final):
    x_ref = refs[0]
    mix_refs = refs[1:1 + n_mix]
    (wout_ref, gm_ref, nw_ref, sc_ref, sh_ref, gf_ref, wg_ref, wu_ref, wd_ref) = refs[1 + n_mix:10 + n_mix]
    rest = refs[10 + n_mix:]
    if final:
        fw_ref, o_ref, act_ref = rest
    else:
        o_ref, act_ref = rest
    bb, tt, d = x_ref.shape
    rows = bb * tt

    mix = None
    off = 0
    for m_ref in mix_refs:
        wdt = m_ref.shape[-1]
        part = jnp.dot(m_ref[...].reshape(rows, wdt), wout_ref[off:off + wdt, :],
                       preferred_element_type=F32)
        mix = part if mix is None else mix + part
        off += wdt
    x1 = x_ref[...] + gm_ref[...] * mix.reshape(bb, tt, d)

    h = _norm_mod(x1, nw_ref[...], sc_ref[...], sh_ref[...])
    hb = h.reshape(rows, d).astype(BF16)
    for f0, f1 in _col_chunks(wg_ref.shape[1]):
        gate = jnp.dot(hb, wg_ref[:, f0:f1], preferred_element_type=F32)
        up = jnp.dot(hb, wu_ref[:, f0:f1], preferred_element_type=F32)
        act_ref[:, f0:f1] = (_silu(gate) * up).astype(BF16)
    ffn = jnp.dot(act_ref[...], wd_ref[...], preferred_element_type=F32)
    x2 = x1 + gf_ref[...] * ffn.reshape(bb, tt, d)
    if final:
        ms = jnp.mean(x2 * x2, axis=-1, keepdims=True)
        x2 = x2 * lax.rsqrt(ms + RMS_EPS) * fw_ref[...]
    o_ref[...] = x2


def _post_call(x, mixes, wout, gm, nw, sc, sh, gf, wg, wu, wd, final_w):
    b, t, d = x.shape
    bb, tt = _row_tiling(b, t)
    final = final_w is not None
    row_spec = lambda w: pl.BlockSpec((bb, tt, w), lambda i, j: (i, j, 0))
    seq_spec = pl.BlockSpec((bb, 1, d), lambda i, j: (i, 0, 0))
    in_specs = ([row_spec(d)] + [row_spec(m.shape[-1]) for m in mixes]
                + [_layer_spec(*wout), seq_spec, _const_spec((1, d)), seq_spec, seq_spec, seq_spec,
                   _layer_spec(*wg), _layer_spec(*wu), _layer_spec(*wd)])
    args = [x, *mixes, wout[0], gm, nw, sc, sh, gf, wg[0], wu[0], wd[0]]
    if final:
        in_specs.append(_const_spec((1, d)))
        args.append(final_w)
    return pl.pallas_call(
        functools.partial(_post_body, n_mix=len(mixes), final=final),
        out_shape=jax.ShapeDtypeStruct((b, t, d), F32),
        grid=(b // bb, t // tt),
        in_specs=in_specs,
        out_specs=row_spec(d),
        scratch_shapes=[pltpu.VMEM((bb * tt, wg[0].shape[2]), BF16)],
        compiler_params=_params(2),
        name="post_ffn",
    )(*args)


def _tri(c, strict=False, reps=1):
    row = lax.broadcasted_iota(jnp.int32, (c, reps * c), 0)
    col = lax.broadcasted_iota(jnp.int32, (c, reps * c), 1) & (c - 1)
    return (row > col) if strict else (row >= col)


def _block_tri(rows, c, upper=False):
    r = lax.broadcasted_iota(jnp.int32, (rows, rows), 0)
    q = lax.broadcasted_iota(jnp.int32, (rows, rows), 1)
    tri = (r <= q) if upper else (r >= q)
    return (tri & ((r ^ q) < c)).astype(BF16)


def _split3(x):
    x1 = x.astype(BF16)
    r1 = x - x1.astype(F32)
    x2 = r1.astype(BF16)
    x3 = (r1 - x2.astype(F32)).astype(BF16)
    return x1, x2, x3


def _split_dot(w01, x):
    x1, x2, x3 = _split3(x)
    dot = lambda piece: jnp.dot(w01, piece, preferred_element_type=F32)
    return dot(x1) + dot(x2) + dot(x3)


def _split_dot_tn(x, w01):
    x1, x2, x3 = _split3(x)
    dot = lambda piece: lax.dot_general(piece, w01, (((0,), (0,)), ((), ())), preferred_element_type=F32)
    return dot(x1) + dot(x2) + dot(x3)


def _expand_cols(x, width):
    k = x.shape[1]
    r = lax.broadcasted_iota(jnp.int32, (k, k * width), 0) * width
    q = lax.broadcasted_iota(jnp.int32, (k, k * width), 1)
    sel = ((q >= r) & (q < r + width)).astype(BF16)
    x1, x2, x3 = _split3(x)
    dot = lambda piece: jnp.dot(piece, sel, preferred_element_type=F32)
    return dot(x1) + dot(x2) + dot(x3)


def _group_sums(x, width):
    w = x.shape[1]
    tile = MXU_TILE if (w % MXU_TILE == 0 and MXU_TILE % width == 0) else w
    r = lax.broadcasted_iota(jnp.int32, (tile, tile), 0)
    q = lax.broadcasted_iota(jnp.int32, (tile, tile), 1)
    ones = ((r ^ q) < width).astype(BF16)
    pieces = _split3(x)
    cols = []
    for t0 in range(0, w, tile):
        acc = None
        for piece in pieces:
            part = jnp.dot(piece[:, t0:t0 + tile], ones, preferred_element_type=F32)
            acc = part if acc is None else acc + part
        cols.append(acc)
    return cols[0] if len(cols) == 1 else jnp.concatenate(cols, axis=1)


def _last_rows(x, bb, c):
    lasts = [x[(b + 1) * c - 1:(b + 1) * c, :] for b in range(bb)]
    tiled = [jnp.broadcast_to(l, (c, x.shape[1])) for l in lasts]
    return lasts, (tiled[0] if bb == 1 else jnp.concatenate(tiled, axis=0))


def _mixer_tiling(b, t, chunk):
    bb = max(n for n in range(1, MIX_SEQS + 1) if b % n == 0)
    return bb, math.gcd(chunk, t)


def _rwkv_init(ins, outs, scr):
    shift_ref, s0_ref = ins[:2]
    prev_ref, sbd_ref = scr[:2]
    bb, heads, hd = s0_ref.shape[:3]
    prev_ref[...] = shift_ref[...]
    zero = jnp.zeros((hd, hd), F32)
    for b in range(bb):
        for j in range(heads // 2):
            sbd_ref[b, j] = jnp.concatenate(
                [jnp.concatenate([s0_ref[b, 2 * j], zero], axis=1),
                 jnp.concatenate([zero, s0_ref[b, 2 * j + 1]], axis=1)], axis=0)


def _rwkv_core(p, ins, outs, scr, tick, *, heads, lora):
    (shift_ref, s0_ref, mu_ref, w0_ref, w2_ref, a0_ref, a2_ref, g2_ref,
     kk_ref, ka_ref, rk_ref, lw_ref, lb_ref) = ins
    o_ref, shift_out_ref, s_ref = outs
    prev_ref, sbd_ref, raw_ref = scr
    bb, cs, wa = o_ref.shape
    c = min(RWKV_CHUNK, cs)
    nsub = cs // c
    rows, cols = p.shape
    hd = s_ref.shape[-1]
    lw_, la_, lg_ = lora

    row = lax.broadcasted_iota(jnp.int32, (rows, cols), 0)
    p_prev = pltpu.roll(p, 1, axis=0)
    for b in range(bb):
        p_prev = jnp.where(row == b * cs, prev_ref[b], p_prev)
        last = p[(b + 1) * cs - 1:(b + 1) * cs, :]
        prev_ref[b] = last
        shift_out_ref[b] = last
    pm = p + (p_prev - p) * mu_ref[...]

    r = pm[:, 0:wa]
    k = pm[:, wa:2 * wa]
    v = pm[:, 2 * wa:3 * wa]
    o1 = 3 * wa
    xw = pm[:, o1:o1 + lw_]
    xa = pm[:, o1 + lw_:o1 + lw_ + la_]
    xg = pm[:, o1 + lw_ + la_:o1 + lw_ + la_ + lg_]

    w = -jax.nn.softplus(-(w0_ref[...] + _bdot(jnp.tanh(xw), w2_ref[...]))) - 0.5
    logd = -jnp.exp(w)
    tick()
    a = jax.nn.sigmoid(a0_ref[...] + _bdot(xa, a2_ref[...]))
    g = _bdot(jax.nn.sigmoid(xg), g2_ref[...])
    kk_raw = k * kk_ref[...]
    kk = kk_raw / jnp.maximum(jnp.sqrt(_group_sums(kk_raw * kk_raw, hd)), 1e-12)
    kka = kk * a
    k2 = k * (1.0 + (a - 1.0) * ka_ref[...])

    slab = min(rows, 2 * c)
    tri = _block_tri(slab, c)
    cums = [_split_dot(tri, logd[r0:r0 + slab]) for r0 in range(0, rows, slab)]
    cum = cums[0] if len(cums) == 1 else jnp.concatenate(cums, axis=0)
    tick()
    tots, tot_rows = _last_rows(cum, bb * nsub, c)
    p_inv = jnp.exp(-cum)
    p_end = jnp.exp(tot_rows - cum)
    at = (-kk * jnp.exp(cum - logd)).astype(BF16)
    rt = (r * jnp.exp(cum)).astype(BF16)
    bt = (kka * p_inv).astype(BF16)
    kt = (k2 * p_inv).astype(BF16)
    be = (kka * p_end).astype(BF16)
    ke = (k2 * p_end).astype(BF16)
    vb = v.astype(BF16)

    pw = 2 * hd
    iota = lambda shape, dim: lax.broadcasted_iota(jnp.int32, shape, dim)
    first_c = iota((c, 2 * c), 1) < c
    first_v = iota((1, pw), 1) < hd
    bd_cc = (iota((2 * c, 2 * c), 0) < c) == (iota((2 * c, 2 * c), 1) < c)
    bd_cv = (iota((2 * c, pw), 0) < c) == (iota((2 * c, pw), 1) < hd)
    bd_vv = (iota((pw, pw), 0) < hd) == (iota((pw, pw), 1) < hd)
    bd_cv4 = (iota((4 * c, pw), 0) < 2 * c) == (iota((4 * c, pw), 1) < hd)
    outer4 = (iota((4 * c, 1), 0) < c) | (iota((4 * c, 1), 0) >= 3 * c)
    strict2 = _tri(c, strict=True, reps=2)
    incl4 = _tri(c, reps=4)
    eye2 = ((iota((c, 2 * c), 1) & (c - 1)) == iota((c, 2 * c), 0)).astype(F32)
    stack2 = lambda m: jnp.concatenate([m, m], axis=0)
    bd2 = lambda m: jnp.where(bd_cv, stack2(m), 0)

    units = [(b, ci, j) for b in range(bb) for ci in range(nsub) for j in range(heads // 2)]
    idx = [(slice(b * cs + ci * c, b * cs + (ci + 1) * c), slice(j * pw, (j + 1) * pw)) for b, ci, j in units]
    xps = [jnp.concatenate([at[rs, sl], rt[rs, sl]], axis=0) for rs, sl in idx]
    ybds = [jnp.concatenate([jnp.where(first_v, jnp.concatenate([bt[rs, sl], kt[rs, sl]], axis=0), 0),
                             jnp.where(first_v, 0, jnp.concatenate([kt[rs, sl], bt[rs, sl]], axis=0))], axis=0)
            for rs, sl in idx]
    eps = [jnp.concatenate([be[rs, sl], ke[rs, sl]], axis=0) for rs, sl in idx]
    vps = [vb[rs, sl] for rs, sl in idx]
    tick()
    gps = [_bdot_nt(xp, ybd) for xp, ybd in zip(xps, ybds)]
    tick()
    a_ps = [jnp.where(strict2, jnp.where(first_c, gp[:c, 0:2 * c], gp[:c, 2 * c:4 * c]), 0.0) for gp in gps]
    k_ps = [jnp.where(strict2, jnp.where(first_c, gp[:c, 2 * c:4 * c], gp[:c, 0:2 * c]), 0.0) for gp in gps]
    akvs = [_bdot(k_p, jnp.where(bd_cv, 0, stack2(vp))) for k_p, vp in zip(k_ps, vps)]
    tick()
    ns = a_ps
    ts = [eye2 + n for n in ns]
    step = 2
    while step < c:
        nbds = [jnp.where(bd_cc, stack2(n.astype(BF16)), 0) for n in ns]
        ns = [_bdot(n, nbd) for n, nbd in zip(ns, nbds)]
        tick()
        ts = [t + _bdot(t, jnp.where(bd_cc, stack2(n.astype(BF16)), 0)) for t, n in zip(ts, ns)]
        tick()
        step *= 2
    tas = [_bdot(t, jnp.concatenate([bd2(xp[:c]), bd2(akv.astype(BF16))], axis=1))
           for t, xp, akv in zip(ts, xps, akvs)]
    tick()
    tabs = [(ta[:, :pw].astype(BF16), ta[:, pw:].astype(BF16)) for ta in tas]
    qos = [_bdot(jnp.where(incl4, gp[c:, :], 0.0),
                 jnp.concatenate([jnp.where(bd_cv4 & outer4, jnp.concatenate([ta, ta, ta, ta], axis=0), 0),
                                  jnp.where(bd_cv4, jnp.concatenate([tkv, vp, vp, tkv], axis=0), 0)], axis=1))
           for gp, (ta, tkv), vp in zip(gps, tabs, vps)]
    tick()
    qs = [(xp[c:].astype(F32) + qo[:, :pw]).astype(BF16) for xp, qo in zip(xps, qos)]
    mds = [_bdot_tn(jnp.concatenate([jnp.concatenate([ta, tkv], axis=1),
                                     jnp.concatenate([jnp.zeros_like(vp), vp], axis=1)], axis=0), ep)
           for (ta, tkv), vp, ep in zip(tabs, vps, eps)]
    tick()
    bd_vv2 = stack2(bd_vv)
    mds = [jnp.where(bd_vv2, md, 0.0) for md in mds]

    for b in range(bb):
        sbds = [sbd_ref[b, j] for j in range(heads // 2)]
        for ci in range(nsub):
            for j in range(heads // 2):
                i = units.index((b, ci, j))
                rs, sl = idx[i]
                sbd = sbds[j]
                raw_ref[b, ci * c:(ci + 1) * c, sl] = _bdot_nt(qs[i], sbd) + qos[i][:, pw:]
                sbds[j] = sbd * jnp.exp(tots[b * nsub + ci][:, sl]) + _bdot(sbd, mds[i][:pw]) + mds[i][pw:]
            tick()
        for j in range(heads // 2):
            sbd_ref[b, j] = sbds[j]

    @pl.when(pl.program_id(1) == pl.num_programs(1) - 1)
    def _():
        for b in range(bb):
            for j in range(heads // 2):
                s_ref[b, 2 * j] = sbd_ref[b, j, 0:hd, 0:hd]
                s_ref[b, 2 * j + 1] = sbd_ref[b, j, hd:pw, hd:pw]

    o = raw_ref[...].reshape(rows, wa)
    dev = o - _group_sums(o, hd) * (1.0 / hd)
    var = _group_sums(dev * dev, hd) * (1.0 / hd)
    o = dev * lax.rsqrt(var + LNX_EPS) * lw_ref[...] + lb_ref[...]
    tick()
    bonus = _group_sums(r * k2 * rk_ref[...], hd) * v
    o_ref[...] = ((o + bonus) * g).reshape(bb, cs, wa).astype(o_ref.dtype)


RWKV_POINTS = 1


def _rwkv_call(x, nw, sc, sh, w_in, shift_prev, s0, prm):
    b, t, _ = x.shape
    cols = w_in[3]
    heads, hd = s0.shape[1], s0.shape[2]
    wa = heads * hd
    lora = (prm["w2"].shape[0], prm["a2"].shape[0], prm["g2"].shape[0])
    vec = lambda a: a.reshape(1, -1)
    consts = [vec(prm["mu"]), vec(prm["w0"]), prm["w2"], vec(prm["a0"]), prm["a2"], prm["g2"],
              vec(prm["k_k"]), vec(prm["k_a"]), vec(prm["r_k"]), vec(prm["lnx_w"]), vec(prm["lnx_b"])]
    seq = lambda bb, shape: pl.BlockSpec((bb,) + shape, lambda i, j: (i,) + (0,) * len(shape))
    return _fused_call(
        "rwkv7_mixer", _rwkv_init, functools.partial(_rwkv_core, heads=heads, lora=lora), RWKV_POINTS,
        x, nw, sc, sh, w_in, RWKV_CHUNK * RWKV_GROUP,
        ins=[shift_prev.reshape(b, 1, cols), s0, *consts],
        in_specs=lambda bb, c: [seq(bb, (1, cols)), seq(bb, s0.shape[1:])] + [_const_spec(a.shape) for a in consts],
        out_shapes=[jax.ShapeDtypeStruct((b, t, wa), BF16),
                    jax.ShapeDtypeStruct((b, 1, cols), F32),
                    jax.ShapeDtypeStruct(s0.shape, F32)],
        out_specs=lambda bb, c: [pl.BlockSpec((bb, c, wa), lambda i, j: (i, j, 0)),
                                 seq(bb, (1, cols)), seq(bb, s0.shape[1:])],
        scratch=lambda bb, c: [pltpu.VMEM((bb, 1, cols), F32),
                               pltpu.VMEM((bb, heads // 2, 2 * hd, 2 * hd), F32),
                               pltpu.VMEM((bb, c, wa), F32)])


def _mamba_init(ins, outs, scr, *, conv_w):
    conv_ref, s0_ref = ins[:2]
    s_ref = outs[2]
    ubuf_ref = scr[0]
    bb, _, xbc = conv_ref.shape
    pad, hist = 8, conv_w - 1
    s_ref[...] = s0_ref[...]
    for b in range(bb):
        ubuf_ref[b, 0:pad, :] = jnp.zeros((pad, xbc), F32)
        ubuf_ref[b, pad - hist:pad, :] = conv_ref[b]


def _mamba_core(p, ins, outs, scr, tick, *, groups, conv_w):
    conv_ref, s0_ref, cw_ref, cb_ref, dtb_ref, alog_ref, dskip_ref, nw_ref = ins
    o_ref, conv_out_ref, s_ref = outs
    ubuf_ref, inter_ref, intra_ref = scr
    bb, c, wb = o_ref.shape
    rows = bb * c
    heads, hd, ns = s_ref.shape[1], s_ref.shape[2], s_ref.shape[3]
    xbc = conv_ref.shape[-1]
    hpg = heads // groups
    pad = 8
    hist = conv_w - 1

    ys = []
    for b in range(bb):
        u = p[b * c:(b + 1) * c, wb:wb + xbc]
        ext = jnp.concatenate([ubuf_ref[b], u], axis=0)
        y = cb_ref[...] + cw_ref[hist:hist + 1, :] * u
        for i in range(hist):
            y = y + cw_ref[i:i + 1, :] * pltpu.roll(ext, hist - i, axis=0)[pad:pad + c, :]
        conv_out_ref[b] = u[c - hist:c, :]
        ubuf_ref[b] = u[c - pad:c, :]
        ys.append(y)
        tick()
    xc = _silu(ys[0] if bb == 1 else jnp.concatenate(ys, axis=0))
    xs = xc[:, 0:wb]
    bm = xc[:, wb:wb + groups * ns]
    cm = xc[:, wb + groups * ns:wb + 2 * groups * ns]
    z = p[:, 0:wb]
    dt_raw = p[:, wb + xbc:wb + xbc + heads]
    dt = jax.nn.softplus(dt_raw + dtb_ref[...])
    la = dt * (-jnp.exp(alog_ref[...]))

    incl = _tri(c)
    acum = _split_dot(_block_tri(rows, c), la)
    acum_t = _split_dot_tn(la, _block_tri(rows, c, upper=True))
    a_lasts, a_last_rows = _last_rows(acum, bb, c)
    e_tots = [jnp.exp(al) for al in a_lasts]
    stack = jnp.concatenate([dt, jnp.exp(acum), jnp.exp(a_last_rows - acum),
                             jnp.broadcast_to(dskip_ref[...], (8, heads))], axis=0)
    full = _expand_cols(stack, hd)
    ecum_f = full[rows:2 * rows]
    dskip_f = full[3 * rows:3 * rows + 1]
    xd_f = xs * full[0:rows]
    xde_f = xd_f * full[2 * rows:3 * rows]
    colb = _expand_cols(acum, c)
    tick()

    rss = [slice(b * c, (b + 1) * c) for b in range(bb)]
    bgs = [[bm[rs, gi * ns:(gi + 1) * ns].astype(BF16) for gi in range(groups)] for rs in rss]
    cgs = [[cm[rs, gi * ns:(gi + 1) * ns].astype(BF16) for gi in range(groups)] for rs in rss]
    cbs = [[_bdot_nt(cg, bg) for cg, bg in zip(cgb, bgb)] for cgb, bgb in zip(cgs, bgs)]
    tick()
    pairs = [(b, h) for b in range(bb) for h in range(heads)]
    sls = [slice(h * hd, (h + 1) * hd) for _, h in pairs]
    s0s = [s_ref[b, h] for b, h in pairs]
    mats = []
    for b, h in pairs:
        seg = colb[rss[b], h * c:(h + 1) * c] - acum_t[h:h + 1, rss[b]]
        decay = jnp.where(incl, jnp.exp(jnp.where(incl, seg, 0.0)), 0.0)
        mats.append((cbs[b][h // hpg] * decay).astype(BF16))
        if h % 4 == 3:
            tick()
    intra = [_bdot(m, xd_f[rss[b], sl]) for m, (b, _), sl in zip(mats, pairs, sls)]
    tick()
    inter = [_bdot_nt(cgs[b][h // hpg], s0) for (b, h), s0 in zip(pairs, s0s)]
    tick()
    upd = [_bdot_tn(xde_f[rss[b], sl], bgs[b][h // hpg]) for (b, h), sl in zip(pairs, sls)]
    tick()
    for i, ((b, h), sl) in enumerate(zip(pairs, sls)):
        s_ref[b, h] = s0s[i] * e_tots[b][:, h:h + 1] + upd[i]
        intra_ref[b, :, sl] = intra[i]
        inter_ref[b, :, sl] = inter[i]

    y_all = intra_ref[...].reshape(rows, wb) + inter_ref[...].reshape(rows, wb) * ecum_f + dskip_f * xs
    yv = y_all * _silu(z)
    gw = wb // groups
    for gi in range(groups):
        sl = slice(gi * gw, (gi + 1) * gw)
        yg = yv[:, sl]
        ms = jnp.mean(yg * yg, axis=-1, keepdims=True)
        o_ref[:, :, sl] = (yg * lax.rsqrt(ms + GROUP_RMS_EPS) * nw_ref[:, sl]).reshape(bb, c, gw).astype(o_ref.dtype)


MAMBA_POINTS = 16


def _mamba_call(x, nw, sc, sh, w_in, conv_prev, s0, prm, groups):
    b, t, _ = x.shape
    heads, hd, ns = s0.shape[1:]
    wb = heads * hd
    hist, xbc = conv_prev.shape[1:]
    vec = lambda a: a.reshape(1, -1)
    consts = [prm["conv_w"], vec(prm["conv_b"]), vec(prm["dt_bias"]), vec(prm["a_log"]),
              vec(prm["d_skip"]), vec(prm["norm_b_w"])]
    seq = lambda bb, shape: pl.BlockSpec((bb,) + shape, lambda i, j: (i,) + (0,) * len(shape))
    return _fused_call(
        "mamba2_mixer", functools.partial(_mamba_init, conv_w=hist + 1),
        functools.partial(_mamba_core, groups=groups, conv_w=hist + 1), MAMBA_POINTS,
        x, nw, sc, sh, w_in, MAMBA_CHUNK,
        ins=[conv_prev, s0, *consts],
        in_specs=lambda bb, c: [seq(bb, (hist, xbc)), seq(bb, s0.shape[1:])] + [_const_spec(a.shape) for a in consts],
        out_shapes=[jax.ShapeDtypeStruct((b, t, wb), BF16),
                    jax.ShapeDtypeStruct(conv_prev.shape, F32),
                    jax.ShapeDtypeStruct(s0.shape, F32)],
        out_specs=lambda bb, c: [pl.BlockSpec((bb, c, wb), lambda i, j: (i, j, 0)),
                                 seq(bb, (hist, xbc)), seq(bb, s0.shape[1:])],
        scratch=lambda bb, c: [pltpu.VMEM((bb, 8, xbc), F32), pltpu.VMEM((bb, c, wb), F32),
                               pltpu.VMEM((bb, c, wb), F32)])


HGRN_FINE = 2


def _hgrn_sum_matrix(c):
    t = np.arange(c)[:, None]
    j = np.arange(c)[None, :]
    blocks = [(j <= t)]
    m = min(HGRN_FINE, c // 2)
    while m >= 1:
        mid = (t // (2 * m)) * (2 * m) + m - 1
        right = (t % (2 * m)) >= m
        blocks.append(np.where(right, (j > mid) & (j <= t), (j > t) & (j <= mid)))
        m //= 2
    return np.concatenate(blocks, axis=0).astype(np.float32)


def _hgrn_init(ins, outs, scr):
    s0_ref = ins[0]
    (st_ref,) = scr
    bb, heads = s0_ref.shape[:2]
    for b in range(bb):
        for h in range(heads):
            st_ref[b, h] = s0_ref[b, h].T


def _hgrn_core(p, ins, outs, scr, tick, *, layer):
    s0_ref, sums_ref, lbp_ref, nw_ref = ins
    o_ref, s_ref = outs
    (st_ref,) = scr
    bb, cs, wc = o_ref.shape
    c = min(HGRN_CHUNK, cs)
    nsub = cs // c
    rows, cols = p.shape
    heads, dk, dv = s0_ref.shape[1:]
    nchunks = pl.num_programs(1)

    lbp = lbp_ref[...]
    e = jnp.exp(lbp - jnp.max(lbp, axis=0, keepdims=True))
    soft = e / jnp.sum(e, axis=0, keepdims=True)
    lb = soft[0:1, :]
    for i in range(1, layer + 1):
        lb = lb + soft[i:i + 1, :]
    lb = lb - soft[0:1, :]

    q = _silu(p[:, 0:wc])
    f = p[:, wc:2 * wc]
    v = p[:, 2 * wc:3 * wc]
    gate = p[:, 3 * wc:4 * wc]
    log_f = jnp.log(lb + (1.0 - lb) * jax.nn.sigmoid(f))
    k = (1.0 - lb) * jax.nn.sigmoid(-f)
    qk = q * k

    row = lax.broadcasted_iota(jnp.int32, (c, 1), 0)
    rr = lax.broadcasted_iota(jnp.int32, (c, c), 0)
    cc = lax.broadcasted_iota(jnp.int32, (c, c), 1)
    eye = rr == cc
    slabs = [(b, ci) for b in range(bb) for ci in range(nsub)]
    rss = [slice(b * cs + ci * c, b * cs + (ci + 1) * c) for b, ci in slabs]
    qes, kes, f_tots, levels = [], [], [], []
    for rs in rss:
        sums = _split_dot(sums_ref[...], log_f[rs])
        bcum = sums[0:c]
        tot = bcum[c - 1:c, :]
        tick()
        qes.append((q[rs] * jnp.exp(bcum)).astype(BF16))
        kes.append((k[rs] * jnp.exp(tot - bcum)).astype(BF16))
        f_tots.append(jnp.exp(tot))
        lv = []
        m = c // 2
        i = 1
        while m >= 1:
            if m > HGRN_FINE:
                mids = [jnp.broadcast_to(bcum[r0 + m - 1:r0 + m, :], (2 * m, wc)) for r0 in range(0, c, 2 * m)]
                diff = bcum - (mids[0] if len(mids) == 1 else jnp.concatenate(mids, axis=0))
                z = jnp.where((row & (2 * m - 1)) >= m, diff, -diff)
            else:
                z = sums[i * c:(i + 1) * c]
                i += 1
            wgt = jnp.exp(z)
            lv.append(((q[rs] * wgt).astype(BF16), (k[rs] * wgt).astype(BF16),
                       (rr > cc) & ((rr ^ cc) >= m) & ((rr ^ cc) < 2 * m)))
            m //= 2
            tick()
        levels.append(lv)

    units = [(si, h) for si in range(len(slabs)) for h in range(heads)]
    sls = [slice(h * dk, (h + 1) * dk) for _, h in units]
    atts = [jnp.where(eye, jnp.sum(qk[rss[si], sl], axis=-1, keepdims=True), 0.0) for (si, _), sl in zip(units, sls)]
    for li in range(len(levels[0])):
        atts = [att + jnp.where(levels[si][li][2], _bdot_nt(levels[si][li][0][:, sl], levels[si][li][1][:, sl]), 0.0)
                for att, (si, _), sl in zip(atts, units, sls)]
        tick()
    vhs = [v[rss[si], sl].astype(BF16) for (si, _), sl in zip(units, sls)]
    intra = [_bdot(att, vh) for att, vh in zip(atts, vhs)]
    tick()
    upd = [_bdot_tn(vh, kes[si][:, sl]) for vh, (si, _), sl in zip(vhs, units, sls)]
    tick()

    for b in range(bb):
        sts = [st_ref[b, h] for h in range(heads)]
        for ci in range(nsub):
            si = b * nsub + ci
            for h in range(heads):
                i = si * heads + h
                sl = sls[i]
                o = intra[i] + _bdot_nt(qes[si][:, sl], sts[h])
                sts[h] = sts[h] * f_tots[si][:, sl] + upd[i]
                ms = jnp.mean(o * o, axis=-1, keepdims=True)
                o_ref[b, ci * c:(ci + 1) * c, sl] = (o * lax.rsqrt(ms + GROUP_RMS_EPS) * nw_ref[:, sl]
                                                     * _silu(gate[rss[si], sl])).astype(o_ref.dtype)
            tick()
        for h in range(heads):
            st_ref[b, h] = sts[h]

    @pl.when(pl.program_id(1) == nchunks - 1)
    def _():
        for b in range(bb):
            for h in range(heads):
                s_ref[b, h] = st_ref[b, h].T


HGRN_POINTS = 40


def _hgrn_call(x, nw, sc, sh, w_in, s0, lb_param, norm_w, layer):
    b, t, _ = x.shape
    heads, dk, dv = s0.shape[1:]
    assert dk == dv
    wc = heads * dk
    sums = jnp.asarray(_hgrn_sum_matrix(math.gcd(HGRN_CHUNK, t)), dtype=BF16)
    ins = [s0, sums, lb_param, norm_w.reshape(1, wc)]
    seq = lambda bb, shape: pl.BlockSpec((bb,) + shape, lambda i, j: (i,) + (0,) * len(shape))
    return _fused_call(
        "hgrn2_mixer", _hgrn_init, functools.partial(_hgrn_core, layer=layer), HGRN_POINTS,
        x, nw, sc, sh, w_in, HGRN_CHUNK * HGRN_GROUP,
        ins=ins,
        in_specs=lambda bb, c: [seq(bb, s0.shape[1:])] + [_const_spec(a.shape) for a in ins[1:]],
        out_shapes=[jax.ShapeDtypeStruct((b, t, wc), BF16), jax.ShapeDtypeStruct(s0.shape, F32)],
        out_specs=lambda bb, c: [pl.BlockSpec((bb, c, wc), lambda i, j: (i, j, 0)), seq(bb, s0.shape[1:])],
        scratch=lambda bb, c: [pltpu.VMEM((bb, heads, dv, dk), F32)])


def _round_up(n, m):
    return -(-n // m) * m


def _prepare_weights(w):
    names = ("w_in_ab", "w_out_ab", "w_in_c", "w_out_c", "w_gate", "w_up", "w_down")
    return {n: w[n].astype(BF16) for n in names}


def _trunk(x, mod, st_rwkv, st_shift, st_ssm, st_conv, st_hgrn, w, wb):
    depth = mod.shape[0]
    b, t, d = x.shape
    groups = (st_conv.shape[-1] - st_ssm.shape[2] * st_ssm.shape[3]) // (2 * st_ssm.shape[4])
    new_rwkv, new_shift, new_ssm, new_conv, new_hgrn = [], [], [], [], []
    for layer in range(depth):
        j = layer // 2
        sh_m, sc_m, g_m, sh_f, sc_f, g_f = (mod[layer, :, None, i * d:(i + 1) * d] for i in range(6))
        nw_mix = w["norm_mix_w"][layer].reshape(1, d)
        nw_ffn = w["norm_ffn_w"][layer].reshape(1, d)
        if layer % 2 == 0:
            prm = dict(mu=w["mu_a"][j], w0=w["w0"][j], w2=w["w2"][j], a0=w["a0"][j], a2=w["a2"][j],
                       g2=w["g2"][j], k_k=w["k_k"][j], k_a=w["k_a"][j], r_k=w["r_k"][j],
                       lnx_w=w["lnx_w"][j], lnx_b=w["lnx_b"][j])
            a_cols = w["mu_a"].shape[1]
            oa, shift_new, rwkv_new = _rwkv_call(x, nw_mix, sc_m, sh_m, (wb["w_in_ab"], j, 0, a_cols),
                                                 st_shift[:, j], st_rwkv[:, j], prm)
            prm_b = dict(conv_w=w["conv_w"][j], conv_b=w["conv_b"][j], dt_bias=w["dt_bias"][j],
                         a_log=w["a_log"][j], d_skip=w["d_skip"][j], norm_b_w=w["norm_b_w"][j])
            b_in = (wb["w_in_ab"], j, a_cols, w["w_in_ab"].shape[2] - a_cols)
            ob, conv_new, ssm_new = _mamba_call(x, nw_mix, sc_m, sh_m, b_in,
                                                st_conv[:, j], st_ssm[:, j], prm_b, groups)
            mixes, wout = [oa, ob], (wb["w_out_ab"], j)
            new_rwkv.append(rwkv_new)
            new_shift.append(shift_new[:, 0])
            new_ssm.append(ssm_new)
            new_conv.append(conv_new)
        else:
            oc, hgrn_new = _hgrn_call(x, nw_mix, sc_m, sh_m, (wb["w_in_c"], j, 0, w["w_in_c"].shape[2]),
                                      st_hgrn[:, j], w["lb_param"], w["norm_c_w"][j], j)
            mixes, wout = [oc], (wb["w_out_c"], j)
            new_hgrn.append(hgrn_new)
        final_w = w["norm_out_w"].reshape(1, d) if layer == depth - 1 else None
        x = _post_call(x, mixes, wout, g_m, nw_ffn, sc_f, sh_f, g_f,
                       (wb["w_gate"], layer), (wb["w_up"], layer), (wb["w_down"], layer), final_w)
    return (x, jnp.stack(new_rwkv, axis=1), jnp.stack(new_shift, axis=1), jnp.stack(new_ssm, axis=1),
            jnp.stack(new_conv, axis=1), jnp.stack(new_hgrn, axis=1))


def _run(x_prompt, x_sample, state_rwkv, state_rwkv_shift, state_ssm, state_conv, state_hgrn,
         c_prompt, c_sample, w):
    bp, bs = x_prompt.shape[0], x_sample.shape[0]
    rows = _round_up(bp + bs, 8)
    c_all = jnp.pad(jnp.concatenate([c_prompt, c_sample], axis=0), ((0, rows - bp - bs), (0, 0)))
    mod = _ada_call(c_all, w["ada_w"], w["ada_b"])
    wb = _prepare_weights(w)
    zeros = lambda s: jnp.zeros((bp,) + s.shape[1:], F32)
    outs_p = _trunk(x_prompt, mod[:, :bp], zeros(state_rwkv), zeros(state_rwkv_shift), zeros(state_ssm),
                    zeros(state_conv), zeros(state_hgrn), w, wb)
    outs_s = _trunk(x_sample, mod[:, bp:bp + bs], state_rwkv, state_rwkv_shift, state_ssm,
                    state_conv, state_hgrn, w, wb)
    return (outs_p[0], outs_s[0]) + outs_p[1:] + outs_s[1:]


def kernel(x_prompt, x_sample, state_rwkv, state_rwkv_shift, state_ssm, state_conv, state_hgrn, c_prompt, c_sample, norm_mix_w, norm_ffn_w, norm_out_w, ada_w, ada_b, w_in_ab, w_out_ab, mu_a, w0, w2, a0, a2, g2, k_k, k_a, r_k, lnx_w, lnx_b, conv_w, conv_b, dt_bias, a_log, d_skip, norm_b_w, w_in_c, w_out_c, lb_param, norm_c_w, w_gate, w_up, w_down):
    w = dict(norm_mix_w=norm_mix_w, norm_ffn_w=norm_ffn_w, norm_out_w=norm_out_w, ada_w=ada_w, ada_b=ada_b,
             w_in_ab=w_in_ab, w_out_ab=w_out_ab, mu_a=mu_a, w0=w0, w2=w2, a0=a0, a2=a2, g2=g2, k_k=k_k,
             k_a=k_a, r_k=r_k, lnx_w=lnx_w, lnx_b=lnx_b, conv_w=conv_w, conv_b=conv_b, dt_bias=dt_bias,
             a_log=a_log, d_skip=d_skip, norm_b_w=norm_b_w, w_in_c=w_in_c, w_out_c=w_out_c,
             lb_param=lb_param, norm_c_w=norm_c_w, w_gate=w_gate, w_up=w_up, w_down=w_down)
    return _run(x_prompt, x_sample, state_rwkv, state_rwkv_shift, state_ssm, state_conv, state_hgrn,
                c_prompt, c_sample, w)
```
